```python
import jax, jax.numpy as jnp
from jax import lax
import numpy as np

D_MODEL = 2048
BATCH = 8
SEQ = 4096
DEPTH = 4

N_A_LAYERS = DEPTH // 2
N_B_LAYERS = DEPTH - N_A_LAYERS
HG_HEADS = 16
HG_DK = D_MODEL // HG_HEADS
HG_DV = D_MODEL // HG_HEADS
HG_CHUNK = 64
MLA_HEADS = 16
MLA_NOPE = 128
MLA_ROPE = 64
MLA_V = 128
MLA_QK = MLA_NOPE + MLA_ROPE
Q_LORA = 512
KV_LORA = 512
ROPE_THETA = 10000.0
Q_BLOCK = 128
D_FF = 4 * D_MODEL
N_MOD = 6
EPS = 1e-6

kernel_name = "hybrid_hgrn2_mla_yoco_adaln"


def rms_norm(x, gain):
    xf = x.astype(jnp.float32)
    y = xf * lax.rsqrt(jnp.mean(xf * xf, axis=-1, keepdims=True) + EPS)
    return (y * gain.astype(jnp.float32)).astype(x.dtype)


def modulate(h, shift, scale):
    return h * (1 + scale[:, None, :]) + shift[:, None, :]


def rope_tables(positions):
    inv_freq = 1.0 / (ROPE_THETA ** (jnp.arange(0, MLA_ROPE, 2, dtype=jnp.float32) / MLA_ROPE))
    ang = positions.astype(jnp.float32)[..., None] * inv_freq
    return jnp.cos(ang)[:, :, None, :], jnp.sin(ang)[:, :, None, :]


def apply_rope(t, cos, sin):
    tf = t.astype(jnp.float32)
    t1, t2 = jnp.split(tf, 2, axis=-1)
    out = jnp.concatenate([t1 * cos - t2 * sin, t2 * cos + t1 * sin], axis=-1)
    return out.astype(t.dtype)


def squared_relu_mlp(h, w1, w2):
    a = jax.nn.relu(h @ w1)
    return (a * a) @ w2


def hgrn2_mixer(h, w_in, lb, o_gain, w_out):
    B, S, D = h.shape
    f32 = jnp.float32
    proj = h @ w_in
    q, fz, i, g = jnp.split(proj, 4, axis=-1)
    fz = fz.astype(f32)
    lbf = lb.astype(f32)
    log_f = jnp.logaddexp(jnp.log(lbf), jnp.log1p(-lbf) + jax.nn.log_sigmoid(fz))
    k = (1.0 - lbf) * jax.nn.sigmoid(-fz)

    n_chunks = S // HG_CHUNK

    def to_chunks(t, d):
        t = t.astype(f32).reshape(B, n_chunks, HG_CHUNK, HG_HEADS, d)
        return t.transpose(1, 0, 3, 2, 4)

    qc = to_chunks(q, HG_DK) * (HG_DK ** -0.5)
    kc = to_chunks(k, HG_DK)
    vc = to_chunks(i, HG_DV)
    lfc = to_chunks(log_f, HG_DK)
    tri = jnp.tril(jnp.ones((HG_CHUNK, HG_CHUNK), dtype=bool))

    def step(state, inp):
        q_, k_, v_, lf_ = inp
        b = jnp.cumsum(lf_, axis=2)
        diff = b[:, :, :, None, :] - b[:, :, None, :, :]
        decay = jnp.where(tri[:, :, None], jnp.exp(jnp.minimum(diff, 0.0)), 0.0)
        scores = jnp.einsum('bhtk,bhtsk,bhsk->bhts', q_, decay, k_)
        o = scores @ v_ + jnp.einsum('bhtk,bhkv->bhtv', q_ * jnp.exp(b), state)
        b_last = b[:, :, -1:, :]
        state = (jnp.exp(b_last[:, :, 0, :, None]) * state
                 + jnp.einsum('bhsk,bhsv->bhkv', k_ * jnp.exp(b_last - b), v_))
        return state, o

    state0 = jnp.zeros((B, HG_HEADS, HG_DK, HG_DV), f32)
    _, o = lax.scan(step, state0, (qc, kc, vc, lfc))
    o = o.transpose(1, 0, 3, 2, 4).reshape(B, S, HG_HEADS, HG_DV)
    o = rms_norm(o, o_gain).reshape(B, S, D)
    o = o.astype(h.dtype) * jax.nn.silu(g)
    return o @ w_out


def mla_shared_kv(x, cond, kv_ada_w, kv_ada_b, kv_in_norm, w_dkv, kv_lat_norm, w_ukv, k_gain, cos, sin):
    B, S, _ = x.shape
    shift, scale = jnp.split(cond @ kv_ada_w + kv_ada_b, 2, axis=-1)
    h = modulate(rms_norm(x, kv_in_norm), shift, scale)
    ckv = h @ w_dkv
    c_lat = rms_norm(ckv[..., :KV_LORA], kv_lat_norm)
    k_pe = ckv[..., KV_LORA:]
    kv = (c_lat @ w_ukv).reshape(B, S, MLA_HEADS, MLA_NOPE + MLA_V)
    k_nope, v = kv[..., :MLA_NOPE], kv[..., MLA_NOPE:]
    k = jnp.concatenate([k_nope, jnp.broadcast_to(k_pe[:, :, None, :], (B, S, MLA_HEADS, MLA_ROPE))], axis=-1)
    k = rms_norm(k, k_gain)
    k = jnp.concatenate([k[..., :MLA_NOPE], apply_rope(k[..., MLA_NOPE:], cos, sin)], axis=-1)
    return k.transpose(0, 2, 1, 3), v.transpose(0, 2, 1, 3)


def mla_mixer(h, k, v, w_dq, q_lat_norm, w_uq, q_gain, w_o, cos, sin):
    B, S, _ = h.shape
    cq = rms_norm(h @ w_dq, q_lat_norm)
    q = (cq @ w_uq).reshape(B, S, MLA_HEADS, MLA_QK)
    q = rms_norm(q, q_gain)
    q = jnp.concatenate([q[..., :MLA_NOPE], apply_rope(q[..., MLA_NOPE:], cos, sin)], axis=-1)
    n_blocks = S // Q_BLOCK
    qb = q.reshape(B, n_blocks, Q_BLOCK, MLA_HEADS, MLA_QK).transpose(1, 0, 3, 2, 4)
    starts = jnp.arange(n_blocks, dtype=jnp.int32) * Q_BLOCK
    key_idx = jnp.arange(S, dtype=jnp.int32)
    sm_scale = MLA_QK ** -0.5

    def attend(args):
        q_blk, start = args
        s = jnp.einsum('bhqd,bhkd->bhqk', q_blk, k, preferred_element_type=jnp.float32) * sm_scale
        q_idx = start + jnp.arange(Q_BLOCK, dtype=jnp.int32)
        s = jnp.where(key_idx[None, :] <= q_idx[:, None], s, -jnp.inf)
        p = jax.nn.softmax(s, axis=-1)
        return jnp.einsum('bhqk,bhkd->bhqd', p.astype(v.dtype), v)

    o = lax.map(attend, (qb, starts))
    o = o.transpose(1, 0, 3, 2, 4).reshape(B, S, MLA_HEADS * MLA_V)
    return o @ w_o


def setup_inputs(seed: int = 0) -> dict:
    key = jax.random.key(seed)
    ks = jax.random.split(key, 32)
    f32 = jnp.float32

    def nrm(k, shape, scale):
        return jax.random.normal(k, shape, f32) * scale

    def gain(k, shape):
        return 1.0 + 0.02 * jax.random.normal(k, shape, f32)

    D = D_MODEL
    offset = jax.random.randint(ks[2], (BATCH, 1), 0, 1024, dtype=jnp.int32)
    positions = (offset + jnp.arange(SEQ, dtype=jnp.int32)[None, :]).astype(jnp.int32)
    return {
        "x": nrm(ks[0], (BATCH, SEQ, D), 1.0),
        "c": nrm(ks[1], (BATCH, D), 1.0),
        "positions": positions,
        "ada_w": nrm(ks[3], (DEPTH, D, N_MOD * D), 0.5 * D ** -0.5),
        "ada_b": nrm(ks[4], (DEPTH, N_MOD * D), 0.02),
        "norm_mix": gain(ks[5], (DEPTH, D)),
        "norm_mlp": gain(ks[6], (DEPTH, D)),
        "mlp_w1": nrm(ks[7], (DEPTH, D, D_FF), D ** -0.5),
        "mlp_w2": nrm(ks[8], (DEPTH, D_FF, D), D_FF ** -0.5),
        "hg_w_in": nrm(ks[9], (N_A_LAYERS, D, 4 * D), D ** -0.5),
        "hg_lower": nrm(ks[10], (N_A_LAYERS + 1, HG_HEADS * HG_DK), 0.1),
        "hg_o_norm": gain(ks[11], (N_A_LAYERS, HG_DV)),
        "hg_w_out": nrm(ks[12], (N_A_LAYERS, D, D), D ** -0.5),
        "kv_ada_w": nrm(ks[13], (D, 2 * D), 0.5 * D ** -0.5),
        "kv_ada_b": nrm(ks[14], (2 * D,), 0.02),
        "kv_in_norm": gain(ks[15], (D,)),
        "mla_w_dkv": nrm(ks[16], (D, KV_LORA + MLA_ROPE), D ** -0.5),
        "mla_kv_norm": gain(ks[17], (KV_LORA,)),
        "mla_w_ukv": nrm(ks[18], (KV_LORA, MLA_HEADS * (MLA_NOPE + MLA_V)), KV_LORA ** -0.5),
        "mla_k_norm": gain(ks[19], (MLA_QK,)),
        "mla_w_dq": nrm(ks[20], (N_B_LAYERS, D, Q_LORA), D ** -0.5),
        "mla_q_lat_norm": gain(ks[21], (N_B_LAYERS, Q_LORA)),
        "mla_w_uq": nrm(ks[22], (N_B_LAYERS, Q_LORA, MLA_HEADS * MLA_QK), Q_LORA ** -0.5),
        "mla_q_norm": gain(ks[23], (N_B_LAYERS, MLA_QK)),
        "mla_w_o": nrm(ks[24], (N_B_LAYERS, MLA_HEADS * MLA_V, D), (MLA_HEADS * MLA_V) ** -0.5),
    }


def reference(x, c, positions, ada_w, ada_b, norm_mix, norm_mlp, mlp_w1, mlp_w2,
              hg_w_in, hg_lower, hg_o_norm, hg_w_out,
              kv_ada_w, kv_ada_b, kv_in_norm, mla_w_dkv, mla_kv_norm, mla_w_ukv, mla_k_norm,
              mla_w_dq, mla_q_lat_norm, mla_w_uq, mla_q_norm, mla_w_o):
    cond = jax.nn.silu(c)
    cos, sin = rope_tables(positions)
    lb_all = jnp.cumsum(jax.nn.softmax(hg_lower.astype(jnp.float32), axis=0), axis=0)[:N_A_LAYERS]
    k_shared = None
    v_shared = None
    for layer in range(DEPTH):
        mod = cond @ ada_w[layer] + ada_b[layer]
        sh1, sc1, g1, sh2, sc2, g2 = jnp.split(mod, N_MOD, axis=-1)
        if layer == N_A_LAYERS:
            k_shared, v_shared = mla_shared_kv(x, cond, kv_ada_w, kv_ada_b, kv_in_norm, mla_w_dkv,
                                               mla_kv_norm, mla_w_ukv, mla_k_norm, cos, sin)
        h = modulate(rms_norm(x, norm_mix[layer]), sh1, sc1)
        if layer < N_A_LAYERS:
            y = hgrn2_mixer(h, hg_w_in[layer], lb_all[layer], hg_o_norm[layer], hg_w_out[layer])
        else:
            j = layer - N_A_LAYERS
            y = mla_mixer(h, k_shared, v_shared, mla_w_dq[j], mla_q_lat_norm[j], mla_w_uq[j],
                          mla_q_norm[j], mla_w_o[j], cos, sin)
        x = x + g1[:, None, :] * y.astype(x.dtype)
        h = modulate(rms_norm(x, norm_mlp[layer]), sh2, sc2)
        x = x + g2[:, None, :] * squared_relu_mlp(h, mlp_w1[layer], mlp_w2[layer]).astype(x.dtype)
    return x
```

```python
import functools

import jax
import jax.numpy as jnp
from jax import lax
from jax.experimental import pallas as pl
from jax.experimental.pallas import tpu as pltpu

F32 = jnp.float32
BF16 = jnp.bfloat16

EPS = 1e-6
LANES = 128
ROPE = 64
QK = LANES + ROPE
QK_PAD = 2 * LANES
CHUNK = 64
ROPE_THETA = 10000.0
VMEM_LIMIT = 56 * 1024 * 1024


def _params(*sem):
    return pltpu.CompilerParams(dimension_semantics=sem, vmem_limit_bytes=VMEM_LIMIT)


def _tile(n, pref):
    t = min(n, pref)
    while n % t:
        t -= LANES
    assert t > 0 and n % t == 0, (n, pref)
    return t


def _norm_mod(x, gain, scale, shift):
    r = lax.rsqrt(jnp.mean(x * x, axis=-1, keepdims=True) + EPS)
    return (x * r) * (gain * (1.0 + scale)) + shift


def _ada_kernel(c_ref, w_ref, b_ref, o_ref):
    c = c_ref[...]
    cond = (c / (1.0 + jnp.exp(-c))).astype(BF16)
    o_ref[...] = jnp.dot(cond, w_ref[...].astype(BF16), preferred_element_type=F32) + b_ref[...]


def _ada(c, w, b):
    n_layers, d, n = w.shape
    bsz = c.shape[0]
    tn = _tile(n, 1024)
    return pl.pallas_call(
        _ada_kernel,
        grid=(n_layers, n // tn),
        in_specs=[
            pl.BlockSpec((bsz, d), lambda l, j: (0, 0)),
            pl.BlockSpec((None, d, tn), lambda l, j: (l, 0, j)),
            pl.BlockSpec((None, 1, tn), lambda l, j: (l, 0, j)),
        ],
        out_specs=pl.BlockSpec((None, bsz, tn), lambda l, j: (l, 0, j)),
        out_shape=jax.ShapeDtypeStruct((n_layers, bsz, n), F32),
        compiler_params=_params("parallel", "parallel"),
        name="ada",
    )(c, w, b.reshape(n_layers, 1, n))


def _mlp_kernel(x_ref, gain_ref, shift_ref, scale_ref, gate_ref, w1_ref, w2_ref, o_ref, h_ref):
    j = pl.program_id(1)

    @pl.when(j == 0)
    def _():
        h_ref[...] = _norm_mod(x_ref[...], gain_ref[...], scale_ref[...], shift_ref[...]).astype(BF16)
        o_ref[...] = jnp.zeros_like(o_ref)

    a = jnp.maximum(jnp.dot(h_ref[...], w1_ref[...], preferred_element_type=F32), 0.0)
    o_ref[...] += jnp.dot((a * a).astype(BF16), w2_ref[...], preferred_element_type=F32)

    @pl.when(j == pl.num_programs(1) - 1)
    def _():
        o_ref[...] = x_ref[...] + gate_ref[...] * o_ref[...]


def _mlp(x, gain, mod, w1, w2, seq):
    t, d = x.shape
    f = w1.shape[1]
    tm = _tile(seq, 512)
    tf = _tile(f, 1024)
    per_b = seq // tm
    mod_spec = lambda k: pl.BlockSpec((None, 1, d), lambda i, j: (i // per_b, 0, k))
    return pl.pallas_call(
        _mlp_kernel,
        grid=(t // tm, f // tf),
        in_specs=[
            pl.BlockSpec((tm, d), lambda i, j: (i, 0)),
            pl.BlockSpec((1, d), lambda i, j: (0, 0)),
            mod_spec(3), mod_spec(4), mod_spec(5),
            pl.BlockSpec((d, tf), lambda i, j: (0, j)),
            pl.BlockSpec((tf, d), lambda i, j: (j, 0)),
        ],
        out_specs=pl.BlockSpec((tm, d), lambda i, j: (i, 0)),
        out_shape=jax.ShapeDtypeStruct((t, d), F32),
        scratch_shapes=[pltpu.VMEM((tm, d), BF16)],
        compiler_params=_params("parallel", "arbitrary"),
        name="mlp",
    )(x, gain.reshape(1, d), mod, mod, mod, w1, w2)


def _proj_res_kernel(y_ref, w_ref, x_ref, gate_ref, o_ref):
    o_ref[...] = x_ref[...] + gate_ref[...] * jnp.dot(y_ref[...], w_ref[...], preferred_element_type=F32)


def _proj_res(y, w, x, mod, gate_idx, seq):
    t, d = x.shape
    k = y.shape[1]
    tm = _tile(seq, 512)
    tn = _tile(d, 1024)
    per_b = seq // tm
    return pl.pallas_call(
        _proj_res_kernel,
        grid=(t // tm, d // tn),
        in_specs=[
            pl.BlockSpec((tm, k), lambda i, j: (i, 0)),
            pl.BlockSpec((k, tn), lambda i, j: (0, j)),
            pl.BlockSpec((tm, tn), lambda i, j: (i, j)),
            pl.BlockSpec((None, 1, tn), lambda i, j: (i // per_b, 0, gate_idx * (d // tn) + j)),
        ],
        out_specs=pl.BlockSpec((tm, tn), lambda i, j: (i, j)),
        out_shape=jax.ShapeDtypeStruct((t, d), F32),
        compiler_params=_params("parallel", "parallel"),
        name="proj_res",
    )(y, w, x, mod)


def _hg_in_kernel(x_ref, gain_ref, shift_ref, scale_ref, llb_ref, l1mlb_ref, omlb_ref,
                  wq_ref, wf_ref, wi_ref, wg_ref,
                  q_ref, k_ref, lf_ref, v_ref, sg_ref, h_ref, *, heads_per_step, q_scale):
    @pl.when(pl.program_id(1) == 0)
    def _():
        h_ref[...] = _norm_mod(x_ref[...], gain_ref[...], scale_ref[...], shift_ref[...]).astype(BF16)

    h = h_ref[...]
    q = jnp.dot(h, wq_ref[...], preferred_element_type=F32) * q_scale
    z = jnp.dot(h, wf_ref[...], preferred_element_type=F32)
    v = jnp.dot(h, wi_ref[...], preferred_element_type=F32)
    g = jnp.dot(h, wg_ref[...], preferred_element_type=F32)

    e = jnp.exp(-jnp.abs(z))
    log_sig = jnp.minimum(z, 0.0) - jnp.log1p(e)
    a = llb_ref[...]
    c = l1mlb_ref[...] + log_sig
    lf = jnp.maximum(a, c) + jnp.log1p(jnp.exp(-jnp.abs(a - c)))
    key = omlb_ref[...] * (jnp.where(z >= 0.0, e, 1.0) / (1.0 + e))
    sg = g / (1.0 + jnp.exp(-g))

    for hh in range(heads_per_step):
        cs = slice(hh * LANES, (hh + 1) * LANES)
        q_ref[hh] = q[:, cs].astype(BF16)
        k_ref[hh] = key[:, cs].astype(BF16)
        lf_ref[hh] = lf[:, cs]
        v_ref[hh] = v[:, cs].astype(BF16)
        sg_ref[hh] = sg[:, cs].astype(BF16)


def _hg_in(x, gain, mod, lb, w_in, bsz, seq):
    t, d = x.shape
    n_heads = d // LANES
    tm = _tile(seq, 1024)
    tn = _tile(d, 2 * LANES)
    hps = tn // LANES
    nj = d // tn
    per_b = seq // tm
    mod_spec = lambda k: pl.BlockSpec((None, 1, d), lambda i, j: (i // per_b, 0, k))
    vec_spec = pl.BlockSpec((1, tn), lambda i, j: (0, j))
    w_spec = lambda grp: pl.BlockSpec((d, tn), lambda i, j: (0, grp * nj + j))
    out_spec = pl.BlockSpec((None, hps, tm, LANES), lambda i, j: (i // per_b, j, i % per_b, 0))
    hm = lambda dt: jax.ShapeDtypeStruct((bsz, n_heads, seq, LANES), dt)
    lbf = lb.reshape(1, d)
    kern = functools.partial(_hg_in_kernel, heads_per_step=hps, q_scale=float(LANES) ** -0.5)
    return pl.pallas_call(
        kern,
        grid=(t // tm, nj),
        in_specs=[
            pl.BlockSpec((tm, d), lambda i, j: (i, 0)),
            pl.BlockSpec((1, d), lambda i, j: (0, 0)),
            mod_spec(0), mod_spec(1),
            vec_spec, vec_spec, vec_spec,
            w_spec(0), w_spec(1), w_spec(2), w_spec(3),
        ],
        out_specs=[out_spec] * 5,
        out_shape=[hm(BF16), hm(BF16), hm(F32), hm(BF16), hm(BF16)],
        scratch_shapes=[pltpu.VMEM((tm, d), BF16)],
        compiler_params=_params("parallel", "arbitrary"),
        name="hg_in",
    )(x, gain.reshape(1, d), mod, mod, jnp.log(lbf), jnp.log1p(-lbf), 1.0 - lbf, w_in, w_in, w_in, w_in)


def _nt_dot(a, b):
    return lax.dot_general(a, b, (((1,), (1,)), ((), ())), preferred_element_type=F32)


def _tn_dot(a, b):
    return lax.dot_general(a, b, (((0,), (0,)), ((), ())), preferred_element_type=F32)


def _chunk_scores(q, k, b, lf):
    row = lax.broadcasted_iota(jnp.int32, (CHUNK, LANES), 0)
    ti = lax.broadcasted_iota(jnp.int32, (CHUNK, CHUNK), 0)
    si = lax.broadcasted_iota(jnp.int32, (CHUNK, CHUNK), 1)
    scores = jnp.where(ti == si, _nt_dot(q.astype(BF16), k.astype(BF16)), 0.0)
    start = b - lf
    c = 1
    while c < CHUNK:
        later = (row & c) != 0
        bound = jnp.where(later, start, pltpu.roll(start, CHUNK - c, axis=0))
        e = jnp.exp(jnp.where(later, b - bound, bound - b))
        m = jnp.where(later, q, k) * e
        qs = jnp.where(later, m, 0.0).astype(BF16)
        ks = jnp.where(later, 0.0, m).astype(BF16)
        same_parent = (ti & -(2 * c)) == (si & -(2 * c))
        scores = scores + jnp.where(same_parent, _nt_dot(qs, ks), 0.0)
        if 2 * c < CHUNK:
            start = jnp.where(later, pltpu.roll(start, c, axis=0), start)
        c *= 2
    return scores


def _rec_kernel(q_ref, k_ref, lf_ref, v_ref, sg_ref, og_ref, o_ref, st_ref, *, n_chunks):
    @pl.when(pl.program_id(2) == 0)
    def _():
        st_ref[...] = jnp.zeros_like(st_ref)

    row = lax.broadcasted_iota(jnp.int32, (CHUNK, LANES), 0)
    og = og_ref[...]
    state_t = st_ref[...]
    for ci in range(n_chunks):
        rs = slice(ci * CHUNK, (ci + 1) * CHUNK)
        lf = lf_ref[rs, :]
        q = q_ref[rs, :].astype(F32)
        k = k_ref[rs, :].astype(F32)
        v = v_ref[rs, :]
        b = lf
        s = 1
        while s < CHUNK:
            b = b + jnp.where(row >= s, pltpu.roll(b, s, axis=0), 0.0)
            s *= 2
        scores = _chunk_scores(q, k, b, lf)
        o = jnp.dot(scores.astype(BF16), v, preferred_element_type=F32)
        o = o + _nt_dot((q * jnp.exp(b)).astype(BF16), state_t.astype(BF16))
        b_last = b[CHUNK - 1:CHUNK, :]
        k_dec = (k * jnp.exp(b_last - b)).astype(BF16)
        state_t = state_t * jnp.exp(b_last) + _tn_dot(v, k_dec)
        o = o * lax.rsqrt(jnp.mean(o * o, axis=-1, keepdims=True) + EPS) * og
        o_ref[rs, :] = (o * sg_ref[rs, :].astype(F32)).astype(BF16)
    st_ref[...] = state_t


def _hg_rec(q, k, lf, v, sg, o_gain):
    bsz, n_heads, seq, _ = q.shape
    ts = _tile(seq, 512)
    in_spec = pl.BlockSpec((None, None, ts, LANES), lambda b, h, s: (b, h, s, 0))
    return pl.pallas_call(
        functools.partial(_rec_kernel, n_chunks=ts // CHUNK),
        grid=(bsz, n_heads, seq // ts),
        in_specs=[in_spec] * 5 + [pl.BlockSpec((1, LANES), lambda b, h, s: (0, 0))],
        out_specs=pl.BlockSpec((None, ts, LANES), lambda b, h, s: (b, s, h)),
        out_shape=jax.ShapeDtypeStruct((bsz, seq, n_heads * LANES), BF16),
        scratch_shapes=[pltpu.VMEM((LANES, LANES), F32)],
        compiler_params=_params("parallel", "parallel", "arbitrary"),
        name="hg_rec",
    )(q, k, lf, v, sg, o_gain.reshape(1, LANES))


def _rope_pair(u):
    return u + pltpu.roll(u, ROPE, axis=1)


def _mla_kv_kernel(x_ref, gain_ref, shift_ref, scale_ref, wd_ref, lat_gain_ref, wu_ref,
                   gn_ref, gr_ref, cs_ref, k_ref, v_ref, *, n_heads, kv_lora):
    h = _norm_mod(x_ref[...], gain_ref[...], scale_ref[...], shift_ref[...]).astype(BF16)
    ckv = jnp.dot(h, wd_ref[...], preferred_element_type=F32)
    lat = ckv[:, :kv_lora]
    pe2 = ckv[:, kv_lora:]
    c_lat = (lat * lax.rsqrt(jnp.mean(lat * lat, axis=-1, keepdims=True) + EPS) * lat_gain_ref[...]).astype(BF16)
    ss_pe = 0.5 * jnp.sum(pe2 * pe2, axis=-1, keepdims=True)
    lane = lax.broadcasted_iota(jnp.int32, pe2.shape, 1)
    rot = jnp.where(lane < ROPE, _rope_pair(pe2 * gr_ref[...] * cs_ref[...]), 0.0)
    gn = gn_ref[...]
    for hh in range(n_heads):
        kv = jnp.dot(c_lat, wu_ref[:, hh * QK_PAD:(hh + 1) * QK_PAD], preferred_element_type=F32)
        kn = kv[:, :LANES]
        r = lax.rsqrt((jnp.sum(kn * kn, axis=-1, keepdims=True) + ss_pe) * (1.0 / QK) + EPS)
        k_ref[hh, :, :LANES] = (kn * r * gn).astype(BF16)
        k_ref[hh, :, LANES:] = (rot * r).astype(BF16)
        v_ref[hh] = kv[:, LANES:].astype(BF16)


def _mla_kv(x, gain, mod, wd, lat_gain, wu, gn, gr, cs, bsz, seq):
    t, d = x.shape
    n_heads = d // LANES
    kv_lora = lat_gain.shape[0]
    tm = _tile(seq, 512)
    per_b = seq // tm
    mod_spec = lambda k: pl.BlockSpec((None, 1, d), lambda i: (i // per_b, 0, k))
    full = lambda a: pl.BlockSpec(a.shape, lambda i: (0,) * a.ndim)
    lat_gain = lat_gain.reshape(1, kv_lora)
    return pl.pallas_call(
        functools.partial(_mla_kv_kernel, n_heads=n_heads, kv_lora=kv_lora),
        grid=(t // tm,),
        in_specs=[
            pl.BlockSpec((tm, d), lambda i: (i, 0)),
            pl.BlockSpec((1, d), lambda i: (0, 0)),
            mod_spec(0), mod_spec(1),
            full(wd), full(lat_gain), full(wu), full(gn), full(gr),
            pl.BlockSpec((tm, LANES), lambda i: (i, 0)),
        ],
        out_specs=[
            pl.BlockSpec((None, n_heads, tm, QK_PAD), lambda i: (i // per_b, 0, i % per_b, 0)),
            pl.BlockSpec((None, n_heads, tm, LANES), lambda i: (i // per_b, 0, i % per_b, 0)),
        ],
        out_shape=[jax.ShapeDtypeStruct((bsz, n_heads, seq, QK_PAD), BF16),
                   jax.ShapeDtypeStruct((bsz, n_heads, seq, LANES), BF16)],
        compiler_params=_params("parallel"),
        name="mla_kv",
    )(x, gain.reshape(1, d), mod, mod, wd, lat_gain, wu, gn, gr, cs)


def _mla_q_kernel(x_ref, gain_ref, shift_ref, scale_ref, wd_ref, lat_gain_ref, wu_ref,
                  gn_ref, gr_ref, cs_ref, q_ref, *, n_heads):
    h = _norm_mod(x_ref[...], gain_ref[...], scale_ref[...], shift_ref[...]).astype(BF16)
    cq = jnp.dot(h, wd_ref[...], preferred_element_type=F32)
    cq = (cq * lax.rsqrt(jnp.mean(cq * cq, axis=-1, keepdims=True) + EPS) * lat_gain_ref[...]).astype(BF16)
    gn = gn_ref[...]
    rope_tab = gr_ref[...] * cs_ref[...]
    for hh in range(n_heads):
        qh = jnp.dot(cq, wu_ref[:, hh * QK_PAD:(hh + 1) * QK_PAD], preferred_element_type=F32)
        qn = qh[:, :LANES]
        qr = qh[:, LANES:]
        ss = jnp.sum(qn * qn, axis=-1, keepdims=True) + 0.5 * jnp.sum(qr * qr, axis=-1, keepdims=True)
        r = lax.rsqrt(ss * (1.0 / QK) + EPS)
        q_ref[hh, :, :LANES] = (qn * r * gn).astype(BF16)
        q_ref[hh, :, LANES:] = (_rope_pair(qr * rope_tab) * r).astype(BF16)


def _mla_q(x, gain, mod, wd, lat_gain, wu, gn, gr, cs, bsz, seq):
    t, d = x.shape
    n_heads = d // LANES
    q_lora = lat_gain.shape[0]
    tm = _tile(seq, 512)
    per_b = seq // tm
    mod_spec = lambda k: pl.BlockSpec((None, 1, d), lambda i: (i // per_b, 0, k))
    full = lambda a: pl.BlockSpec(a.shape, lambda i: (0,) * a.ndim)
    lat_gain = lat_gain.reshape(1, q_lora)
    return pl.pallas_call(
        functools.partial(_mla_q_kernel, n_heads=n_heads),
        grid=(t // tm,),
        in_specs=[
            pl.BlockSpec((tm, d), lambda i: (i, 0)),
            pl.BlockSpec((1, d), lambda i: (0, 0)),
            mod_spec(0), mod_spec(1),
            full(wd), full(lat_gain), full(wu), full(gn), full(gr),
            pl.BlockSpec((tm, LANES), lambda i: (i, 0)),
        ],
        out_specs=pl.BlockSpec((None, n_heads, tm, QK_PAD), lambda i: (i // per_b, 0, i % per_b, 0)),
        out_shape=jax.ShapeDtypeStruct((bsz, n_heads, seq, QK_PAD), BF16),
        compiler_params=_params("parallel"),
        name="mla_q",
    )(x, gain.reshape(1, d), mod, mod, wd, lat_gain, wu, gn, gr, cs)


def _flash_kernel(q_ref, k_ref, v_ref, o_ref, *, tq):
    i = pl.program_id(2)
    q = q_ref[...]

    def step(j, carry, masked):
        m, l, acc = carry
        rows = pl.ds(pl.multiple_of(j * tq, tq), tq)
        s = _nt_dot(q, k_ref[rows, :])
        if masked:
            qi = lax.broadcasted_iota(jnp.int32, (tq, tq), 0)
            ki = lax.broadcasted_iota(jnp.int32, (tq, tq), 1)
            s = jnp.where(ki <= qi, s, -jnp.inf)
        m_new = jnp.maximum(m, jnp.max(s, axis=-1, keepdims=True))
        alpha = jnp.exp(m - m_new)
        p = jnp.exp(s - m_new)
        l = alpha * l + jnp.sum(p, axis=-1, keepdims=True)
        acc = alpha * acc + jnp.dot(p.astype(BF16), v_ref[rows, :], preferred_element_type=F32)
        return m_new, l, acc

    init = (jnp.full((tq, 1), -jnp.inf, F32), jnp.zeros((tq, 1), F32), jnp.zeros((tq, LANES), F32))
    carry = lax.fori_loop(0, i, lambda j, c: step(j, c, False), init)
    _, l, acc = step(i, carry, True)
    o_ref[...] = (acc / l).astype(BF16)


def _flash(q, k, v):
    bsz, n_heads, seq, _ = q.shape
    tq = _tile(seq, 512)
    return pl.pallas_call(
        functools.partial(_flash_kernel, tq=tq),
        grid=(bsz, n_heads, seq // tq),
        in_specs=[
            pl.BlockSpec((None, None, tq, QK_PAD), lambda b, h, i: (b, h, i, 0)),
            pl.BlockSpec((None, None, seq, QK_PAD), lambda b, h, i: (b, h, 0, 0)),
            pl.BlockSpec((None, None, seq, LANES), lambda b, h, i: (b, h, 0, 0)),
        ],
        out_specs=pl.BlockSpec((None, tq, LANES), lambda b, h, i: (b, i, h)),
        out_shape=jax.ShapeDtypeStruct((bsz, seq, n_heads * LANES), BF16),
        compiler_params=_params("parallel", "parallel", "arbitrary"),
        name="flash",
    )(q, k, v)


def _swap_halves(a):
    half = a.shape[-1] // 2
    return jnp.concatenate([a[..., half:], a[..., :half]], axis=-1)


def _rope_gain(g):
    return jnp.concatenate([g, _swap_halves(g)]).reshape(1, LANES)


def kernel(x, c, positions, ada_w, ada_b, norm_mix, norm_mlp, mlp_w1, mlp_w2, hg_w_in, hg_lower, hg_o_norm, hg_w_out, kv_ada_w, kv_ada_b, kv_in_norm, mla_w_dkv, mla_kv_norm, mla_w_ukv, mla_k_norm, mla_w_dq, mla_q_lat_norm, mla_w_uq, mla_q_norm, mla_w_o):
    bsz, seq, d = x.shape
    depth = ada_w.shape[0]
    n_a = hg_w_in.shape[0]
    n_heads = d // LANES
    kv_lora = mla_kv_norm.shape[0]
    q_lora = mla_w_dq.shape[2]
    t = bsz * seq

    mod = _ada(c, ada_w, ada_b).reshape(depth, bsz, 1, 6 * d)
    kv_mod = _ada(c, kv_ada_w[None], kv_ada_b[None]).reshape(bsz, 1, 2 * d)

    inv_freq = 1.0 / (ROPE_THETA ** (jnp.arange(0, ROPE, 2, dtype=F32) / ROPE))
    ang = positions.astype(F32)[..., None] * inv_freq
    cos, sin = jnp.cos(ang), jnp.sin(ang)
    cs = jnp.concatenate([cos, cos, -sin, sin], axis=-1).reshape(t, LANES)

    lb_all = jnp.cumsum(jax.nn.softmax(hg_lower.astype(F32), axis=0), axis=0)[:n_a]

    xf = x.reshape(t, d)
    k_sh = v_sh = None
    for layer in range(depth):
        if layer == n_a:
            wd = jnp.concatenate([mla_w_dkv, _swap_halves(mla_w_dkv[:, kv_lora:])], axis=1).astype(BF16)
            k_sh, v_sh = _mla_kv(xf, kv_in_norm, kv_mod, wd, mla_kv_norm, mla_w_ukv.astype(BF16),
                                 mla_k_norm[:LANES].reshape(1, LANES), _rope_gain(mla_k_norm[LANES:]),
                                 cs, bsz, seq)
        if layer < n_a:
            q, k, lf, v, sg = _hg_in(xf, norm_mix[layer], mod[layer], lb_all[layer],
                                     hg_w_in[layer].astype(BF16), bsz, seq)
            y = _hg_rec(q, k, lf, v, sg, hg_o_norm[layer]).reshape(t, d)
            xf = _proj_res(y, hg_w_out[layer].astype(BF16), xf, mod[layer], 2, seq)
        else:
            j = layer - n_a
            wu = mla_w_uq[j].reshape(q_lora, n_heads, QK)
            wu = jnp.concatenate([wu, _swap_halves(wu[..., LANES:])], axis=-1)
            wu = wu.reshape(q_lora, n_heads * QK_PAD).astype(BF16)
            sm_scale = float(QK) ** -0.5
            qg = mla_q_norm[j] * sm_scale
            qh = _mla_q(xf, norm_mix[layer], mod[layer], mla_w_dq[j].astype(BF16), mla_q_lat_norm[j], wu,
                        qg[:LANES].reshape(1, LANES), _rope_gain(qg[LANES:]), cs, bsz, seq)
            y = _flash(qh, k_sh, v_sh).reshape(t, d)
            xf = _proj_res(y, mla_w_o[j].astype(BF16), xf, mod[layer], 2, seq)
        xf = _mlp(xf, norm_mlp[layer], mod[layer], mlp_w1[layer].astype(BF16), mlp_w2[layer].astype(BF16), seq)
    return xf.reshape(bsz, seq, d)
```

```python
import functools

import jax
import jax.numpy as jnp
from jax import lax
from jax.experimental import pallas as pl
from jax.experimental.pallas import tpu as pltpu

F32 = jnp.float32
BF16 = jnp.bfloat16

EPS = 1e-6
LANES = 128
ROPE = 64
QK = LANES + ROPE
QK_PAD = 2 * LANES
CHUNK = 64
ROPE_THETA = 10000.0
LOG2E = 1.4426950408889634
VMEM_LIMIT = 56 * 1024 * 1024


def _params(*sem):
    return pltpu.CompilerParams(dimension_semantics=sem, vmem_limit_bytes=VMEM_LIMIT)


def _tile(n, pref):
    t = min(n, pref)
    while n % t:
        t -= LANES
    assert t > 0 and n % t == 0, (n, pref)
    return t


def _norm_mod(x, gain, scale, shift):
    r = lax.rsqrt(jnp.mean(x * x, axis=-1, keepdims=True) + EPS)
    return (x * r) * (gain * (1.0 + scale)) + shift


def _ada_kernel(c_ref, w_ref, b_ref, o_ref):
    c = c_ref[...]
    cond = (c / (1.0 + jnp.exp(-c))).astype(BF16)
    o_ref[...] = jnp.dot(cond, w_ref[...].astype(BF16), preferred_element_type=F32) + b_ref[...]


def _ada(c, w, b):
    n_layers, d, n = w.shape
    bsz = c.shape[0]
    tn = _tile(n, 1024)
    return pl.pallas_call(
        _ada_kernel,
        grid=(n_layers, n // tn),
        in_specs=[
            pl.BlockSpec((bsz, d), lambda l, j: (0, 0)),
            pl.BlockSpec((None, d, tn), lambda l, j: (l, 0, j)),
            pl.BlockSpec((None, 1, tn), lambda l, j: (l, 0, j)),
        ],
        out_specs=pl.BlockSpec((None, bsz, tn), lambda l, j: (l, 0, j)),
        out_shape=jax.ShapeDtypeStruct((n_layers, bsz, n), F32),
        compiler_params=_params("parallel", "parallel"),
        name="ada",
    )(c, w, b.reshape(n_layers, 1, n))


def _mlp_kernel(x_ref, gain_ref, shift_ref, scale_ref, gate_ref, w1_ref, w2_ref, o_ref, h_ref):
    j = pl.program_id(1)

    @pl.when(j == 0)
    def _():
        h_ref[...] = _norm_mod(x_ref[...], gain_ref[...], scale_ref[...], shift_ref[...]).astype(BF16)
        o_ref[...] = jnp.zeros_like(o_ref)

    a = jnp.maximum(jnp.dot(h_ref[...], w1_ref[...], preferred_element_type=F32), 0.0)
    o_ref[...] += jnp.dot((a * a).astype(BF16), w2_ref[...], preferred_element_type=F32)

    @pl.when(j == pl.num_programs(1) - 1)
    def _():
        o_ref[...] = x_ref[...] + gate_ref[...] * o_ref[...]


def _mlp(x, gain, mod, w1, w2, seq):
    t, d = x.shape
    f = w1.shape[1]
    tm = _tile(seq, 512)
    tf = _tile(f, 1024)
    per_b = seq // tm
    mod_spec = lambda k: pl.BlockSpec((None, 1, d), lambda i, j: (i // per_b, 0, k))
    return pl.pallas_call(
        _mlp_kernel,
        grid=(t // tm, f // tf),
        in_specs=[
            pl.BlockSpec((tm, d), lambda i, j: (i, 0)),
            pl.BlockSpec((1, d), lambda i, j: (0, 0)),
            mod_spec(3), mod_spec(4), mod_spec(5),
            pl.BlockSpec((d, tf), lambda i, j: (0, j)),
            pl.BlockSpec((tf, d), lambda i, j: (j, 0)),
        ],
        out_specs=pl.BlockSpec((tm, d), lambda i, j: (i, 0)),
        out_shape=jax.ShapeDtypeStruct((t, d), F32),
        scratch_shapes=[pltpu.VMEM((tm, d), BF16)],
        compiler_params=_params("parallel", "arbitrary"),
        name="mlp",
    )(x, gain.reshape(1, d), mod, mod, mod, w1, w2)


def _proj_res_kernel(y_ref, w_ref, x_ref, gate_ref, o_ref):
    o_ref[...] = x_ref[...] + gate_ref[...] * jnp.dot(y_ref[...], w_ref[...], preferred_element_type=F32)


def _proj_res(y, w, x, mod, gate_idx, seq):
    t, d = x.shape
    k = y.shape[1]
    tm = _tile(seq, 512)
    tn = _tile(d, 1024)
    per_b = seq // tm
    return pl.pallas_call(
        _proj_res_kernel,
        grid=(t // tm, d // tn),
        in_specs=[
            pl.BlockSpec((tm, k), lambda i, j: (i, 0)),
            pl.BlockSpec((k, tn), lambda i, j: (0, j)),
            pl.BlockSpec((tm, tn), lambda i, j: (i, j)),
            pl.BlockSpec((None, 1, tn), lambda i, j: (i // per_b, 0, gate_idx * (d // tn) + j)),
        ],
        out_specs=pl.BlockSpec((tm, tn), lambda i, j: (i, j)),
        out_shape=jax.ShapeDtypeStruct((t, d), F32),
        compiler_params=_params("parallel", "parallel"),
        name="proj_res",
    )(y, w, x, mod)


def _hg_in_kernel(x_ref, gain_ref, shift_ref, scale_ref, llb_ref, l1mlb_ref, omlb_ref,
                  wq_ref, wf_ref, wi_ref, wg_ref,
                  q_ref, k_ref, lf_ref, v_ref, sg_ref, h_ref, *, heads_per_step, q_scale):
    @pl.when(pl.program_id(1) == 0)
    def _():
        h_ref[...] = _norm_mod(x_ref[...], gain_ref[...], scale_ref[...], shift_ref[...]).astype(BF16)

    h = h_ref[...]
    q = jnp.dot(h, wq_ref[...], preferred_element_type=F32) * q_scale
    z = jnp.dot(h, wf_ref[...], preferred_element_type=F32)
    v = jnp.dot(h, wi_ref[...], preferred_element_type=F32)
    g = jnp.dot(h, wg_ref[...], preferred_element_type=F32)

    e = jnp.exp(-jnp.abs(z))
    log_sig = jnp.minimum(z, 0.0) - jnp.log1p(e)
    a = llb_ref[...]
    c = l1mlb_ref[...] + log_sig
    lf = jnp.maximum(a, c) + jnp.log1p(jnp.exp(-jnp.abs(a - c)))
    key = omlb_ref[...] * (jnp.where(z >= 0.0, e, 1.0) / (1.0 + e))
    sg = g / (1.0 + jnp.exp(-g))

    for hh in range(heads_per_step):
        cs = slice(hh * LANES, (hh + 1) * LANES)
        q_ref[hh] = q[:, cs].astype(BF16)
        k_ref[hh] = key[:, cs].astype(BF16)
        lf_ref[hh] = lf[:, cs]
        v_ref[hh] = v[:, cs].astype(BF16)
        sg_ref[hh] = sg[:, cs].astype(BF16)


def _hg_in(x, gain, mod, lb, w_in, bsz, seq):
    t, d = x.shape
    n_heads = d // LANES
    tm = _tile(seq, 1024)
    tn = _tile(d, 2 * LANES)
    hps = tn // LANES
    nj = d // tn
    per_b = seq // tm
    mod_spec = lambda k: pl.BlockSpec((None, 1, d), lambda i, j: (i // per_b, 0, k))
    vec_spec = pl.BlockSpec((1, tn), lambda i, j: (0, j))
    w_spec = lambda grp: pl.BlockSpec((d, tn), lambda i, j: (0, grp * nj + j))
    out_spec = pl.BlockSpec((None, hps, tm, LANES), lambda i, j: (i // per_b, j, i % per_b, 0))
    hm = lambda dt: jax.ShapeDtypeStruct((bsz, n_heads, seq, LANES), dt)
    lbf = lb.reshape(1, d)
    kern = functools.partial(_hg_in_kernel, heads_per_step=hps, q_scale=float(LANES) ** -0.5)
    return pl.pallas_call(
        kern,
        grid=(t // tm, nj),
        in_specs=[
            pl.BlockSpec((tm, d), lambda i, j: (i, 0)),
            pl.BlockSpec((1, d), lambda i, j: (0, 0)),
            mod_spec(0), mod_spec(1),
            vec_spec, vec_spec, vec_spec,
            w_spec(0), w_spec(1), w_spec(2), w_spec(3),
        ],
        out_specs=[out_spec] * 5,
        out_shape=[hm(BF16), hm(BF16), hm(F32), hm(BF16), hm(BF16)],
        scratch_shapes=[pltpu.VMEM((tm, d), BF16)],
        compiler_params=_params("parallel", "arbitrary"),
        name="hg_in",
    )(x, gain.reshape(1, d), mod, mod, jnp.log(lbf), jnp.log1p(-lbf), 1.0 - lbf, w_in, w_in, w_in, w_in)


def _nt_dot(a, b):
    return lax.dot_general(a, b, (((1,), (1,)), ((), ())), preferred_element_type=F32)


def _tn_dot(a, b):
    return lax.dot_general(a, b, (((0,), (0,)), ((), ())), preferred_element_type=F32)


def _chunk_scores(q, k, b, lf):
    row = lax.broadcasted_iota(jnp.int32, (CHUNK, LANES), 0)
    ti = lax.broadcasted_iota(jnp.int32, (CHUNK, CHUNK), 0)
    si = lax.broadcasted_iota(jnp.int32, (CHUNK, CHUNK), 1)
    scores = jnp.where(ti == si, _nt_dot(q.astype(BF16), k.astype(BF16)), 0.0)
    start = b - lf
    c = 1
    while c < CHUNK:
        later = (row & c) != 0
        bound = jnp.where(later, start, pltpu.roll(start, CHUNK - c, axis=0))
        e = jnp.exp(jnp.where(later, b - bound, bound - b))
        m = jnp.where(later, q, k) * e
        qs = jnp.where(later, m, 0.0).astype(BF16)
        ks = jnp.where(later, 0.0, m).astype(BF16)
        same_parent = (ti & -(2 * c)) == (si & -(2 * c))
        scores = scores + jnp.where(same_parent, _nt_dot(qs, ks), 0.0)
        if 2 * c < CHUNK:
            start = jnp.where(later, pltpu.roll(start, c, axis=0), start)
        c *= 2
    return scores


def _rec_kernel(q_ref, k_ref, lf_ref, v_ref, sg_ref, og_ref, o_ref, st_ref, *, n_chunks):
    @pl.when(pl.program_id(2) == 0)
    def _():
        st_ref[...] = jnp.zeros_like(st_ref)

    row = lax.broadcasted_iota(jnp.int32, (CHUNK, LANES), 0)
    og = og_ref[...]
    state_t = st_ref[...]
    for ci in range(n_chunks):
        rs = slice(ci * CHUNK, (ci + 1) * CHUNK)
        lf = lf_ref[rs, :]
        q = q_ref[rs, :].astype(F32)
        k = k_ref[rs, :].astype(F32)
        v = v_ref[rs, :]
        b = lf
        s = 1
        while s < CHUNK:
            b = b + jnp.where(row >= s, pltpu.roll(b, s, axis=0), 0.0)
            s *= 2
        scores = _chunk_scores(q, k, b, lf)
        o = jnp.dot(scores.astype(BF16), v, preferred_element_type=F32)
        o = o + _nt_dot((q * jnp.exp(b)).astype(BF16), state_t.astype(BF16))
        b_last = b[CHUNK - 1:CHUNK, :]
        k_dec = (k * jnp.exp(b_last - b)).astype(BF16)
        state_t = state_t * jnp.exp(b_last) + _tn_dot(v, k_dec)
        o = o * lax.rsqrt(jnp.mean(o * o, axis=-1, keepdims=True) + EPS) * og
        o_ref[rs, :] = (o * sg_ref[rs, :].astype(F32)).astype(BF16)
    st_ref[...] = state_t


def _hg_rec(q, k, lf, v, sg, o_gain):
    bsz, n_heads, seq, _ = q.shape
    ts = _tile(seq, 512)
    in_spec = pl.BlockSpec((None, None, ts, LANES), lambda b, h, s: (b, h, s, 0))
    return pl.pallas_call(
        functools.partial(_rec_kernel, n_chunks=ts // CHUNK),
        grid=(bsz, n_heads, seq // ts),
        in_specs=[in_spec] * 5 + [pl.BlockSpec((1, LANES), lambda b, h, s: (0, 0))],
        out_specs=pl.BlockSpec((None, ts, LANES), lambda b, h, s: (b, s, h)),
        out_shape=jax.ShapeDtypeStruct((bsz, seq, n_heads * LANES), BF16),
        scratch_shapes=[pltpu.VMEM((LANES, LANES), F32)],
        compiler_params=_params("parallel", "parallel", "arbitrary"),
        name="hg_rec",
    )(q, k, lf, v, sg, o_gain.reshape(1, LANES))


def _rope_pair(u):
    return u + pltpu.roll(u, ROPE, axis=1)


def _mla_kv_kernel(x_ref, gain_ref, shift_ref, scale_ref, wd_ref, lat_gain_ref, wu_ref,
                   gn_ref, gr_ref, cs_ref, k_ref, v_ref, *, n_heads, kv_lora):
    h = _norm_mod(x_ref[...], gain_ref[...], scale_ref[...], shift_ref[...]).astype(BF16)
    ckv = jnp.dot(h, wd_ref[...], preferred_element_type=F32)
    lat = ckv[:, :kv_lora]
    pe2 = ckv[:, kv_lora:]
    c_lat = (lat * lax.rsqrt(jnp.mean(lat * lat, axis=-1, keepdims=True) + EPS) * lat_gain_ref[...]).astype(BF16)
    ones = jnp.ones((LANES, LANES), BF16)
    ss_pe = 0.5 * jnp.dot((pe2 * pe2).astype(BF16), ones, preferred_element_type=F32)
    rot = _rope_pair(pe2 * gr_ref[...] * cs_ref[...])
    gn = gn_ref[...]
    def head_kv(hh):
        return jnp.dot(c_lat, wu_ref[:, hh * QK_PAD:(hh + 1) * QK_PAD], preferred_element_type=F32)

    nxt = head_kv(0)
    for hh in range(n_heads):
        kv = nxt
        if hh + 1 < n_heads:
            nxt = head_kv(hh + 1)
        kn = kv[:, :LANES]
        ss = jnp.dot((kn * kn).astype(BF16), ones, preferred_element_type=F32) + ss_pe
        r = lax.rsqrt(ss * (1.0 / QK) + EPS)
        k_ref[hh, :, :LANES] = (kn * r * gn).astype(BF16)
        k_ref[hh, :, LANES:] = (rot * r).astype(BF16)
        v_ref[hh] = kv[:, LANES:].astype(BF16)


def _mla_kv(x, gain, mod, wd, lat_gain, wu, gn, gr, cs, bsz, seq):
    t, d = x.shape
    n_heads = d // LANES
    kv_lora = lat_gain.shape[0]
    tm = _tile(seq, 512)
    per_b = seq // tm
    mod_spec = lambda k: pl.BlockSpec((None, 1, d), lambda i: (i // per_b, 0, k))
    full = lambda a: pl.BlockSpec(a.shape, lambda i: (0,) * a.ndim)
    lat_gain = lat_gain.reshape(1, kv_lora)
    return pl.pallas_call(
        functools.partial(_mla_kv_kernel, n_heads=n_heads, kv_lora=kv_lora),
        grid=(t // tm,),
        in_specs=[
            pl.BlockSpec((tm, d), lambda i: (i, 0)),
            pl.BlockSpec((1, d), lambda i: (0, 0)),
            mod_spec(0), mod_spec(1),
            full(wd), full(lat_gain), full(wu), full(gn), full(gr),
            pl.BlockSpec((tm, LANES), lambda i: (i, 0)),
        ],
        out_specs=[
            pl.BlockSpec((None, n_heads, tm, QK_PAD), lambda i: (i // per_b, 0, i % per_b, 0)),
            pl.BlockSpec((None, n_heads, tm, LANES), lambda i: (i // per_b, 0, i % per_b, 0)),
        ],
        out_shape=[jax.ShapeDtypeStruct((bsz, n_heads, seq, QK_PAD), BF16),
                   jax.ShapeDtypeStruct((bsz, n_heads, seq, LANES), BF16)],
        compiler_params=_params("parallel"),
        name="mla_kv",
    )(x, gain.reshape(1, d), mod, mod, wd, lat_gain, wu, gn, gr, cs)


def _mla_q_kernel(x_ref, gain_ref, shift_ref, scale_ref, wd_ref, lat_gain_ref, wu_ref,
                  gn_ref, gr_ref, cs_ref, q_ref, *, n_heads):
    h = _norm_mod(x_ref[...], gain_ref[...], scale_ref[...], shift_ref[...]).astype(BF16)
    cq = jnp.dot(h, wd_ref[...], preferred_element_type=F32)
    cq = (cq * lax.rsqrt(jnp.mean(cq * cq, axis=-1, keepdims=True) + EPS) * lat_gain_ref[...]).astype(BF16)
    gn = gn_ref[...]
    rope_tab = gr_ref[...] * cs_ref[...]
    row = lax.broadcasted_iota(jnp.int32, (QK_PAD, LANES), 0)
    sum_w = jnp.where(row < LANES, 1.0, 0.5).astype(BF16)
    def head_q(hh):
        return jnp.dot(cq, wu_ref[:, hh * QK_PAD:(hh + 1) * QK_PAD], preferred_element_type=F32)

    nxt = head_q(0)
    for hh in range(n_heads):
        qh = nxt
        if hh + 1 < n_heads:
            nxt = head_q(hh + 1)
        qn = qh[:, :LANES]
        qr = qh[:, LANES:]
        ss = jnp.dot((qh * qh).astype(BF16), sum_w, preferred_element_type=F32)
        r = lax.rsqrt(ss * (1.0 / QK) + EPS)
        q_ref[hh, :, :LANES] = (qn * r * gn).astype(BF16)
        q_ref[hh, :, LANES:] = (qr * rope_tab * r).astype(BF16)


def _mla_q(x, gain, mod, wd, lat_gain, wu, gn, gr, cs, bsz, seq):
    t, d = x.shape
    n_heads = d // LANES
    q_lora = lat_gain.shape[0]
    tm = _tile(seq, 512)
    per_b = seq // tm
    mod_spec = lambda k: pl.BlockSpec((None, 1, d), lambda i: (i // per_b, 0, k))
    full = lambda a: pl.BlockSpec(a.shape, lambda i: (0,) * a.ndim)
    lat_gain = lat_gain.reshape(1, q_lora)
    return pl.pallas_call(
        functools.partial(_mla_q_kernel, n_heads=n_heads),
        grid=(t // tm,),
        in_specs=[
            pl.BlockSpec((tm, d), lambda i: (i, 0)),
            pl.BlockSpec((1, d), lambda i: (0, 0)),
            mod_spec(0), mod_spec(1),
            full(wd), full(lat_gain), full(wu), full(gn), full(gr),
            pl.BlockSpec((tm, LANES), lambda i: (i, 0)),
        ],
        out_specs=pl.BlockSpec((None, n_heads, tm, QK_PAD), lambda i: (i // per_b, 0, i % per_b, 0)),
        out_shape=jax.ShapeDtypeStruct((bsz, n_heads, seq, QK_PAD), BF16),
        compiler_params=_params("parallel"),
        name="mla_q",
    )(x, gain.reshape(1, d), mod, mod, wd, lat_gain, wu, gn, gr, cs)


def _flash_kernel(q_ref, k_ref, v_ref, o_ref, s_ref, m_ref, l_ref, acc_ref, *, tq, heads_per_step):
    i = pl.program_id(2)
    qs = [q_ref[hh] for hh in range(heads_per_step)]
    m_ref[...] = jnp.full(m_ref.shape, -jnp.inf, F32)
    l_ref[...] = jnp.zeros_like(l_ref)
    acc_ref[...] = jnp.zeros_like(acc_ref)

    def key_rows(j):
        return pl.ds(pl.multiple_of(j * tq, tq), tq)

    def scores(hh, j):
        return _nt_dot(k_ref[hh, key_rows(j), :], qs[hh])

    def consume(hh, j, masked):
        s = s_ref[hh]
        if masked:
            ki = lax.broadcasted_iota(jnp.int32, (tq, tq), 0)
            qi = lax.broadcasted_iota(jnp.int32, (tq, tq), 1)
            s = jnp.where(ki <= qi, s, -jnp.inf)
        m = m_ref[hh]
        m_new = jnp.maximum(m, jnp.max(s, axis=0, keepdims=True))
        alpha = jnp.exp2(m - m_new)
        p = jnp.exp2(s - m_new)
        m_ref[hh] = m_new
        l_ref[hh] = alpha * l_ref[hh] + jnp.sum(p, axis=0, keepdims=True)
        acc_ref[hh] = alpha * acc_ref[hh] + _tn_dot(v_ref[hh, key_rows(j), :], p.astype(BF16))

    for hh in range(heads_per_step):
        s_ref[hh] = scores(hh, 0)

    def body(j, carry):
        for hh in range(heads_per_step):
            s_next = scores(hh, j + 1)
            consume(hh, j, False)
            s_ref[hh] = s_next
        return carry

    lax.fori_loop(0, i, body, 0)
    for hh in range(heads_per_step):
        consume(hh, i, True)
        o_ref[:, hh * LANES:(hh + 1) * LANES] = (acc_ref[hh] / l_ref[hh]).T.astype(BF16)


def _flash(q, k, v):
    bsz, n_heads, seq, _ = q.shape
    tq = _tile(seq, 512)
    hps = 4 if n_heads % 4 == 0 else (2 if n_heads % 2 == 0 else 1)
    return pl.pallas_call(
        functools.partial(_flash_kernel, tq=tq, heads_per_step=hps),
        grid=(bsz, n_heads // hps, seq // tq),
        in_specs=[
            pl.BlockSpec((None, hps, tq, QK_PAD), lambda b, h, i: (b, h, i, 0)),
            pl.BlockSpec((None, hps, seq, QK_PAD), lambda b, h, i: (b, h, 0, 0)),
            pl.BlockSpec((None, hps, seq, LANES), lambda b, h, i: (b, h, 0, 0)),
        ],
        out_specs=pl.BlockSpec((None, tq, hps * LANES), lambda b, h, i: (b, i, h)),
        out_shape=jax.ShapeDtypeStruct((bsz, seq, n_heads * LANES), BF16),
        scratch_shapes=[pltpu.VMEM((hps, tq, tq), F32),
                        pltpu.VMEM((hps, 1, tq), F32), pltpu.VMEM((hps, 1, tq), F32),
                        pltpu.VMEM((hps, LANES, tq), F32)],
        compiler_params=_params("parallel", "parallel", "arbitrary"),
        name="flash",
    )(q, k, v)


def _swap_halves(a):
    half = a.shape[-1] // 2
    return jnp.concatenate([a[..., half:], a[..., :half]], axis=-1)


def _rope_gain(g):
    return jnp.concatenate([g, _swap_halves(g)]).reshape(1, LANES)


def kernel(x, c, positions, ada_w, ada_b, norm_mix, norm_mlp, mlp_w1, mlp_w2, hg_w_in, hg_lower, hg_o_norm, hg_w_out, kv_ada_w, kv_ada_b, kv_in_norm, mla_w_dkv, mla_kv_norm, mla_w_ukv, mla_k_norm, mla_w_dq, mla_q_lat_norm, mla_w_uq, mla_q_norm, mla_w_o):
    bsz, seq, d = x.shape
    depth = ada_w.shape[0]
    n_a = hg_w_in.shape[0]
    n_heads = d // LANES
    kv_lora = mla_kv_norm.shape[0]
    q_lora = mla_w_dq.shape[2]
    t = bsz * seq

    mod = _ada(c, ada_w, ada_b).reshape(depth, bsz, 1, 6 * d)
    kv_mod = _ada(c, kv_ada_w[None], kv_ada_b[None]).reshape(bsz, 1, 2 * d)

    inv_freq = 1.0 / (ROPE_THETA ** (jnp.arange(0, ROPE, 2, dtype=F32) / ROPE))
    ang = positions.astype(F32)[..., None] * inv_freq
    cos, sin = jnp.cos(ang), jnp.sin(ang)
    cs = jnp.concatenate([cos, cos, -sin, sin], axis=-1).reshape(t, LANES)

    lb_all = jnp.cumsum(jax.nn.softmax(hg_lower.astype(F32), axis=0), axis=0)[:n_a]

    xf = x.reshape(t, d)
    k_sh = v_sh = None
    for layer in range(depth):
        if layer == n_a:
            wd = jnp.concatenate([mla_w_dkv, _swap_halves(mla_w_dkv[:, kv_lora:])], axis=1).astype(BF16)
            k_sh, v_sh = _mla_kv(xf, kv_in_norm, kv_mod, wd, mla_kv_norm, mla_w_ukv.astype(BF16),
                                 mla_k_norm[:LANES].reshape(1, LANES), _rope_gain(mla_k_norm[LANES:]),
                                 cs, bsz, seq)
        if layer < n_a:
            q, k, lf, v, sg = _hg_in(xf, norm_mix[layer], mod[layer], lb_all[layer],
                                     hg_w_in[layer].astype(BF16), bsz, seq)
            y = _hg_rec(q, k, lf, v, sg, hg_o_norm[layer]).reshape(t, d)
            xf = _proj_res(y, hg_w_out[layer].astype(BF16), xf, mod[layer], 2, seq)
        else:
            j = layer - n_a
            wu = mla_w_uq[j].reshape(q_lora, n_heads, QK)
            wu = jnp.concatenate([wu, _swap_halves(wu[..., LANES:])], axis=-1)
            wu = wu.reshape(q_lora, n_heads * QK_PAD).astype(BF16)
            sm_scale = float(QK) ** -0.5 * LOG2E
            qg = mla_q_norm[j] * sm_scale
            qh = _mla_q(xf, norm_mix[layer], mod[layer], mla_w_dq[j].astype(BF16), mla_q_lat_norm[j], wu,
                        qg[:LANES].reshape(1, LANES), _rope_gain(qg[LANES:]), cs, bsz, seq)
            y = _flash(qh, k_sh, v_sh).reshape(t, d)
            xf = _proj_res(y, mla_w_o[j].astype(BF16), xf, mod[layer], 2, seq)
        xf = _mlp(xf, norm_mlp[layer], mod[layer], mlp_w1[layer].astype(BF16), mlp_w2[layer].astype(BF16), seq)
    return xf.reshape(bsz, seq, d)
```

```python
import functools

import jax
import jax.numpy as jnp
from jax import lax
from jax.experimental import pallas as pl
from jax.experimental.pallas import tpu as pltpu

F32 = jnp.float32
BF16 = jnp.bfloat16

EPS = 1e-6
LANES = 128
ROPE = 64
QK = LANES + ROPE
QK_PAD = 2 * LANES
CHUNK = 64
MAX_FACTORED_DECAY = 72.0
ROPE_THETA = 10000.0
LOG2E = 1.4426950408889634
VMEM_LIMIT = 56 * 1024 * 1024


def _params(*sem):
    return pltpu.CompilerParams(dimension_semantics=sem, vmem_limit_bytes=VMEM_LIMIT)


def _tile(n, pref):
    t = min(n, pref)
    while n % t:
        t -= LANES
    assert t > 0 and n % t == 0, (n, pref)
    return t


def _norm_mod(x, gain, scale, shift):
    r = lax.rsqrt(jnp.mean(x * x, axis=-1, keepdims=True) + EPS)
    return (x * r) * (gain * (1.0 + scale)) + shift


def _ada_kernel(c_ref, w_ref, b_ref, o_ref):
    c = c_ref[...]
    cond = (c / (1.0 + jnp.exp(-c))).astype(BF16)
    o_ref[...] = jnp.dot(cond, w_ref[...].astype(BF16), preferred_element_type=F32) + b_ref[...]


def _ada(c, w, b):
    n_layers, d, n = w.shape
    bsz = c.shape[0]
    tn = _tile(n, 1024)
    return pl.pallas_call(
        _ada_kernel,
        grid=(n_layers, n // tn),
        in_specs=[
            pl.BlockSpec((bsz, d), lambda l, j: (0, 0)),
            pl.BlockSpec((None, d, tn), lambda l, j: (l, 0, j)),
            pl.BlockSpec((None, 1, tn), lambda l, j: (l, 0, j)),
        ],
        out_specs=pl.BlockSpec((None, bsz, tn), lambda l, j: (l, 0, j)),
        out_shape=jax.ShapeDtypeStruct((n_layers, bsz, n), F32),
        compiler_params=_params("parallel", "parallel"),
        name="ada",
    )(c, w, b.reshape(n_layers, 1, n))


def _mlp_kernel(x_ref, gain_ref, shift_ref, scale_ref, gate_ref, w1_ref, w2_ref, o_ref, h_ref):
    j = pl.program_id(1)

    @pl.when(j == 0)
    def _():
        h_ref[...] = _norm_mod(x_ref[...], gain_ref[...], scale_ref[...], shift_ref[...]).astype(BF16)
        o_ref[...] = jnp.zeros_like(o_ref)

    a = jnp.maximum(jnp.dot(h_ref[...], w1_ref[...], preferred_element_type=F32), 0.0)
    o_ref[...] += jnp.dot((a * a).astype(BF16), w2_ref[...], preferred_element_type=F32)

    @pl.when(j == pl.num_programs(1) - 1)
    def _():
        o_ref[...] = x_ref[...] + gate_ref[...] * o_ref[...]


def _mlp(x, gain, mod, w1, w2, seq):
    t, d = x.shape
    f = w1.shape[1]
    tm = _tile(seq, 512)
    tf = _tile(f, 1024)
    per_b = seq // tm
    mod_spec = lambda k: pl.BlockSpec((None, 1, d), lambda i, j: (i // per_b, 0, k))
    return pl.pallas_call(
        _mlp_kernel,
        grid=(t // tm, f // tf),
        in_specs=[
            pl.BlockSpec((tm, d), lambda i, j: (i, 0)),
            pl.BlockSpec((1, d), lambda i, j: (0, 0)),
            mod_spec(3), mod_spec(4), mod_spec(5),
            pl.BlockSpec((d, tf), lambda i, j: (0, j)),
            pl.BlockSpec((tf, d), lambda i, j: (j, 0)),
        ],
        out_specs=pl.BlockSpec((tm, d), lambda i, j: (i, 0)),
        out_shape=jax.ShapeDtypeStruct((t, d), F32),
        scratch_shapes=[pltpu.VMEM((tm, d), BF16)],
        compiler_params=_params("parallel", "arbitrary"),
        name="mlp",
    )(x, gain.reshape(1, d), mod, mod, mod, w1, w2)


def _proj_res_kernel(y_ref, w_ref, x_ref, gate_ref, o_ref):
    o_ref[...] = x_ref[...] + gate_ref[...] * jnp.dot(y_ref[...], w_ref[...], preferred_element_type=F32)


def _proj_res(y, w, x, mod, gate_idx, seq):
    t, d = x.shape
    k = y.shape[1]
    tm = _tile(seq, 512)
    tn = _tile(d, 1024)
    per_b = seq // tm
    return pl.pallas_call(
        _proj_res_kernel,
        grid=(t // tm, d // tn),
        in_specs=[
            pl.BlockSpec((tm, k), lambda i, j: (i, 0)),
            pl.BlockSpec((k, tn), lambda i, j: (0, j)),
            pl.BlockSpec((tm, tn), lambda i, j: (i, j)),
            pl.BlockSpec((None, 1, tn), lambda i, j: (i // per_b, 0, gate_idx * (d // tn) + j)),
        ],
        out_specs=pl.BlockSpec((tm, tn), lambda i, j: (i, j)),
        out_shape=jax.ShapeDtypeStruct((t, d), F32),
        compiler_params=_params("parallel", "parallel"),
        name="proj_res",
    )(y, w, x, mod)


def _hg_in_kernel(x_ref, gain_ref, shift_ref, scale_ref, llb_ref, l1mlb_ref, omlb_ref,
                  wq_ref, wf_ref, wi_ref, wg_ref,
                  q_ref, k_ref, lf_ref, v_ref, sg_ref, h_ref, *, heads_per_step, q_scale):
    @pl.when(pl.program_id(1) == 0)
    def _():
        h_ref[...] = _norm_mod(x_ref[...], gain_ref[...], scale_ref[...], shift_ref[...]).astype(BF16)

    h = h_ref[...]
    z = jnp.dot(h, wf_ref[...], preferred_element_type=F32)
    q = jnp.dot(h, wq_ref[...], preferred_element_type=F32) * q_scale
    v = jnp.dot(h, wi_ref[...], preferred_element_type=F32)
    g = jnp.dot(h, wg_ref[...], preferred_element_type=F32)

    e = jnp.exp(-jnp.abs(z))
    log_sig = jnp.minimum(z, 0.0) - jnp.log(1.0 + e)
    a = llb_ref[...]
    c = l1mlb_ref[...] + log_sig
    lf = jnp.maximum(a, c) + jnp.log(1.0 + jnp.exp(-jnp.abs(a - c)))
    key = omlb_ref[...] * (jnp.where(z >= 0.0, e, 1.0) / (1.0 + e))
    sg = g / (1.0 + jnp.exp(-g))

    for hh in range(heads_per_step):
        cs = slice(hh * LANES, (hh + 1) * LANES)
        q_ref[hh] = q[:, cs].astype(BF16)
        k_ref[hh] = key[:, cs].astype(BF16)
        lf_ref[hh] = lf[:, cs]
        v_ref[hh] = v[:, cs].astype(BF16)
        sg_ref[hh] = sg[:, cs].astype(BF16)


def _hg_in(x, gain, mod, lb, w_in, bsz, seq):
    t, d = x.shape
    n_heads = d // LANES
    tm = _tile(seq, 1024)
    tn = _tile(d, 2 * LANES)
    hps = tn // LANES
    nj = d // tn
    per_b = seq // tm
    mod_spec = lambda k: pl.BlockSpec((None, 1, d), lambda i, j: (i // per_b, 0, k))
    vec_spec = pl.BlockSpec((1, tn), lambda i, j: (0, j))
    w_spec = lambda grp: pl.BlockSpec((d, tn), lambda i, j: (0, grp * nj + j))
    out_spec = pl.BlockSpec((None, hps, tm, LANES), lambda i, j: (i // per_b, j, i % per_b, 0))
    hm = lambda dt: jax.ShapeDtypeStruct((bsz, n_heads, seq, LANES), dt)
    lbf = lb.reshape(1, d)
    kern = functools.partial(_hg_in_kernel, heads_per_step=hps, q_scale=float(LANES) ** -0.5)
    return pl.pallas_call(
        kern,
        grid=(t // tm, nj),
        in_specs=[
            pl.BlockSpec((tm, d), lambda i, j: (i, 0)),
            pl.BlockSpec((1, d), lambda i, j: (0, 0)),
            mod_spec(0), mod_spec(1),
            vec_spec, vec_spec, vec_spec,
            w_spec(0), w_spec(1), w_spec(2), w_spec(3),
        ],
        out_specs=[out_spec] * 5,
        out_shape=[hm(BF16), hm(BF16), hm(F32), hm(BF16), hm(BF16)],
        scratch_shapes=[pltpu.VMEM((tm, d), BF16)],
        compiler_params=_params("parallel", "arbitrary"),
        name="hg_in",
    )(x, gain.reshape(1, d), mod, mod, jnp.log(lbf), jnp.log1p(-lbf), 1.0 - lbf, w_in, w_in, w_in, w_in)


def _nt_dot(a, b):
    return lax.dot_general(a, b, (((1,), (1,)), ((), ())), preferred_element_type=F32)


def _tn_dot(a, b):
    return lax.dot_general(a, b, (((0,), (0,)), ((), ())), preferred_element_type=F32)


def _chunk_scores(q, k, b, lf):
    row = lax.broadcasted_iota(jnp.int32, (CHUNK, LANES), 0)
    ti = lax.broadcasted_iota(jnp.int32, (CHUNK, CHUNK), 0)
    si = lax.broadcasted_iota(jnp.int32, (CHUNK, CHUNK), 1)
    scores = jnp.where(ti == si, _nt_dot(q.astype(BF16), k.astype(BF16)), 0.0)
    start = b - lf
    c = 1
    while c < CHUNK:
        later = (row & c) != 0
        bound = jnp.where(later, start, pltpu.roll(start, CHUNK - c, axis=0))
        e = jnp.exp(jnp.where(later, b - bound, bound - b))
        m = jnp.where(later, q, k) * e
        qs = jnp.where(later, m, 0.0).astype(BF16)
        ks = jnp.where(later, 0.0, m).astype(BF16)
        same_parent = (ti & -(2 * c)) == (si & -(2 * c))
        scores = scores + jnp.where(same_parent, _nt_dot(qs, ks), 0.0)
        if 2 * c < CHUNK:
            start = jnp.where(later, pltpu.roll(start, c, axis=0), start)
        c *= 2
    return scores


def _cumsum_rows(lf, tri):
    hi = lf.astype(BF16)
    r1 = lf - hi.astype(F32)
    mid = r1.astype(BF16)
    lo = (r1 - mid.astype(F32)).astype(BF16)
    dot = lambda part: jnp.dot(tri, part, preferred_element_type=F32)
    return dot(hi) + dot(mid) + dot(lo)


def _rec_kernel(q_ref, k_ref, lf_ref, v_ref, sg_ref, og_ref, o_ref, st_ref, *, n_chunks):
    @pl.when(pl.program_id(2) == 0)
    def _():
        st_ref[...] = jnp.zeros_like(st_ref)

    ti = lax.broadcasted_iota(jnp.int32, (CHUNK, CHUNK), 0)
    si = lax.broadcasted_iota(jnp.int32, (CHUNK, CHUNK), 1)
    causal = si <= ti
    tri = jnp.where(causal, 1.0, 0.0).astype(BF16)
    og = og_ref[...]

    def rows(ci):
        return slice(ci * CHUNK, (ci + 1) * CHUNK)

    mean_w = jnp.full((LANES, LANES), 1.0 / LANES, BF16)

    def finish(outs):
        ms = [jnp.dot((o * o).astype(BF16), mean_w, preferred_element_type=F32) for o in outs]
        for ci, o in enumerate(outs):
            o = o * lax.rsqrt(ms[ci] + EPS) * og
            o_ref[rows(ci), :] = (o * sg_ref[rows(ci), :].astype(F32)).astype(BF16)

    cs = range(n_chunks)
    b = [_cumsum_rows(lf_ref[rows(ci), :], tri) for ci in cs]
    total = b[0][CHUNK - 1:CHUNK, :]
    for ci in range(1, n_chunks):
        total = jnp.minimum(total, b[ci][CHUNK - 1:CHUNK, :])
    mild = jnp.min(total) >= -MAX_FACTORED_DECAY
    q_dec = [(q_ref[rows(ci), :].astype(F32) * jnp.exp(b[ci])).astype(BF16) for ci in cs]
    decay = [jnp.exp(b[ci][CHUNK - 1:CHUNK, :]) for ci in cs]

    @pl.when(mild)
    def _():
        k_grow = [k_ref[rows(ci), :].astype(F32) * jnp.exp(-b[ci]) for ci in cs]
        scores = [jnp.where(causal, _nt_dot(q_dec[ci], k_grow[ci].astype(BF16)), 0.0).astype(BF16) for ci in cs]
        intra = [jnp.dot(scores[ci], v_ref[rows(ci), :], preferred_element_type=F32) for ci in cs]
        update = [_tn_dot(v_ref[rows(ci), :], (k_grow[ci] * decay[ci]).astype(BF16)) for ci in cs]
        states = [st_ref[...]]
        for ci in cs:
            states.append(states[ci] * decay[ci] + update[ci])
        finish([intra[ci] + _nt_dot(q_dec[ci], states[ci].astype(BF16)) for ci in cs])
        st_ref[...] = states[n_chunks]

    @pl.when(jnp.logical_not(mild))
    def _():
        state_t = st_ref[...]
        outs = []
        for ci in range(n_chunks):
            lf = lf_ref[rows(ci), :]
            q = q_ref[rows(ci), :].astype(F32)
            k = k_ref[rows(ci), :].astype(F32)
            v = v_ref[rows(ci), :]
            scores = _chunk_scores(q, k, b[ci], lf)
            o = jnp.dot(scores.astype(BF16), v, preferred_element_type=F32)
            o = o + _nt_dot(q_dec[ci], state_t.astype(BF16))
            k_dec = (k * jnp.exp(b[ci][CHUNK - 1:CHUNK, :] - b[ci])).astype(BF16)
            state_t = state_t * decay[ci] + _tn_dot(v, k_dec)
            outs.append(o)
        finish(outs)
        st_ref[...] = state_t


def _hg_rec(q, k, lf, v, sg, o_gain):
    bsz, n_heads, seq, _ = q.shape
    ts = _tile(seq, 1024)
    in_spec = pl.BlockSpec((None, None, ts, LANES), lambda b, h, s: (b, h, s, 0))
    return pl.pallas_call(
        functools.partial(_rec_kernel, n_chunks=ts // CHUNK),
        grid=(bsz, n_heads, seq // ts),
        in_specs=[in_spec] * 5 + [pl.BlockSpec((1, LANES), lambda b, h, s: (0, 0))],
        out_specs=pl.BlockSpec((None, ts, LANES), lambda b, h, s: (b, s, h)),
        out_shape=jax.ShapeDtypeStruct((bsz, seq, n_heads * LANES), BF16),
        scratch_shapes=[pltpu.VMEM((LANES, LANES), F32)],
        compiler_params=_params("parallel", "parallel", "arbitrary"),
        name="hg_rec",
    )(q, k, lf, v, sg, o_gain.reshape(1, LANES))


def _rope_pair(u):
    return u + pltpu.roll(u, ROPE, axis=1)


def _mla_kv_kernel(x_ref, gain_ref, shift_ref, scale_ref, wd_ref, lat_gain_ref, wu_ref,
                   gn_ref, gr_ref, cs_ref, k_ref, v_ref, *, n_heads, kv_lora):
    h = _norm_mod(x_ref[...], gain_ref[...], scale_ref[...], shift_ref[...]).astype(BF16)
    ckv = jnp.dot(h, wd_ref[...], preferred_element_type=F32)
    lat = ckv[:, :kv_lora]
    pe2 = ckv[:, kv_lora:]
    c_lat = (lat * lax.rsqrt(jnp.mean(lat * lat, axis=-1, keepdims=True) + EPS) * lat_gain_ref[...]).astype(BF16)
    ones = jnp.ones((LANES, LANES), BF16)
    ss_pe = 0.5 * jnp.dot((pe2 * pe2).astype(BF16), ones, preferred_element_type=F32)
    rot = _rope_pair(pe2 * gr_ref[...] * cs_ref[...])
    gn = gn_ref[...]
    def head_kv(hh):
        return jnp.dot(c_lat, wu_ref[:, hh * QK_PAD:(hh + 1) * QK_PAD], preferred_element_type=F32)

    nxt = head_kv(0)
    for hh in range(n_heads):
        kv = nxt
        if hh + 1 < n_heads:
            nxt = head_kv(hh + 1)
        kn = kv[:, :LANES]
        ss = jnp.dot((kn * kn).astype(BF16), ones, preferred_element_type=F32) + ss_pe
        r = lax.rsqrt(ss * (1.0 / QK) + EPS)
        k_ref[hh, :, :LANES] = (kn * r * gn).astype(BF16)
        k_ref[hh, :, LANES:] = (rot * r).astype(BF16)
        v_ref[hh] = kv[:, LANES:].astype(BF16)


def _mla_kv(x, gain, mod, wd, lat_gain, wu, gn, gr, cs, bsz, seq):
    t, d = x.shape
    n_heads = d // LANES
    kv_lora = lat_gain.shape[0]
    tm = _tile(seq, 512)
    per_b = seq // tm
    mod_spec = lambda k: pl.BlockSpec((None, 1, d), lambda i: (i // per_b, 0, k))
    full = lambda a: pl.BlockSpec(a.shape, lambda i: (0,) * a.ndim)
    lat_gain = lat_gain.reshape(1, kv_lora)
    return pl.pallas_call(
        functools.partial(_mla_kv_kernel, n_heads=n_heads, kv_lora=kv_lora),
        grid=(t // tm,),
        in_specs=[
            pl.BlockSpec((tm, d), lambda i: (i, 0)),
            pl.BlockSpec((1, d), lambda i: (0, 0)),
            mod_spec(0), mod_spec(1),
            full(wd), full(lat_gain), full(wu), full(gn), full(gr),
            pl.BlockSpec((tm, LANES), lambda i: (i, 0)),
        ],
        out_specs=[
            pl.BlockSpec((None, n_heads, tm, QK_PAD), lambda i: (i // per_b, 0, i % per_b, 0)),
            pl.BlockSpec((None, n_heads, tm, LANES), lambda i: (i // per_b, 0, i % per_b, 0)),
        ],
        out_shape=[jax.ShapeDtypeStruct((bsz, n_heads, seq, QK_PAD), BF16),
                   jax.ShapeDtypeStruct((bsz, n_heads, seq, LANES), BF16)],
        compiler_params=_params("parallel"),
        name="mla_kv",
    )(x, gain.reshape(1, d), mod, mod, wd, lat_gain, wu, gn, gr, cs)


def _mla_q_kernel(x_ref, gain_ref, shift_ref, scale_ref, wd_ref, lat_gain_ref, wu_ref,
                  gn_ref, gr_ref, cs_ref, q_ref, *, n_heads):
    h = _norm_mod(x_ref[...], gain_ref[...], scale_ref[...], shift_ref[...]).astype(BF16)
    cq = jnp.dot(h, wd_ref[...], preferred_element_type=F32)
    cq = (cq * lax.rsqrt(jnp.mean(cq * cq, axis=-1, keepdims=True) + EPS) * lat_gain_ref[...]).astype(BF16)
    gn = gn_ref[...]
    rope_tab = gr_ref[...] * cs_ref[...]
    row = lax.broadcasted_iota(jnp.int32, (QK_PAD, LANES), 0)
    sum_w = jnp.where(row < LANES, 1.0, 0.5).astype(BF16)
    def head_q(hh):
        return jnp.dot(cq, wu_ref[:, hh * QK_PAD:(hh + 1) * QK_PAD], preferred_element_type=F32)

    nxt = head_q(0)
    for hh in range(n_heads):
        qh = nxt
        if hh + 1 < n_heads:
            nxt = head_q(hh + 1)
        qn = qh[:, :LANES]
        qr = qh[:, LANES:]
        ss = jnp.dot((qh * qh).astype(BF16), sum_w, preferred_element_type=F32)
        r = lax.rsqrt(ss * (1.0 / QK) + EPS)
        q_ref[hh, :, :LANES] = (qn * r * gn).astype(BF16)
        q_ref[hh, :, LANES:] = (qr * rope_tab * r).astype(BF16)


def _mla_q(x, gain, mod, wd, lat_gain, wu, gn, gr, cs, bsz, seq):
    t, d = x.shape
    n_heads = d // LANES
    q_lora = lat_gain.shape[0]
    tm = _tile(seq, 512)
    per_b = seq // tm
    mod_spec = lambda k: pl.BlockSpec((None, 1, d), lambda i: (i // per_b, 0, k))
    full = lambda a: pl.BlockSpec(a.shape, lambda i: (0,) * a.ndim)
    lat_gain = lat_gain.reshape(1, q_lora)
    return pl.pallas_call(
        functools.partial(_mla_q_kernel, n_heads=n_heads),
        grid=(t // tm,),
        in_specs=[
            pl.BlockSpec((tm, d), lambda i: (i, 0)),
            pl.BlockSpec((1, d), lambda i: (0, 0)),
            mod_spec(0), mod_spec(1),
            full(wd), full(lat_gain), full(wu), full(gn), full(gr),
            pl.BlockSpec((tm, LANES), lambda i: (i, 0)),
        ],
        out_specs=pl.BlockSpec((None, n_heads, tm, QK_PAD), lambda i: (i // per_b, 0, i % per_b, 0)),
        out_shape=jax.ShapeDtypeStruct((bsz, n_heads, seq, QK_PAD), BF16),
        compiler_params=_params("parallel"),
        name="mla_q",
    )(x, gain.reshape(1, d), mod, mod, wd, lat_gain, wu, gn, gr, cs)


def _flash_kernel(q_ref, k_ref, v_ref, o_ref, s_ref, m_ref, l_ref, acc_ref, *, tq, heads_per_step):
    i = pl.program_id(2)
    qs = [q_ref[hh] for hh in range(heads_per_step)]
    m_ref[...] = jnp.full(m_ref.shape, -jnp.inf, F32)
    l_ref[...] = jnp.zeros_like(l_ref)
    acc_ref[...] = jnp.zeros_like(acc_ref)

    def key_rows(j):
        return pl.ds(pl.multiple_of(j * tq, tq), tq)

    def scores(hh, j):
        return _nt_dot(k_ref[hh, key_rows(j), :], qs[hh])

    def consume(hh, j, masked):
        s = s_ref[hh]
        if masked:
            ki = lax.broadcasted_iota(jnp.int32, (tq, tq), 0)
            qi = lax.broadcasted_iota(jnp.int32, (tq, tq), 1)
            s = jnp.where(ki <= qi, s, -jnp.inf)
        m = m_ref[hh]
        m_new = jnp.maximum(m, jnp.max(s, axis=0, keepdims=True))
        alpha = jnp.exp2(m - m_new)
        p = jnp.exp2(s - m_new)
        m_ref[hh] = m_new
        l_ref[hh] = alpha * l_ref[hh] + jnp.sum(p, axis=0, keepdims=True)
        acc_ref[hh] = alpha * acc_ref[hh] + _tn_dot(v_ref[hh, key_rows(j), :], p.astype(BF16))

    for hh in range(heads_per_step):
        s_ref[hh] = scores(hh, 0)

    def body(j, carry):
        for hh in range(heads_per_step):
            s_next = scores(hh, j + 1)
            consume(hh, j, False)
            s_ref[hh] = s_next
        return carry

    lax.fori_loop(0, i, body, 0)
    for hh in range(heads_per_step):
        consume(hh, i, True)
        o_ref[:, hh * LANES:(hh + 1) * LANES] = (acc_ref[hh] / l_ref[hh]).T.astype(BF16)


def _flash(q, k, v):
    bsz, n_heads, seq, _ = q.shape
    tq = _tile(seq, 512)
    hps = 4 if n_heads % 4 == 0 else (2 if n_heads % 2 == 0 else 1)
    return pl.pallas_call(
        functools.partial(_flash_kernel, tq=tq, heads_per_step=hps),
        grid=(bsz, n_heads // hps, seq // tq),
        in_specs=[
            pl.BlockSpec((None, hps, tq, QK_PAD), lambda b, h, i: (b, h, i, 0)),
            pl.BlockSpec((None, hps, seq, QK_PAD), lambda b, h, i: (b, h, 0, 0)),
            pl.BlockSpec((None, hps, seq, LANES), lambda b, h, i: (b, h, 0, 0)),
        ],
        out_specs=pl.BlockSpec((None, tq, hps * LANES), lambda b, h, i: (b, i, h)),
        out_shape=jax.ShapeDtypeStruct((bsz, seq, n_heads * LANES), BF16),
        scratch_shapes=[pltpu.VMEM((hps, tq, tq), F32),
                        pltpu.VMEM((hps, 1, tq), F32), pltpu.VMEM((hps, 1, tq), F32),
                        pltpu.VMEM((hps, LANES, tq), F32)],
        compiler_params=_params("parallel", "parallel", "arbitrary"),
        name="flash",
    )(q, k, v)


def _swap_halves(a):
    half = a.shape[-1] // 2
    return jnp.concatenate([a[..., half:], a[..., :half]], axis=-1)


def _rope_gain(g):
    return jnp.concatenate([g, _swap_halves(g)]).reshape(1, LANES)


def kernel(x, c, positions, ada_w, ada_b, norm_mix, norm_mlp, mlp_w1, mlp_w2, hg_w_in, hg_lower, hg_o_norm, hg_w_out, kv_ada_w, kv_ada_b, kv_in_norm, mla_w_dkv, mla_kv_norm, mla_w_ukv, mla_k_norm, mla_w_dq, mla_q_lat_norm, mla_w_uq, mla_q_norm, mla_w_o):
    bsz, seq, d = x.shape
    depth = ada_w.shape[0]
    n_a = hg_w_in.shape[0]
    n_heads = d // LANES
    kv_lora = mla_kv_norm.shape[0]
    q_lora = mla_w_dq.shape[2]
    t = bsz * seq

    mod = _ada(c, ada_w, ada_b).reshape(depth, bsz, 1, 6 * d)
    kv_mod = _ada(c, kv_ada_w[None], kv_ada_b[None]).reshape(bsz, 1, 2 * d)

    inv_freq = 1.0 / (ROPE_THETA ** (jnp.arange(0, ROPE, 2, dtype=F32) / ROPE))
    ang = positions.astype(F32)[..., None] * inv_freq
    cos, sin = jnp.cos(ang), jnp.sin(ang)
    cs = jnp.concatenate([cos, cos, -sin, sin], axis=-1).reshape(t, LANES)

    lb_all = jnp.cumsum(jax.nn.softmax(hg_lower.astype(F32), axis=0), axis=0)[:n_a]

    xf = x.reshape(t, d)
    k_sh = v_sh = None
    for layer in range(depth):
        if layer == n_a:
            wd = jnp.concatenate([mla_w_dkv, _swap_halves(mla_w_dkv[:, kv_lora:])], axis=1).astype(BF16)
            k_sh, v_sh = _mla_kv(xf, kv_in_norm, kv_mod, wd, mla_kv_norm, mla_w_ukv.astype(BF16),
                                 mla_k_norm[:LANES].reshape(1, LANES), _rope_gain(mla_k_norm[LANES:]),
                                 cs, bsz, seq)
        if layer < n_a:
            q, k, lf, v, sg = _hg_in(xf, norm_mix[layer], mod[layer], lb_all[layer],
                                     hg_w_in[layer].astype(BF16), bsz, seq)
            y = _hg_rec(q, k, lf, v, sg, hg_o_norm[layer]).reshape(t, d)
            xf = _proj_res(y, hg_w_out[layer].astype(BF16), xf, mod[layer], 2, seq)
        else:
            j = layer - n_a
            wu = mla_w_uq[j].reshape(q_lora, n_heads, QK)
            wu = jnp.concatenate([wu, _swap_halves(wu[..., LANES:])], axis=-1)
            wu = wu.reshape(q_lora, n_heads * QK_PAD).astype(BF16)
            sm_scale = float(QK) ** -0.5 * LOG2E
            qg = mla_q_norm[j] * sm_scale
            qh = _mla_q(xf, norm_mix[layer], mod[layer], mla_w_dq[j].astype(BF16), mla_q_lat_norm[j], wu,
                        qg[:LANES].reshape(1, LANES), _rope_gain(qg[LANES:]), cs, bsz, seq)
            y = _flash(qh, k_sh, v_sh).reshape(t, d)
            xf = _proj_res(y, mla_w_o[j].astype(BF16), xf, mod[layer], 2, seq)
        xf = _mlp(xf, norm_mlp[layer], mod[layer], mlp_w1[layer].astype(BF16), mlp_w2[layer].astype(BF16), seq)
    return xf.reshape(bsz, seq, d)
```

```python
import functools

import jax
import jax.numpy as jnp
from jax import lax
from jax.experimental import pallas as pl
from jax.experimental.pallas import tpu as pltpu

F32 = jnp.float32
BF16 = jnp.bfloat16

EPS = 1e-6
LANES = 128
ROPE = 64
QK = LANES + ROPE
QK_PAD = 2 * LANES
CHUNK = 64
ATTN_TILE = 512
MAX_FACTORED_DECAY = 72.0
ROPE_THETA = 10000.0
LOG2E = 1.4426950408889634
VMEM_LIMIT = 56 * 1024 * 1024


def _params(*sem):
    return pltpu.CompilerParams(dimension_semantics=sem, vmem_limit_bytes=VMEM_LIMIT)


def _tile(n, pref):
    t = min(n, pref)
    while n % t:
        t -= LANES
    assert t > 0 and n % t == 0, (n, pref)
    return t


def _norm_mod(x, gain, scale, shift):
    r = lax.rsqrt(jnp.mean(x * x, axis=-1, keepdims=True) + EPS)
    return (x * r) * (gain * (1.0 + scale)) + shift


def _ada_kernel(c_ref, w_ref, b_ref, o_ref):
    c = c_ref[...]
    cond = (c / (1.0 + jnp.exp(-c))).astype(BF16)
    o_ref[...] = jnp.dot(cond, w_ref[...].astype(BF16), preferred_element_type=F32) + b_ref[...]


def _ada(c, w, b):
    n_layers, d, n = w.shape
    bsz = c.shape[0]
    tn = _tile(n, 1024)
    return pl.pallas_call(
        _ada_kernel,
        grid=(n_layers, n // tn),
        in_specs=[
            pl.BlockSpec((bsz, d), lambda l, j: (0, 0)),
            pl.BlockSpec((None, d, tn), lambda l, j: (l, 0, j)),
            pl.BlockSpec((None, 1, tn), lambda l, j: (l, 0, j)),
        ],
        out_specs=pl.BlockSpec((None, bsz, tn), lambda l, j: (l, 0, j)),
        out_shape=jax.ShapeDtypeStruct((n_layers, bsz, n), F32),
        compiler_params=_params("parallel", "parallel"),
        name="ada",
    )(c, w, b.reshape(n_layers, 1, n))


def _mlp_kernel(x_ref, xn_ref, gain_ref, shift_ref, scale_ref, gate_ref, shiftn_ref, scalen_ref,
                w1_ref, w2_ref, o_ref, h_ref):
    i = pl.program_id(0)
    j = pl.program_id(1)
    last = pl.num_programs(1) - 1
    slot = i % 2

    @pl.when(jnp.logical_and(i == 0, j == 0))
    def _():
        h_ref[0] = _norm_mod(x_ref[...], gain_ref[...], scale_ref[...], shift_ref[...]).astype(BF16)

    def gated_chunk():
        a = jnp.maximum(jnp.dot(h_ref[slot], w1_ref[...], preferred_element_type=F32), 0.0)
        return gate_ref[...] * jnp.dot((a * a).astype(BF16), w2_ref[...], preferred_element_type=F32)

    @pl.when(j == 0)
    def _():
        o_ref[...] = x_ref[...] + gated_chunk()

    @pl.when(jnp.logical_and(j > 0, j < last))
    def _():
        o_ref[...] += gated_chunk()

    @pl.when(j == last)
    def _():
        o_ref[...] += gated_chunk()
        h_ref[1 - slot] = _norm_mod(xn_ref[...], gain_ref[...], scalen_ref[...], shiftn_ref[...]).astype(BF16)


def _mlp(x, gain, mod, w1, w2, seq):
    t, d = x.shape
    f = w1.shape[1]
    tm = _tile(seq, 512)
    tf = _tile(f, min(1024, f // 2))
    per_b = seq // tm
    n_i = t // tm
    nxt = lambda i: jnp.minimum(i + 1, n_i - 1)
    mod_spec = lambda k: pl.BlockSpec((None, 1, d), lambda i, j: (i // per_b, 0, k))
    modn_spec = lambda k: pl.BlockSpec((None, 1, d), lambda i, j: (nxt(i) // per_b, 0, k))
    return pl.pallas_call(
        _mlp_kernel,
        grid=(n_i, f // tf),
        in_specs=[
            pl.BlockSpec((tm, d), lambda i, j: (i, 0)),
            pl.BlockSpec((tm, d), lambda i, j: (nxt(i), 0)),
            pl.BlockSpec((1, d), lambda i, j: (0, 0)),
            mod_spec(3), mod_spec(4), mod_spec(5), modn_spec(3), modn_spec(4),
            pl.BlockSpec((d, tf), lambda i, j: (0, j)),
            pl.BlockSpec((tf, d), lambda i, j: (j, 0)),
        ],
        out_specs=pl.BlockSpec((tm, d), lambda i, j: (i, 0)),
        out_shape=jax.ShapeDtypeStruct((t, d), F32),
        scratch_shapes=[pltpu.VMEM((2, tm, d), BF16)],
        compiler_params=_params("arbitrary", "arbitrary"),
        name="mlp",
    )(x, x, gain.reshape(1, d), mod, mod, mod, mod, mod, w1, w2)


def _proj_res_kernel(y_ref, w_ref, x_ref, gate_ref, o_ref):
    o_ref[...] = x_ref[...] + gate_ref[...] * jnp.dot(y_ref[...], w_ref[...], preferred_element_type=F32)


def _proj_res(y, w, x, mod, gate_idx, seq):
    t, d = x.shape
    k = y.shape[1]
    tm = _tile(seq, 512)
    tn = _tile(d, 2048)
    per_b = seq // tm
    return pl.pallas_call(
        _proj_res_kernel,
        grid=(t // tm, d // tn),
        in_specs=[
            pl.BlockSpec((tm, k), lambda i, j: (i, 0)),
            pl.BlockSpec((k, tn), lambda i, j: (0, j)),
            pl.BlockSpec((tm, tn), lambda i, j: (i, j)),
            pl.BlockSpec((None, 1, tn), lambda i, j: (i // per_b, 0, gate_idx * (d // tn) + j)),
        ],
        out_specs=pl.BlockSpec((tm, tn), lambda i, j: (i, j)),
        out_shape=jax.ShapeDtypeStruct((t, d), F32),
        compiler_params=_params("parallel", "parallel"),
        name="proj_res",
    )(y, w, x, mod)


def _hg_in_kernel(x_ref, gain_ref, shift_ref, scale_ref, llb_ref, l1mlb_ref, omlb_ref,
                  wq_ref, wf_ref, wi_ref, wg_ref,
                  q_ref, k_ref, lf_ref, v_ref, sg_ref, h_ref, *, heads_per_step, q_scale):
    @pl.when(pl.program_id(1) == 0)
    def _():
        h_ref[...] = _norm_mod(x_ref[...], gain_ref[...], scale_ref[...], shift_ref[...]).astype(BF16)

    h = h_ref[...]
    z = jnp.dot(h, wf_ref[...], preferred_element_type=F32)
    q = jnp.dot(h, wq_ref[...], preferred_element_type=F32) * q_scale
    v = jnp.dot(h, wi_ref[...], preferred_element_type=F32)
    g = jnp.dot(h, wg_ref[...], preferred_element_type=F32)

    e = jnp.exp(-jnp.abs(z))
    log_sig = jnp.minimum(z, 0.0) - jnp.log(1.0 + e)
    a = llb_ref[...]
    c = l1mlb_ref[...] + log_sig
    lf = jnp.maximum(a, c) + jnp.log(1.0 + jnp.exp(-jnp.abs(a - c)))
    key = omlb_ref[...] * (jnp.where(z >= 0.0, e, 1.0) / (1.0 + e))
    sg = g / (1.0 + jnp.exp(-g))

    for hh in range(heads_per_step):
        cs = slice(hh * LANES, (hh + 1) * LANES)
        q_ref[hh] = q[:, cs].astype(BF16)
        k_ref[hh] = key[:, cs].astype(BF16)
        lf_ref[hh] = lf[:, cs]
        v_ref[hh] = v[:, cs].astype(BF16)
        sg_ref[hh] = sg[:, cs].astype(BF16)


def _hg_in(x, gain, mod, lb, w_in, bsz, seq):
    t, d = x.shape
    n_heads = d // LANES
    tm = _tile(seq, 1024)
    tn = _tile(d, 2 * LANES)
    hps = tn // LANES
    nj = d // tn
    per_b = seq // tm
    mod_spec = lambda k: pl.BlockSpec((None, 1, d), lambda i, j: (i // per_b, 0, k))
    vec_spec = pl.BlockSpec((1, tn), lambda i, j: (0, j))
    w_spec = lambda grp: pl.BlockSpec((d, tn), lambda i, j: (0, grp * nj + j))
    out_spec = pl.BlockSpec((None, hps, tm, LANES), lambda i, j: (i // per_b, j, i % per_b, 0))
    hm = lambda dt: jax.ShapeDtypeStruct((bsz, n_heads, seq, LANES), dt)
    lbf = lb.reshape(1, d)
    kern = functools.partial(_hg_in_kernel, heads_per_step=hps, q_scale=float(LANES) ** -0.5)
    return pl.pallas_call(
        kern,
        grid=(t // tm, nj),
        in_specs=[
            pl.BlockSpec((tm, d), lambda i, j: (i, 0)),
            pl.BlockSpec((1, d), lambda i, j: (0, 0)),
            mod_spec(0), mod_spec(1),
            vec_spec, vec_spec, vec_spec,
            w_spec(0), w_spec(1), w_spec(2), w_spec(3),
        ],
        out_specs=[out_spec] * 5,
        out_shape=[hm(BF16), hm(BF16), hm(F32), hm(BF16), hm(BF16)],
        scratch_shapes=[pltpu.VMEM((tm, d), BF16)],
        compiler_params=_params("parallel", "arbitrary"),
        name="hg_in",
    )(x, gain.reshape(1, d), mod, mod, jnp.log(lbf), jnp.log1p(-lbf), 1.0 - lbf, w_in, w_in, w_in, w_in)


def _nt_dot(a, b):
    return lax.dot_general(a, b, (((1,), (1,)), ((), ())), preferred_element_type=F32)


def _tn_dot(a, b):
    return lax.dot_general(a, b, (((0,), (0,)), ((), ())), preferred_element_type=F32)


def _chunk_scores(q, k, b, lf):
    row = lax.broadcasted_iota(jnp.int32, (CHUNK, LANES), 0)
    ti = lax.broadcasted_iota(jnp.int32, (CHUNK, CHUNK), 0)
    si = lax.broadcasted_iota(jnp.int32, (CHUNK, CHUNK), 1)
    scores = jnp.where(ti == si, _nt_dot(q.astype(BF16), k.astype(BF16)), 0.0)
    start = b - lf
    c = 1
    while c < CHUNK:
        later = (row & c) != 0
        bound = jnp.where(later, start, pltpu.roll(start, CHUNK - c, axis=0))
        e = jnp.exp(jnp.where(later, b - bound, bound - b))
        m = jnp.where(later, q, k) * e
        qs = jnp.where(later, m, 0.0).astype(BF16)
        ks = jnp.where(later, 0.0, m).astype(BF16)
        same_parent = (ti & -(2 * c)) == (si & -(2 * c))
        scores = scores + jnp.where(same_parent, _nt_dot(qs, ks), 0.0)
        if 2 * c < CHUNK:
            start = jnp.where(later, pltpu.roll(start, c, axis=0), start)
        c *= 2
    return scores


def _cumsum_rows(lf, tri):
    hi = lf.astype(BF16)
    r1 = lf - hi.astype(F32)
    mid = r1.astype(BF16)
    lo = (r1 - mid.astype(F32)).astype(BF16)
    dot = lambda part: jnp.dot(tri, part, preferred_element_type=F32)
    return dot(hi) + dot(mid) + dot(lo)


def _rec_kernel(q_ref, k_ref, lf_ref, v_ref, sg_ref, og_ref, o_ref, st_ref, *, n_chunks):
    @pl.when(pl.program_id(2) == 0)
    def _():
        st_ref[...] = jnp.zeros_like(st_ref)

    ti = lax.broadcasted_iota(jnp.int32, (CHUNK, CHUNK), 0)
    si = lax.broadcasted_iota(jnp.int32, (CHUNK, CHUNK), 1)
    causal = si <= ti
    tri = jnp.where(causal, 1.0, 0.0).astype(BF16)
    og = og_ref[...]

    def rows(ci):
        return slice(ci * CHUNK, (ci + 1) * CHUNK)

    mean_w = jnp.full((LANES, LANES), 1.0 / LANES, BF16)

    def finish(outs):
        ms = [jnp.dot((o * o).astype(BF16), mean_w, preferred_element_type=F32) for o in outs]
        for ci, o in enumerate(outs):
            o = o * lax.rsqrt(ms[ci] + EPS) * og
            o_ref[rows(ci), :] = (o * sg_ref[rows(ci), :].astype(F32)).astype(BF16)

    cs = range(n_chunks)
    b = [_cumsum_rows(lf_ref[rows(ci), :], tri) for ci in cs]
    total = b[0][CHUNK - 1:CHUNK, :]
    for ci in range(1, n_chunks):
        total = jnp.minimum(total, b[ci][CHUNK - 1:CHUNK, :])
    mild = jnp.min(total) >= -MAX_FACTORED_DECAY
    q_dec = [(q_ref[rows(ci), :].astype(F32) * jnp.exp(b[ci])).astype(BF16) for ci in cs]
    decay = [jnp.exp(b[ci][CHUNK - 1:CHUNK, :]) for ci in cs]

    @pl.when(mild)
    def _():
        k_grow = [k_ref[rows(ci), :].astype(F32) * jnp.exp(-b[ci]) for ci in cs]
        scores = [jnp.where(causal, _nt_dot(q_dec[ci], k_grow[ci].astype(BF16)), 0.0).astype(BF16) for ci in cs]
        intra = [jnp.dot(scores[ci], v_ref[rows(ci), :], preferred_element_type=F32) for ci in cs]
        update = [_tn_dot(v_ref[rows(ci), :], (k_grow[ci] * decay[ci]).astype(BF16)) for ci in cs]
        states = [st_ref[...]]
        for ci in cs:
            states.append(states[ci] * decay[ci] + update[ci])
        finish([intra[ci] + _nt_dot(q_dec[ci], states[ci].astype(BF16)) for ci in cs])
        st_ref[...] = states[n_chunks]

    @pl.when(jnp.logical_not(mild))
    def _():
        state_t = st_ref[...]
        outs = []
        for ci in range(n_chunks):
            lf = lf_ref[rows(ci), :]
            q = q_ref[rows(ci), :].astype(F32)
            k = k_ref[rows(ci), :].astype(F32)
            v = v_ref[rows(ci), :]
            scores = _chunk_scores(q, k, b[ci], lf)
            o = jnp.dot(scores.astype(BF16), v, preferred_element_type=F32)
            o = o + _nt_dot(q_dec[ci], state_t.astype(BF16))
            k_dec = (k * jnp.exp(b[ci][CHUNK - 1:CHUNK, :] - b[ci])).astype(BF16)
            state_t = state_t * decay[ci] + _tn_dot(v, k_dec)
            outs.append(o)
        finish(outs)
        st_ref[...] = state_t


def _hg_rec(q, k, lf, v, sg, o_gain):
    bsz, n_heads, seq, _ = q.shape
    ts = _tile(seq, 1024)
    in_spec = pl.BlockSpec((None, None, ts, LANES), lambda b, h, s: (b, h, s, 0))
    return pl.pallas_call(
        functools.partial(_rec_kernel, n_chunks=ts // CHUNK),
        grid=(bsz, n_heads, seq // ts),
        in_specs=[in_spec] * 5 + [pl.BlockSpec((1, LANES), lambda b, h, s: (0, 0))],
        out_specs=pl.BlockSpec((None, ts, LANES), lambda b, h, s: (b, s, h)),
        out_shape=jax.ShapeDtypeStruct((bsz, seq, n_heads * LANES), BF16),
        scratch_shapes=[pltpu.VMEM((LANES, LANES), F32)],
        compiler_params=_params("parallel", "parallel", "arbitrary"),
        name="hg_rec",
    )(q, k, lf, v, sg, o_gain.reshape(1, LANES))


def _rope_pair(u):
    return u + pltpu.roll(u, ROPE, axis=1)


def _mla_kv_kernel(x_ref, gain_ref, shift_ref, scale_ref, wd_ref, lat_gain_ref, wu_ref,
                   gn_ref, gr_ref, cs_ref, k_ref, v_ref, *, n_heads, kv_lora):
    h = _norm_mod(x_ref[...], gain_ref[...], scale_ref[...], shift_ref[...]).astype(BF16)
    ckv = jnp.dot(h, wd_ref[...], preferred_element_type=F32)
    lat = ckv[:, :kv_lora]
    pe2 = ckv[:, kv_lora:]
    c_lat = (lat * lax.rsqrt(jnp.mean(lat * lat, axis=-1, keepdims=True) + EPS) * lat_gain_ref[...]).astype(BF16)
    ones = jnp.ones((LANES, LANES), BF16)
    ss_pe = 0.5 * jnp.dot((pe2 * pe2).astype(BF16), ones, preferred_element_type=F32)
    rot = _rope_pair(pe2 * gr_ref[...] * cs_ref[...])
    gn = gn_ref[...]
    def head_kv(hh):
        return jnp.dot(c_lat, wu_ref[:, hh * QK_PAD:(hh + 1) * QK_PAD], preferred_element_type=F32)

    nxt = head_kv(0)
    for hh in range(n_heads):
        kv = nxt
        if hh + 1 < n_heads:
            nxt = head_kv(hh + 1)
        kn = kv[:, :LANES]
        ss = jnp.dot((kn * kn).astype(BF16), ones, preferred_element_type=F32) + ss_pe
        r = lax.rsqrt(ss * (1.0 / QK) + EPS)
        k_ref[hh, :, :LANES] = (kn * r * gn).astype(BF16)
        k_ref[hh, :, LANES:] = (rot * r).astype(BF16)
        v_ref[hh] = kv[:, LANES:].T.astype(BF16)


def _mla_kv(x, gain, mod, wd, lat_gain, wu, gn, gr, cs, bsz, seq):
    t, d = x.shape
    n_heads = d // LANES
    kv_lora = lat_gain.shape[0]
    tm = _tile(seq, ATTN_TILE)
    per_b = seq // tm
    mod_spec = lambda k: pl.BlockSpec((None, 1, d), lambda i: (i // per_b, 0, k))
    full = lambda a: pl.BlockSpec(a.shape, lambda i: (0,) * a.ndim)
    lat_gain = lat_gain.reshape(1, kv_lora)
    return pl.pallas_call(
        functools.partial(_mla_kv_kernel, n_heads=n_heads, kv_lora=kv_lora),
        grid=(t // tm,),
        in_specs=[
            pl.BlockSpec((tm, d), lambda i: (i, 0)),
            pl.BlockSpec((1, d), lambda i: (0, 0)),
            mod_spec(0), mod_spec(1),
            full(wd), full(lat_gain), full(wu), full(gn), full(gr),
            pl.BlockSpec((tm, LANES), lambda i: (i, 0)),
        ],
        out_specs=[
            pl.BlockSpec((None, n_heads, tm, QK_PAD), lambda i: (i // per_b, 0, i % per_b, 0)),
            pl.BlockSpec((None, n_heads, None, LANES, tm), lambda i: (i // per_b, 0, i % per_b, 0, 0)),
        ],
        out_shape=[jax.ShapeDtypeStruct((bsz, n_heads, seq, QK_PAD), BF16),
                   jax.ShapeDtypeStruct((bsz, n_heads, seq // tm, LANES, tm), BF16)],
        compiler_params=_params("parallel"),
        name="mla_kv",
    )(x, gain.reshape(1, d), mod, mod, wd, lat_gain, wu, gn, gr, cs)


def _mla_q_kernel(x_ref, gain_ref, shift_ref, scale_ref, wd_ref, lat_gain_ref, wu_ref,
                  gn_ref, gr_ref, cs_ref, q_ref, *, n_heads):
    h = _norm_mod(x_ref[...], gain_ref[...], scale_ref[...], shift_ref[...]).astype(BF16)
    cq = jnp.dot(h, wd_ref[...], preferred_element_type=F32)
    cq = (cq * lax.rsqrt(jnp.mean(cq * cq, axis=-1, keepdims=True) + EPS) * lat_gain_ref[...]).astype(BF16)
    gn = gn_ref[...]
    rope_tab = gr_ref[...] * cs_ref[...]
    row = lax.broadcasted_iota(jnp.int32, (QK_PAD, LANES), 0)
    sum_w = jnp.where(row < LANES, 1.0, 0.5).astype(BF16)
    def head_q(hh):
        return jnp.dot(cq, wu_ref[:, hh * QK_PAD:(hh + 1) * QK_PAD], preferred_element_type=F32)

    nxt = head_q(0)
    for hh in range(n_heads):
        qh = nxt
        if hh + 1 < n_heads:
            nxt = head_q(hh + 1)
        qn = qh[:, :LANES]
        qr = qh[:, LANES:]
        ss = jnp.dot((qh * qh).astype(BF16), sum_w, preferred_element_type=F32)
        r = lax.rsqrt(ss * (1.0 / QK) + EPS)
        q_ref[hh, :, :LANES] = (qn * r * gn).astype(BF16)
        q_ref[hh, :, LANES:] = (qr * rope_tab * r).astype(BF16)


def _mla_q(x, gain, mod, wd, lat_gain, wu, gn, gr, cs, bsz, seq):
    t, d = x.shape
    n_heads = d // LANES
    q_lora = lat_gain.shape[0]
    tm = _tile(seq, 512)
    per_b = seq // tm
    mod_spec = lambda k: pl.BlockSpec((None, 1, d), lambda i: (i // per_b, 0, k))
    full = lambda a: pl.BlockSpec(a.shape, lambda i: (0,) * a.ndim)
    lat_gain = lat_gain.reshape(1, q_lora)
    return pl.pallas_call(
        functools.partial(_mla_q_kernel, n_heads=n_heads),
        grid=(t // tm,),
        in_specs=[
            pl.BlockSpec((tm, d), lambda i: (i, 0)),
            pl.BlockSpec((1, d), lambda i: (0, 0)),
            mod_spec(0), mod_spec(1),
            full(wd), full(lat_gain), full(wu), full(gn), full(gr),
            pl.BlockSpec((tm, LANES), lambda i: (i, 0)),
        ],
        out_specs=pl.BlockSpec((None, n_heads, tm, QK_PAD), lambda i: (i // per_b, 0, i % per_b, 0)),
        out_shape=jax.ShapeDtypeStruct((bsz, n_heads, seq, QK_PAD), BF16),
        compiler_params=_params("parallel"),
        name="mla_q",
    )(x, gain.reshape(1, d), mod, mod, wd, lat_gain, wu, gn, gr, cs)


def _flash_kernel(q_ref, k_ref, vt_ref, o_ref, s_ref, m_ref, l_ref, acc_ref, *, tq, heads_per_step):
    i = pl.program_id(2)
    qs = [q_ref[hh] for hh in range(heads_per_step)]
    m_ref[...] = jnp.full(m_ref.shape, -jnp.inf, F32)
    l_ref[...] = jnp.zeros_like(l_ref)
    acc_ref[...] = jnp.zeros_like(acc_ref)

    def key_rows(j):
        return pl.ds(pl.multiple_of(j * tq, tq), tq)

    def scores(hh, j):
        return _nt_dot(k_ref[hh, key_rows(j), :], qs[hh])

    def consume(hh, j, masked):
        s = s_ref[hh]
        if masked:
            ki = lax.broadcasted_iota(jnp.int32, (tq, tq), 0)
            qi = lax.broadcasted_iota(jnp.int32, (tq, tq), 1)
            s = jnp.where(ki <= qi, s, -jnp.inf)
        m = m_ref[hh]
        m_new = jnp.maximum(m, jnp.max(s, axis=0, keepdims=True))
        alpha = jnp.exp2(m - m_new)
        p = jnp.exp2(s - m_new)
        m_ref[hh] = m_new
        l_ref[hh] = alpha * l_ref[hh] + jnp.sum(p, axis=0, keepdims=True)
        pv = jnp.dot(vt_ref[hh, j], p.astype(BF16), preferred_element_type=F32)
        acc_ref[hh] = alpha * acc_ref[hh] + pv

    for hh in range(heads_per_step):
        s_ref[hh] = scores(hh, 0)

    def body(j, carry):
        for hh in range(heads_per_step):
            s_next = scores(hh, j + 1)
            consume(hh, j, False)
            s_ref[hh] = s_next
        return carry

    lax.fori_loop(0, i, body, 0)
    for hh in range(heads_per_step):
        consume(hh, i, True)
        o_ref[:, hh * LANES:(hh + 1) * LANES] = (acc_ref[hh] / l_ref[hh]).T.astype(BF16)


def _flash(q, k, vt):
    bsz, n_heads, seq, _ = q.shape
    tq = vt.shape[-1]
    hps = 4 if n_heads % 4 == 0 else (2 if n_heads % 2 == 0 else 1)
    return pl.pallas_call(
        functools.partial(_flash_kernel, tq=tq, heads_per_step=hps),
        grid=(bsz, n_heads // hps, seq // tq),
        in_specs=[
            pl.BlockSpec((None, hps, tq, QK_PAD), lambda b, h, i: (b, h, i, 0)),
            pl.BlockSpec((None, hps, seq, QK_PAD), lambda b, h, i: (b, h, 0, 0)),
            pl.BlockSpec((None, hps, seq // tq, LANES, tq), lambda b, h, i: (b, h, 0, 0, 0)),
        ],
        out_specs=pl.BlockSpec((None, tq, hps * LANES), lambda b, h, i: (b, i, h)),
        out_shape=jax.ShapeDtypeStruct((bsz, seq, n_heads * LANES), BF16),
        scratch_shapes=[pltpu.VMEM((hps, tq, tq), F32),
                        pltpu.VMEM((hps, 1, tq), F32), pltpu.VMEM((hps, 1, tq), F32),
                        pltpu.VMEM((hps, LANES, tq), F32)],
        compiler_params=_params("parallel", "parallel", "arbitrary"),
        name="flash",
    )(q, k, vt)


def _swap_halves(a):
    half = a.shape[-1] // 2
    return jnp.concatenate([a[..., half:], a[..., :half]], axis=-1)


def _rope_gain(g):
    return jnp.concatenate([g, _swap_halves(g)]).reshape(1, LANES)


def kernel(x, c, positions, ada_w, ada_b, norm_mix, norm_mlp, mlp_w1, mlp_w2, hg_w_in, hg_lower, hg_o_norm, hg_w_out, kv_ada_w, kv_ada_b, kv_in_norm, mla_w_dkv, mla_kv_norm, mla_w_ukv, mla_k_norm, mla_w_dq, mla_q_lat_norm, mla_w_uq, mla_q_norm, mla_w_o):
    bsz, seq, d = x.shape
    depth = ada_w.shape[0]
    n_a = hg_w_in.shape[0]
    n_heads = d // LANES
    kv_lora = mla_kv_norm.shape[0]
    q_lora = mla_w_dq.shape[2]
    t = bsz * seq

    mod = _ada(c, ada_w, ada_b).reshape(depth, bsz, 1, 6 * d)
    kv_mod = _ada(c, kv_ada_w[None], kv_ada_b[None]).reshape(bsz, 1, 2 * d)

    inv_freq = 1.0 / (ROPE_THETA ** (jnp.arange(0, ROPE, 2, dtype=F32) / ROPE))
    ang = positions.astype(F32)[..., None] * inv_freq
    cos, sin = jnp.cos(ang), jnp.sin(ang)
    cs = jnp.concatenate([cos, cos, -sin, sin], axis=-1).reshape(t, LANES)

    lb_all = jnp.cumsum(jax.nn.softmax(hg_lower.astype(F32), axis=0), axis=0)[:n_a]

    xf = x.reshape(t, d)
    k_sh = v_sh = None
    for layer in range(depth):
        if layer == n_a:
            wd = jnp.concatenate([mla_w_dkv, _swap_halves(mla_w_dkv[:, kv_lora:])], axis=1).astype(BF16)
            k_sh, v_sh = _mla_kv(xf, kv_in_norm, kv_mod, wd, mla_kv_norm, mla_w_ukv.astype(BF16),
                                 mla_k_norm[:LANES].reshape(1, LANES), _rope_gain(mla_k_norm[LANES:]),
                                 cs, bsz, seq)
        if layer < n_a:
            q, k, lf, v, sg = _hg_in(xf, norm_mix[layer], mod[layer], lb_all[layer],
                                     hg_w_in[layer].astype(BF16), bsz, seq)
            y = _hg_rec(q, k, lf, v, sg, hg_o_norm[layer]).reshape(t, d)
            xf = _proj_res(y, hg_w_out[layer].astype(BF16), xf, mod[layer], 2, seq)
        else:
            j = layer - n_a
            wu = mla_w_uq[j].reshape(q_lora, n_heads, QK)
            wu = jnp.concatenate([wu, _swap_halves(wu[..., LANES:])], axis=-1)
            wu = wu.reshape(q_lora, n_heads * QK_PAD).astype(BF16)
            sm_scale = float(QK) ** -0.5 * LOG2E
            qg = mla_q_norm[j] * sm_scale
            qh = _mla_q(xf, norm_mix[layer], mod[layer], mla_w_dq[j].astype(BF16), mla_q_lat_norm[j], wu,
                        qg[:LANES].reshape(1, LANES), _rope_gain(qg[LANES:]), cs, bsz, seq)
            y = _flash(qh, k_sh, v_sh).reshape(t, d)
            xf = _proj_res(y, mla_w_o[j].astype(BF16), xf, mod[layer], 2, seq)
        xf = _mlp(xf, norm_mlp[layer], mod[layer], mlp_w1[layer].astype(BF16), mlp_w2[layer].astype(BF16), seq)
    return xf.reshape(bsz, seq, d)
```

```python
import functools

import jax
import jax.numpy as jnp
from jax import lax
from jax.experimental import pallas as pl
from jax.experimental.pallas import tpu as pltpu

F32 = jnp.float32
BF16 = jnp.bfloat16

EPS = 1e-6
LANES = 128
ROPE = 64
QK = LANES + ROPE
QK_PAD = 2 * LANES
CHUNK = 64
ATTN_TILE = 512
MAX_FACTORED_DECAY = 72.0
ROPE_THETA = 10000.0
LOG2E = 1.4426950408889634
VMEM_LIMIT = 56 * 1024 * 1024


def _params(*sem):
    return pltpu.CompilerParams(dimension_semantics=sem, vmem_limit_bytes=VMEM_LIMIT)


def _tile(n, pref):
    t = min(n, pref)
    while n % t:
        t -= LANES
    assert t > 0 and n % t == 0, (n, pref)
    return t


def _norm_mod(x, gain, scale, shift):
    r = lax.rsqrt(jnp.mean(x * x, axis=-1, keepdims=True) + EPS)
    return (x * r) * (gain * (1.0 + scale)) + shift


def _ada_kernel(c_ref, w_ref, b_ref, o_ref):
    c = c_ref[...]
    cond = (c / (1.0 + jnp.exp(-c))).astype(BF16)
    o_ref[...] = jnp.dot(cond, w_ref[...].astype(BF16), preferred_element_type=F32) + b_ref[...]


def _ada(c, w, b):
    n_layers, d, n = w.shape
    bsz = c.shape[0]
    tn = _tile(n, 1024)
    return pl.pallas_call(
        _ada_kernel,
        grid=(n_layers, n // tn),
        in_specs=[
            pl.BlockSpec((bsz, d), lambda l, j: (0, 0)),
            pl.BlockSpec((None, d, tn), lambda l, j: (l, 0, j)),
            pl.BlockSpec((None, 1, tn), lambda l, j: (l, 0, j)),
        ],
        out_specs=pl.BlockSpec((None, bsz, tn), lambda l, j: (l, 0, j)),
        out_shape=jax.ShapeDtypeStruct((n_layers, bsz, n), F32),
        compiler_params=_params("parallel", "parallel"),
        name="ada",
    )(c, w, b.reshape(n_layers, 1, n))


def _mlp_kernel(x_ref, gain_ref, shift_ref, scale_ref, gate_ref, w1_ref, w2_ref, o_ref, h_ref):
    j = pl.program_id(1)

    def gated_chunk():
        a = jnp.maximum(jnp.dot(h_ref[...], w1_ref[...], preferred_element_type=F32), 0.0)
        return gate_ref[...] * jnp.dot((a * a).astype(BF16), w2_ref[...], preferred_element_type=F32)

    @pl.when(j == 0)
    def _():
        h_ref[...] = _norm_mod(x_ref[...], gain_ref[...], scale_ref[...], shift_ref[...]).astype(BF16)
        o_ref[...] = x_ref[...] + gated_chunk()

    @pl.when(j > 0)
    def _():
        o_ref[...] += gated_chunk()


def _mlp(x, gain, mod, w1, w2, seq):
    t, d = x.shape
    f = w1.shape[1]
    tm = _tile(seq, 1024)
    tf = _tile(f, 512)
    per_b = seq // tm
    mod_spec = lambda k: pl.BlockSpec((None, 1, d), lambda i, j: (i // per_b, 0, k))
    return pl.pallas_call(
        _mlp_kernel,
        grid=(t // tm, f // tf),
        in_specs=[
            pl.BlockSpec((tm, d), lambda i, j: (i, 0)),
            pl.BlockSpec((1, d), lambda i, j: (0, 0)),
            mod_spec(3), mod_spec(4), mod_spec(5),
            pl.BlockSpec((d, tf), lambda i, j: (0, j)),
            pl.BlockSpec((tf, d), lambda i, j: (j, 0)),
        ],
        out_specs=pl.BlockSpec((tm, d), lambda i, j: (i, 0)),
        out_shape=jax.ShapeDtypeStruct((t, d), F32),
        scratch_shapes=[pltpu.VMEM((tm, d), BF16)],
        compiler_params=_params("parallel", "arbitrary"),
        name="mlp",
    )(x, gain.reshape(1, d), mod, mod, mod, w1, w2)


def _proj_res_kernel(y_ref, w_ref, x_ref, gate_ref, o_ref):
    o_ref[...] = x_ref[...] + gate_ref[...] * jnp.dot(y_ref[...], w_ref[...], preferred_element_type=F32)


def _proj_res(y, w, x, mod, gate_idx, seq):
    t, d = x.shape
    k = y.shape[1]
    tm = _tile(seq, 512)
    tn = _tile(d, 2048)
    per_b = seq // tm
    return pl.pallas_call(
        _proj_res_kernel,
        grid=(t // tm, d // tn),
        in_specs=[
            pl.BlockSpec((tm, k), lambda i, j: (i, 0)),
            pl.BlockSpec((k, tn), lambda i, j: (0, j)),
            pl.BlockSpec((tm, tn), lambda i, j: (i, j)),
            pl.BlockSpec((None, 1, tn), lambda i, j: (i // per_b, 0, gate_idx * (d // tn) + j)),
        ],
        out_specs=pl.BlockSpec((tm, tn), lambda i, j: (i, j)),
        out_shape=jax.ShapeDtypeStruct((t, d), F32),
        compiler_params=_params("parallel", "parallel"),
        name="proj_res",
    )(y, w, x, mod)


def _hg_in_kernel(x_ref, gain_ref, shift_ref, scale_ref, llb_ref, l1mlb_ref, omlb_ref,
                  wq_ref, wf_ref, wi_ref, wg_ref,
                  q_ref, k_ref, lf_ref, v_ref, sg_ref, h_ref, *, heads_per_step, q_scale):
    @pl.when(pl.program_id(1) == 0)
    def _():
        h_ref[...] = _norm_mod(x_ref[...], gain_ref[...], scale_ref[...], shift_ref[...]).astype(BF16)

    h = h_ref[...]
    z = jnp.dot(h, wf_ref[...], preferred_element_type=F32)
    q = jnp.dot(h, wq_ref[...], preferred_element_type=F32) * q_scale
    v = jnp.dot(h, wi_ref[...], preferred_element_type=F32)
    g = jnp.dot(h, wg_ref[...], preferred_element_type=F32)

    e = jnp.exp(-jnp.abs(z))
    log_sig = jnp.minimum(z, 0.0) - jnp.log(1.0 + e)
    a = llb_ref[...]
    c = l1mlb_ref[...] + log_sig
    lf = jnp.maximum(a, c) + jnp.log(1.0 + jnp.exp(-jnp.abs(a - c)))
    key = omlb_ref[...] * (jnp.where(z >= 0.0, e, 1.0) / (1.0 + e))
    sg = g / (1.0 + jnp.exp(-g))

    for hh in range(heads_per_step):
        cs = slice(hh * LANES, (hh + 1) * LANES)
        q_ref[hh] = q[:, cs].astype(BF16)
        k_ref[hh] = key[:, cs].astype(BF16)
        lf_ref[hh] = lf[:, cs]
        v_ref[hh] = v[:, cs].astype(BF16)
        sg_ref[hh] = sg[:, cs].astype(BF16)


def _hg_in(x, gain, mod, lb, w_in, bsz, seq):
    t, d = x.shape
    n_heads = d // LANES
    tm = _tile(seq, 1024)
    tn = _tile(d, 2 * LANES)
    hps = tn // LANES
    nj = d // tn
    per_b = seq // tm
    mod_spec = lambda k: pl.BlockSpec((None, 1, d), lambda i, j: (i // per_b, 0, k))
    vec_spec = pl.BlockSpec((1, tn), lambda i, j: (0, j))
    w_spec = lambda grp: pl.BlockSpec((d, tn), lambda i, j: (0, grp * nj + j))
    out_spec = pl.BlockSpec((None, hps, tm, LANES), lambda i, j: (i // per_b, j, i % per_b, 0))
    hm = lambda dt: jax.ShapeDtypeStruct((bsz, n_heads, seq, LANES), dt)
    lbf = lb.reshape(1, d)
    kern = functools.partial(_hg_in_kernel, heads_per_step=hps, q_scale=float(LANES) ** -0.5)
    return pl.pallas_call(
        kern,
        grid=(t // tm, nj),
        in_specs=[
            pl.BlockSpec((tm, d), lambda i, j: (i, 0)),
            pl.BlockSpec((1, d), lambda i, j: (0, 0)),
            mod_spec(0), mod_spec(1),
            vec_spec, vec_spec, vec_spec,
            w_spec(0), w_spec(1), w_spec(2), w_spec(3),
        ],
        out_specs=[out_spec] * 5,
        out_shape=[hm(BF16), hm(BF16), hm(F32), hm(BF16), hm(BF16)],
        scratch_shapes=[pltpu.VMEM((tm, d), BF16)],
        compiler_params=_params("parallel", "arbitrary"),
        name="hg_in",
    )(x, gain.reshape(1, d), mod, mod, jnp.log(lbf), jnp.log1p(-lbf), 1.0 - lbf, w_in, w_in, w_in, w_in)


def _nt_dot(a, b):
    return lax.dot_general(a, b, (((1,), (1,)), ((), ())), preferred_element_type=F32)


def _tn_dot(a, b):
    return lax.dot_general(a, b, (((0,), (0,)), ((), ())), preferred_element_type=F32)


def _chunk_scores(q, k, b, lf):
    row = lax.broadcasted_iota(jnp.int32, (CHUNK, LANES), 0)
    ti = lax.broadcasted_iota(jnp.int32, (CHUNK, CHUNK), 0)
    si = lax.broadcasted_iota(jnp.int32, (CHUNK, CHUNK), 1)
    scores = jnp.where(ti == si, _nt_dot(q.astype(BF16), k.astype(BF16)), 0.0)
    start = b - lf
    c = 1
    while c < CHUNK:
        later = (row & c) != 0
        bound = jnp.where(later, start, pltpu.roll(start, CHUNK - c, axis=0))
        e = jnp.exp(jnp.where(later, b - bound, bound - b))
        m = jnp.where(later, q, k) * e
        qs = jnp.where(later, m, 0.0).astype(BF16)
        ks = jnp.where(later, 0.0, m).astype(BF16)
        same_parent = (ti & -(2 * c)) == (si & -(2 * c))
        scores = scores + jnp.where(same_parent, _nt_dot(qs, ks), 0.0)
        if 2 * c < CHUNK:
            start = jnp.where(later, pltpu.roll(start, c, axis=0), start)
        c *= 2
    return scores


def _cumsum_rows(lf, tri):
    hi = lf.astype(BF16)
    r1 = lf - hi.astype(F32)
    mid = r1.astype(BF16)
    lo = (r1 - mid.astype(F32)).astype(BF16)
    dot = lambda part: jnp.dot(tri, part, preferred_element_type=F32)
    return dot(hi) + dot(mid) + dot(lo)


def _rec_kernel(q_ref, k_ref, lf_ref, v_ref, sg_ref, og_ref, o_ref, st_ref, *, n_chunks):
    @pl.when(pl.program_id(2) == 0)
    def _():
        st_ref[...] = jnp.zeros_like(st_ref)

    ti = lax.broadcasted_iota(jnp.int32, (CHUNK, CHUNK), 0)
    si = lax.broadcasted_iota(jnp.int32, (CHUNK, CHUNK), 1)
    causal = si <= ti
    tri = jnp.where(causal, 1.0, 0.0).astype(BF16)
    og = og_ref[...]

    def rows(ci):
        return slice(ci * CHUNK, (ci + 1) * CHUNK)

    mean_w = jnp.full((LANES, LANES), 1.0 / LANES, BF16)

    def finish(outs):
        ms = [jnp.dot((o * o).astype(BF16), mean_w, preferred_element_type=F32) for o in outs]
        for ci, o in enumerate(outs):
            o = o * lax.rsqrt(ms[ci] + EPS) * og
            o_ref[rows(ci), :] = (o * sg_ref[rows(ci), :].astype(F32)).astype(BF16)

    cs = range(n_chunks)
    b = [_cumsum_rows(lf_ref[rows(ci), :], tri) for ci in cs]
    total = b[0][CHUNK - 1:CHUNK, :]
    for ci in range(1, n_chunks):
        total = jnp.minimum(total, b[ci][CHUNK - 1:CHUNK, :])
    mild = jnp.min(total) >= -MAX_FACTORED_DECAY
    q_dec = [(q_ref[rows(ci), :].astype(F32) * jnp.exp(b[ci])).astype(BF16) for ci in cs]
    decay = [jnp.exp(b[ci][CHUNK - 1:CHUNK, :]) for ci in cs]

    @pl.when(mild)
    def _():
        k_grow = [k_ref[rows(ci), :].astype(F32) * jnp.exp(-b[ci]) for ci in cs]
        scores = [jnp.where(causal, _nt_dot(q_dec[ci], k_grow[ci].astype(BF16)), 0.0).astype(BF16) for ci in cs]
        intra = [jnp.dot(scores[ci], v_ref[rows(ci), :], preferred_element_type=F32) for ci in cs]
        update = [_tn_dot(v_ref[rows(ci), :], (k_grow[ci] * decay[ci]).astype(BF16)) for ci in cs]
        states = [st_ref[...]]
        for ci in cs:
            states.append(states[ci] * decay[ci] + update[ci])
        finish([intra[ci] + _nt_dot(q_dec[ci], states[ci].astype(BF16)) for ci in cs])
        st_ref[...] = states[n_chunks]

    @pl.when(jnp.logical_not(mild))
    def _():
        state_t = st_ref[...]
        outs = []
        for ci in range(n_chunks):
            lf = lf_ref[rows(ci), :]
            q = q_ref[rows(ci), :].astype(F32)
            k = k_ref[rows(ci), :].astype(F32)
            v = v_ref[rows(ci), :]
            scores = _chunk_scores(q, k, b[ci], lf)
            o = jnp.dot(scores.astype(BF16), v, preferred_element_type=F32)
            o = o + _nt_dot(q_dec[ci], state_t.astype(BF16))
            k_dec = (k * jnp.exp(b[ci][CHUNK - 1:CHUNK, :] - b[ci])).astype(BF16)
            state_t = state_t * decay[ci] + _tn_dot(v, k_dec)
            outs.append(o)
        finish(outs)
        st_ref[...] = state_t


def _hg_rec(q, k, lf, v, sg, o_gain):
    bsz, n_heads, seq, _ = q.shape
    ts = _tile(seq, 1024)
    in_spec = pl.BlockSpec((None, None, ts, LANES), lambda b, h, s: (b, h, s, 0))
    return pl.pallas_call(
        functools.partial(_rec_kernel, n_chunks=ts // CHUNK),
        grid=(bsz, n_heads, seq // ts),
        in_specs=[in_spec] * 5 + [pl.BlockSpec((1, LANES), lambda b, h, s: (0, 0))],
        out_specs=pl.BlockSpec((None, ts, LANES), lambda b, h, s: (b, s, h)),
        out_shape=jax.ShapeDtypeStruct((bsz, seq, n_heads * LANES), BF16),
        scratch_shapes=[pltpu.VMEM((LANES, LANES), F32)],
        compiler_params=_params("parallel", "parallel", "arbitrary"),
        name="hg_rec",
    )(q, k, lf, v, sg, o_gain.reshape(1, LANES))


def _pair_sum_weights():
    row = lax.broadcasted_iota(jnp.int32, (QK_PAD, QK_PAD), 0)
    col = lax.broadcasted_iota(jnp.int32, (QK_PAD, QK_PAD), 1)
    return jnp.where((row < LANES) == (col < LANES), 1.0, 0.0).astype(BF16)


def _rope_pair(u):
    return u + pltpu.roll(u, ROPE, axis=1)


def _mla_kv_kernel(x_ref, gain_ref, shift_ref, scale_ref, wd_ref, lat_gain_ref, wu_ref,
                   gn_ref, gr_ref, cs_ref, k_ref, v_ref, *, n_heads, kv_lora):
    h = _norm_mod(x_ref[...], gain_ref[...], scale_ref[...], shift_ref[...]).astype(BF16)
    ckv = jnp.dot(h, wd_ref[...], preferred_element_type=F32)
    lat = ckv[:, :kv_lora]
    pe2 = ckv[:, kv_lora:]
    c_lat = (lat * lax.rsqrt(jnp.mean(lat * lat, axis=-1, keepdims=True) + EPS) * lat_gain_ref[...]).astype(BF16)
    pair_w = _pair_sum_weights()
    sq_pe = 0.5 * (pe2 * pe2)
    rot = _rope_pair(pe2 * gr_ref[...] * cs_ref[...])
    gn = gn_ref[...]

    def pair_kv(p):
        return [jnp.dot(c_lat, wu_ref[:, hh * QK_PAD:(hh + 1) * QK_PAD], preferred_element_type=F32)
                for hh in (2 * p, 2 * p + 1)]

    nxt = pair_kv(0)
    for p in range(n_heads // 2):
        kvs = nxt
        if p + 1 < n_heads // 2:
            nxt = pair_kv(p + 1)
        sq = jnp.concatenate([kv[:, :LANES] * kv[:, :LANES] + sq_pe for kv in kvs], axis=1)
        r2 = lax.rsqrt(jnp.dot(sq.astype(BF16), pair_w, preferred_element_type=F32) * (1.0 / QK) + EPS)
        for idx, kv in enumerate(kvs):
            hh = 2 * p + idx
            r = r2[:, idx * LANES:(idx + 1) * LANES]
            k_ref[hh, :, :LANES] = (kv[:, :LANES] * r * gn).astype(BF16)
            k_ref[hh, :, LANES:] = (rot * r).astype(BF16)
            v_ref[hh] = kv[:, LANES:].T.astype(BF16)


def _mla_kv(x, gain, mod, wd, lat_gain, wu, gn, gr, cs, bsz, seq):
    t, d = x.shape
    n_heads = d // LANES
    kv_lora = lat_gain.shape[0]
    tm = _tile(seq, ATTN_TILE)
    per_b = seq // tm
    mod_spec = lambda k: pl.BlockSpec((None, 1, d), lambda i: (i // per_b, 0, k))
    assert n_heads % 2 == 0, n_heads
    full = lambda a: pl.BlockSpec(a.shape, lambda i: (0,) * a.ndim)
    lat_gain = lat_gain.reshape(1, kv_lora)
    return pl.pallas_call(
        functools.partial(_mla_kv_kernel, n_heads=n_heads, kv_lora=kv_lora),
        grid=(t // tm,),
        in_specs=[
            pl.BlockSpec((tm, d), lambda i: (i, 0)),
            pl.BlockSpec((1, d), lambda i: (0, 0)),
            mod_spec(0), mod_spec(1),
            full(wd), full(lat_gain), full(wu), full(gn), full(gr),
            pl.BlockSpec((tm, LANES), lambda i: (i, 0)),
        ],
        out_specs=[
            pl.BlockSpec((None, n_heads, tm, QK_PAD), lambda i: (i // per_b, 0, i % per_b, 0)),
            pl.BlockSpec((None, n_heads, None, LANES, tm), lambda i: (i // per_b, 0, i % per_b, 0, 0)),
        ],
        out_shape=[jax.ShapeDtypeStruct((bsz, n_heads, seq, QK_PAD), BF16),
                   jax.ShapeDtypeStruct((bsz, n_heads, seq // tm, LANES, tm), BF16)],
        compiler_params=_params("parallel"),
        name="mla_kv",
    )(x, gain.reshape(1, d), mod, mod, wd, lat_gain, wu, gn, gr, cs)


def _mla_q_kernel(x_ref, gain_ref, shift_ref, scale_ref, wd_ref, lat_gain_ref, wu_ref,
                  gn_ref, gr_ref, cs_ref, q_ref, *, n_heads):
    h = _norm_mod(x_ref[...], gain_ref[...], scale_ref[...], shift_ref[...]).astype(BF16)
    cq = jnp.dot(h, wd_ref[...], preferred_element_type=F32)
    cq = (cq * lax.rsqrt(jnp.mean(cq * cq, axis=-1, keepdims=True) + EPS) * lat_gain_ref[...]).astype(BF16)
    gn = gn_ref[...]
    rope_tab = gr_ref[...] * cs_ref[...]
    pair_w = _pair_sum_weights()

    def pair_q(p):
        return [jnp.dot(cq, wu_ref[:, hh * QK_PAD:(hh + 1) * QK_PAD], preferred_element_type=F32)
                for hh in (2 * p, 2 * p + 1)]

    nxt = pair_q(0)
    for p in range(n_heads // 2):
        qhs = nxt
        if p + 1 < n_heads // 2:
            nxt = pair_q(p + 1)
        sq = jnp.concatenate([qh[:, :LANES] * qh[:, :LANES] + 0.5 * (qh[:, LANES:] * qh[:, LANES:])
                              for qh in qhs], axis=1)
        r2 = lax.rsqrt(jnp.dot(sq.astype(BF16), pair_w, preferred_element_type=F32) * (1.0 / QK) + EPS)
        for idx, qh in enumerate(qhs):
            hh = 2 * p + idx
            r = r2[:, idx * LANES:(idx + 1) * LANES]
            q_ref[hh, :, :LANES] = (qh[:, :LANES] * r * gn).astype(BF16)
            q_ref[hh, :, LANES:] = (qh[:, LANES:] * rope_tab * r).astype(BF16)


def _mla_q(x, gain, mod, wd, lat_gain, wu, gn, gr, cs, bsz, seq):
    t, d = x.shape
    n_heads = d // LANES
    q_lora = lat_gain.shape[0]
    tm = _tile(seq, 512)
    per_b = seq // tm
    mod_spec = lambda k: pl.BlockSpec((None, 1, d), lambda i: (i // per_b, 0, k))
    assert n_heads % 2 == 0, n_heads
    full = lambda a: pl.BlockSpec(a.shape, lambda i: (0,) * a.ndim)
    lat_gain = lat_gain.reshape(1, q_lora)
    return pl.pallas_call(
        functools.partial(_mla_q_kernel, n_heads=n_heads),
        grid=(t // tm,),
        in_specs=[
            pl.BlockSpec((tm, d), lambda i: (i, 0)),
            pl.BlockSpec((1, d), lambda i: (0, 0)),
            mod_spec(0), mod_spec(1),
            full(wd), full(lat_gain), full(wu), full(gn), full(gr),
            pl.BlockSpec((tm, LANES), lambda i: (i, 0)),
        ],
        out_specs=pl.BlockSpec((None, n_heads, tm, QK_PAD), lambda i: (i // per_b, 0, i % per_b, 0)),
        out_shape=jax.ShapeDtypeStruct((bsz, n_heads, seq, QK_PAD), BF16),
        compiler_params=_params("parallel"),
        name="mla_q",
    )(x, gain.reshape(1, d), mod, mod, wd, lat_gain, wu, gn, gr, cs)


def _flash_kernel(q_ref, k_ref, vt_ref, o_ref, s_ref, m_ref, l_ref, acc_ref, *, tq, heads_per_step):
    i = pl.program_id(2)
    qs = [q_ref[hh] for hh in range(heads_per_step)]
    m_ref[...] = jnp.full(m_ref.shape, -jnp.inf, F32)
    l_ref[...] = jnp.zeros_like(l_ref)
    acc_ref[...] = jnp.zeros_like(acc_ref)

    def key_rows(j):
        return pl.ds(pl.multiple_of(j * tq, tq), tq)

    def scores(hh, j):
        return _nt_dot(k_ref[hh, key_rows(j), :], qs[hh])

    def consume(hh, j, masked):
        s = s_ref[hh]
        if masked:
            ki = lax.broadcasted_iota(jnp.int32, (tq, tq), 0)
            qi = lax.broadcasted_iota(jnp.int32, (tq, tq), 1)
            s = jnp.where(ki <= qi, s, -jnp.inf)
        m = m_ref[hh]
        m_new = jnp.maximum(m, jnp.max(s, axis=0, keepdims=True))
        alpha = jnp.exp2(m - m_new)
        p = jnp.exp2(s - m_new)
        m_ref[hh] = m_new
        l_ref[hh] = alpha * l_ref[hh] + jnp.sum(p, axis=0, keepdims=True)
        pv = jnp.dot(vt_ref[hh, j], p.astype(BF16), preferred_element_type=F32)
        acc_ref[hh] = alpha * acc_ref[hh] + pv

    for hh in range(heads_per_step):
        s_ref[hh] = scores(hh, 0)

    def body(j, carry):
        for hh in range(heads_per_step):
            s_next = scores(hh, j + 1)
            consume(hh, j, False)
            s_ref[hh] = s_next
        return carry

    lax.fori_loop(0, i, body, 0)
    for hh in range(heads_per_step):
        consume(hh, i, True)
        o_ref[:, hh * LANES:(hh + 1) * LANES] = (acc_ref[hh] / l_ref[hh]).T.astype(BF16)


def _flash(q, k, vt):
    bsz, n_heads, seq, _ = q.shape
    tq = vt.shape[-1]
    hps = 4 if n_heads % 4 == 0 else (2 if n_heads % 2 == 0 else 1)
    return pl.pallas_call(
        functools.partial(_flash_kernel, tq=tq, heads_per_step=hps),
        grid=(bsz, n_heads // hps, seq // tq),
        in_specs=[
            pl.BlockSpec((None, hps, tq, QK_PAD), lambda b, h, i: (b, h, i, 0)),
            pl.BlockSpec((None, hps, seq, QK_PAD), lambda b, h, i: (b, h, 0, 0)),
            pl.BlockSpec((None, hps, seq // tq, LANES, tq), lambda b, h, i: (b, h, 0, 0, 0)),
        ],
        out_specs=pl.BlockSpec((None, tq, hps * LANES), lambda b, h, i: (b, i, h)),
        out_shape=jax.ShapeDtypeStruct((bsz, seq, n_heads * LANES), BF16),
        scratch_shapes=[pltpu.VMEM((hps, tq, tq), F32),
                        pltpu.VMEM((hps, 1, tq), F32), pltpu.VMEM((hps, 1, tq), F32),
                        pltpu.VMEM((hps, LANES, tq), F32)],
        compiler_params=_params("parallel", "parallel", "arbitrary"),
        name="flash",
    )(q, k, vt)


def _swap_halves(a):
    half = a.shape[-1] // 2
    return jnp.concatenate([a[..., half:], a[..., :half]], axis=-1)


def _rope_gain(g):
    return jnp.concatenate([g, _swap_halves(g)]).reshape(1, LANES)


def kernel(x, c, positions, ada_w, ada_b, norm_mix, norm_mlp, mlp_w1, mlp_w2, hg_w_in, hg_lower, hg_o_norm, hg_w_out, kv_ada_w, kv_ada_b, kv_in_norm, mla_w_dkv, mla_kv_norm, mla_w_ukv, mla_k_norm, mla_w_dq, mla_q_lat_norm, mla_w_uq, mla_q_norm, mla_w_o):
    bsz, seq, d = x.shape
    depth = ada_w.shape[0]
    n_a = hg_w_in.shape[0]
    n_heads = d // LANES
    kv_lora = mla_kv_norm.shape[0]
    q_lora = mla_w_dq.shape[2]
    t = bsz * seq

    mod = _ada(c, ada_w, ada_b).reshape(depth, bsz, 1, 6 * d)
    kv_mod = _ada(c, kv_ada_w[None], kv_ada_b[None]).reshape(bsz, 1, 2 * d)

    inv_freq = 1.0 / (ROPE_THETA ** (jnp.arange(0, ROPE, 2, dtype=F32) / ROPE))
    ang = positions.astype(F32)[..., None] * inv_freq
    cos, sin = jnp.cos(ang), jnp.sin(ang)
    cs = jnp.concatenate([cos, cos, -sin, sin], axis=-1).reshape(t, LANES)

    lb_all = jnp.cumsum(jax.nn.softmax(hg_lower.astype(F32), axis=0), axis=0)[:n_a]

    xf = x.reshape(t, d)
    k_sh = v_sh = None
    for layer in range(depth):
        if layer == n_a:
            wd = jnp.concatenate([mla_w_dkv, _swap_halves(mla_w_dkv[:, kv_lora:])], axis=1).astype(BF16)
            k_sh, v_sh = _mla_kv(xf, kv_in_norm, kv_mod, wd, mla_kv_norm, mla_w_ukv.astype(BF16),
                                 mla_k_norm[:LANES].reshape(1, LANES), _rope_gain(mla_k_norm[LANES:]),
                                 cs, bsz, seq)
        if layer < n_a:
            q, k, lf, v, sg = _hg_in(xf, norm_mix[layer], mod[layer], lb_all[layer],
                                     hg_w_in[layer].astype(BF16), bsz, seq)
            y = _hg_rec(q, k, lf, v, sg, hg_o_norm[layer]).reshape(t, d)
            xf = _proj_res(y, hg_w_out[layer].astype(BF16), xf, mod[layer], 2, seq)
        else:
            j = layer - n_a
            wu = mla_w_uq[j].reshape(q_lora, n_heads, QK)
            wu = jnp.concatenate([wu, _swap_halves(wu[..., LANES:])], axis=-1)
            wu = wu.reshape(q_lora, n_heads * QK_PAD).astype(BF16)
            sm_scale = float(QK) ** -0.5 * LOG2E
            qg = mla_q_norm[j] * sm_scale
            qh = _mla_q(xf, norm_mix[layer], mod[layer], mla_w_dq[j].astype(BF16), mla_q_lat_norm[j], wu,
                        qg[:LANES].reshape(1, LANES), _rope_gain(qg[LANES:]), cs, bsz, seq)
            y = _flash(qh, k_sh, v_sh).reshape(t, d)
            xf = _proj_res(y, mla_w_o[j].astype(BF16), xf, mod[layer], 2, seq)
        xf = _mlp(xf, norm_mlp[layer], mod[layer], mlp_w1[layer].astype(BF16), mlp_w2[layer].astype(BF16), seq)
    return xf.reshape(bsz, seq, d)
```

```python
import functools

import jax
import jax.numpy as jnp
from jax import lax
from jax.experimental import pallas as pl
from jax.experimental.pallas import tpu as pltpu

F32 = jnp.float32
BF16 = jnp.bfloat16

EPS = 1e-6
LANES = 128
ROPE = 64
QK = LANES + ROPE
QK_PAD = 2 * LANES
CHUNK = 64
ATTN_TILE = 512
MAX_FACTORED_DECAY = 72.0
ROPE_THETA = 10000.0
LOG2E = 1.4426950408889634
VMEM_LIMIT = 56 * 1024 * 1024


def _params(*sem):
    return pltpu.CompilerParams(dimension_semantics=sem, vmem_limit_bytes=VMEM_LIMIT)


def _tile(n, pref):
    t = min(n, pref)
    while n % t:
        t -= LANES
    assert t > 0 and n % t == 0, (n, pref)
    return t


def _norm_mod(x, gain, scale, shift):
    r = lax.rsqrt(jnp.mean(x * x, axis=-1, keepdims=True) + EPS)
    return (x * r) * (gain * (1.0 + scale)) + shift


def _ada_kernel(c_ref, w_ref, b_ref, o_ref):
    c = c_ref[...]
    cond = (c / (1.0 + jnp.exp(-c))).astype(BF16)
    o_ref[...] = jnp.dot(cond, w_ref[...].astype(BF16), preferred_element_type=F32) + b_ref[...]


def _ada(c, w, b):
    n_layers, d, n = w.shape
    bsz = c.shape[0]
    tn = _tile(n, 1024)
    return pl.pallas_call(
        _ada_kernel,
        grid=(n_layers, n // tn),
        in_specs=[
            pl.BlockSpec((bsz, d), lambda l, j: (0, 0)),
            pl.BlockSpec((None, d, tn), lambda l, j: (l, 0, j)),
            pl.BlockSpec((None, 1, tn), lambda l, j: (l, 0, j)),
        ],
        out_specs=pl.BlockSpec((None, bsz, tn), lambda l, j: (l, 0, j)),
        out_shape=jax.ShapeDtypeStruct((n_layers, bsz, n), F32),
        compiler_params=_params("parallel", "parallel"),
        name="ada",
    )(c, w, b.reshape(n_layers, 1, n))


def _mlp_kernel(x_ref, gain_ref, shift_ref, scale_ref, gate_ref, w1_ref, w2_ref, o_ref, h_ref):
    j = pl.program_id(1)

    def gated_chunk():
        a = jnp.maximum(jnp.dot(h_ref[...], w1_ref[...], preferred_element_type=F32), 0.0)
        return gate_ref[...] * jnp.dot((a * a).astype(BF16), w2_ref[...], preferred_element_type=F32)

    @pl.when(j == 0)
    def _():
        h_ref[...] = _norm_mod(x_ref[...], gain_ref[...], scale_ref[...], shift_ref[...]).astype(BF16)
        o_ref[...] = x_ref[...] + gated_chunk()

    @pl.when(j > 0)
    def _():
        o_ref[...] += gated_chunk()


def _mlp(x, gain, mod, w1, w2, seq):
    t, d = x.shape
    f = w1.shape[1]
    tm = _tile(seq, 1024)
    tf = _tile(f, 512)
    per_b = seq // tm
    mod_spec = lambda k: pl.BlockSpec((None, 1, d), lambda i, j: (i // per_b, 0, k))
    return pl.pallas_call(
        _mlp_kernel,
        grid=(t // tm, f // tf),
        in_specs=[
            pl.BlockSpec((tm, d), lambda i, j: (i, 0)),
            pl.BlockSpec((1, d), lambda i, j: (0, 0)),
            mod_spec(3), mod_spec(4), mod_spec(5),
            pl.BlockSpec((d, tf), lambda i, j: (0, j)),
            pl.BlockSpec((tf, d), lambda i, j: (j, 0)),
        ],
        out_specs=pl.BlockSpec((tm, d), lambda i, j: (i, 0)),
        out_shape=jax.ShapeDtypeStruct((t, d), F32),
        scratch_shapes=[pltpu.VMEM((tm, d), BF16)],
        compiler_params=_params("parallel", "arbitrary"),
        name="mlp",
    )(x, gain.reshape(1, d), mod, mod, mod, w1, w2)


def _proj_res_kernel(y_ref, w_ref, x_ref, gate_ref, o_ref):
    o_ref[...] = x_ref[...] + gate_ref[...] * jnp.dot(y_ref[...], w_ref[...], preferred_element_type=F32)


def _proj_res(y, w, x, mod, gate_idx, seq):
    t, d = x.shape
    k = y.shape[1]
    tm = _tile(seq, 512)
    tn = _tile(d, 2048)
    per_b = seq // tm
    return pl.pallas_call(
        _proj_res_kernel,
        grid=(t // tm, d // tn),
        in_specs=[
            pl.BlockSpec((tm, k), lambda i, j: (i, 0)),
            pl.BlockSpec((k, tn), lambda i, j: (0, j)),
            pl.BlockSpec((tm, tn), lambda i, j: (i, j)),
            pl.BlockSpec((None, 1, tn), lambda i, j: (i // per_b, 0, gate_idx * (d // tn) + j)),
        ],
        out_specs=pl.BlockSpec((tm, tn), lambda i, j: (i, j)),
        out_shape=jax.ShapeDtypeStruct((t, d), F32),
        compiler_params=_params("parallel", "parallel"),
        name="proj_res",
    )(y, w, x, mod)


def _hg_in_kernel(x_ref, gain_ref, shift_ref, scale_ref, llb_ref, l1mlb_ref, omlb_ref,
                  wq_ref, wf_ref, wi_ref, wg_ref,
                  q_ref, k_ref, lf_ref, v_ref, sg_ref, h_ref, *, heads_per_step, q_scale):
    def project():
        h = h_ref[...]
        z = jnp.dot(h, wf_ref[...], preferred_element_type=F32)
        q = jnp.dot(h, wq_ref[...], preferred_element_type=F32) * q_scale
        v = jnp.dot(h, wi_ref[...], preferred_element_type=F32)
        g = jnp.dot(h, wg_ref[...], preferred_element_type=F32)

        e = jnp.exp(-jnp.abs(z))
        log_sig = jnp.minimum(z, 0.0) - jnp.log(1.0 + e)
        a = llb_ref[...]
        c = l1mlb_ref[...] + log_sig
        lf = jnp.maximum(a, c) + jnp.log(1.0 + jnp.exp(-jnp.abs(a - c)))
        key = omlb_ref[...] * (jnp.where(z >= 0.0, e, 1.0) / (1.0 + e))
        sg = g / (1.0 + jnp.exp(-g))

        for hh in range(heads_per_step):
            cs = slice(hh * LANES, (hh + 1) * LANES)
            q_ref[hh] = q[:, cs].astype(BF16)
            k_ref[hh] = key[:, cs].astype(BF16)
            lf_ref[hh] = lf[:, cs]
            v_ref[hh] = v[:, cs].astype(BF16)
            sg_ref[hh] = sg[:, cs].astype(BF16)

    @pl.when(pl.program_id(1) == 0)
    def _():
        h_ref[...] = _norm_mod(x_ref[...], gain_ref[...], scale_ref[...], shift_ref[...]).astype(BF16)
        project()

    @pl.when(pl.program_id(1) > 0)
    def _():
        project()


def _hg_in(x, gain, mod, lb, w_in, bsz, seq):
    t, d = x.shape
    n_heads = d // LANES
    tm = _tile(seq, 1024)
    tn = _tile(d, 2 * LANES)
    hps = tn // LANES
    nj = d // tn
    per_b = seq // tm
    mod_spec = lambda k: pl.BlockSpec((None, 1, d), lambda i, j: (i // per_b, 0, k))
    vec_spec = pl.BlockSpec((1, tn), lambda i, j: (0, j))
    w_spec = lambda grp: pl.BlockSpec((d, tn), lambda i, j: (0, grp * nj + j))
    out_spec = pl.BlockSpec((None, hps, tm, LANES), lambda i, j: (i // per_b, j, i % per_b, 0))
    hm = lambda dt: jax.ShapeDtypeStruct((bsz, n_heads, seq, LANES), dt)
    lbf = lb.reshape(1, d)
    kern = functools.partial(_hg_in_kernel, heads_per_step=hps, q_scale=float(LANES) ** -0.5)
    return pl.pallas_call(
        kern,
        grid=(t // tm, nj),
        in_specs=[
            pl.BlockSpec((tm, d), lambda i, j: (i, 0)),
            pl.BlockSpec((1, d), lambda i, j: (0, 0)),
            mod_spec(0), mod_spec(1),
            vec_spec, vec_spec, vec_spec,
            w_spec(0), w_spec(1), w_spec(2), w_spec(3),
        ],
        out_specs=[out_spec] * 5,
        out_shape=[hm(BF16), hm(BF16), hm(F32), hm(BF16), hm(BF16)],
        scratch_shapes=[pltpu.VMEM((tm, d), BF16)],
        compiler_params=_params("parallel", "arbitrary"),
        name="hg_in",
    )(x, gain.reshape(1, d), mod, mod, jnp.log(lbf), jnp.log1p(-lbf), 1.0 - lbf, w_in, w_in, w_in, w_in)


def _nt_dot(a, b):
    return lax.dot_general(a, b, (((1,), (1,)), ((), ())), preferred_element_type=F32)


def _tn_dot(a, b):
    return lax.dot_general(a, b, (((0,), (0,)), ((), ())), preferred_element_type=F32)


def _chunk_scores(q, k, b, lf):
    row = lax.broadcasted_iota(jnp.int32, (CHUNK, LANES), 0)
    ti = lax.broadcasted_iota(jnp.int32, (CHUNK, CHUNK), 0)
    si = lax.broadcasted_iota(jnp.int32, (CHUNK, CHUNK), 1)
    scores = jnp.where(ti == si, _nt_dot(q.astype(BF16), k.astype(BF16)), 0.0)
    start = b - lf
    c = 1
    while c < CHUNK:
        later = (row & c) != 0
        bound = jnp.where(later, start, pltpu.roll(start, CHUNK - c, axis=0))
        e = jnp.exp(jnp.where(later, b - bound, bound - b))
        m = jnp.where(later, q, k) * e
        qs = jnp.where(later, m, 0.0).astype(BF16)
        ks = jnp.where(later, 0.0, m).astype(BF16)
        same_parent = (ti & -(2 * c)) == (si & -(2 * c))
        scores = scores + jnp.where(same_parent, _nt_dot(qs, ks), 0.0)
        if 2 * c < CHUNK:
            start = jnp.where(later, pltpu.roll(start, c, axis=0), start)
        c *= 2
    return scores


def _cumsum_rows(lf, tri):
    hi = lf.astype(BF16)
    r1 = lf - hi.astype(F32)
    mid = r1.astype(BF16)
    lo = (r1 - mid.astype(F32)).astype(BF16)
    dot = lambda part: jnp.dot(tri, part, preferred_element_type=F32)
    return dot(hi) + dot(mid) + dot(lo)


def _rec_kernel(q_ref, k_ref, lf_ref, v_ref, sg_ref, og_ref, o_ref, st_ref, *, n_chunks):
    @pl.when(pl.program_id(2) == 0)
    def _():
        st_ref[...] = jnp.zeros_like(st_ref)

    ti = lax.broadcasted_iota(jnp.int32, (CHUNK, CHUNK), 0)
    si = lax.broadcasted_iota(jnp.int32, (CHUNK, CHUNK), 1)
    causal = si <= ti
    tri = jnp.where(causal, 1.0, 0.0).astype(BF16)
    og = og_ref[...]

    def rows(ci):
        return slice(ci * CHUNK, (ci + 1) * CHUNK)

    mean_w = jnp.full((LANES, LANES), 1.0 / LANES, BF16)

    def finish(outs):
        ms = [jnp.dot((o * o).astype(BF16), mean_w, preferred_element_type=F32) for o in outs]
        for ci, o in enumerate(outs):
            o = o * lax.rsqrt(ms[ci] + EPS) * og
            o_ref[rows(ci), :] = (o * sg_ref[rows(ci), :].astype(F32)).astype(BF16)

    cs = range(n_chunks)
    b = [_cumsum_rows(lf_ref[rows(ci), :], tri) for ci in cs]
    total = b[0][CHUNK - 1:CHUNK, :]
    for ci in range(1, n_chunks):
        total = jnp.minimum(total, b[ci][CHUNK - 1:CHUNK, :])
    mild = jnp.min(total) >= -MAX_FACTORED_DECAY
    q_dec = [(q_ref[rows(ci), :].astype(F32) * jnp.exp(b[ci])).astype(BF16) for ci in cs]
    decay = [jnp.exp(b[ci][CHUNK - 1:CHUNK, :]) for ci in cs]
    k_grow = [k_ref[rows(ci), :].astype(F32) * jnp.exp(-b[ci]) for ci in cs]
    scores = [jnp.where(causal, _nt_dot(q_dec[ci], k_grow[ci].astype(BF16)), 0.0).astype(BF16) for ci in cs]
    intra = [jnp.dot(scores[ci], v_ref[rows(ci), :], preferred_element_type=F32) for ci in cs]

    @pl.when(mild)
    def _():
        update = [_tn_dot(v_ref[rows(ci), :], (k_grow[ci] * decay[ci]).astype(BF16)) for ci in cs]
        states = [st_ref[...]]
        for ci in cs:
            states.append(states[ci] * decay[ci] + update[ci])
        finish([intra[ci] + _nt_dot(q_dec[ci], states[ci].astype(BF16)) for ci in cs])
        st_ref[...] = states[n_chunks]

    @pl.when(jnp.logical_not(mild))
    def _():
        state_t = st_ref[...]
        outs = []
        for ci in range(n_chunks):
            lf = lf_ref[rows(ci), :]
            q = q_ref[rows(ci), :].astype(F32)
            k = k_ref[rows(ci), :].astype(F32)
            v = v_ref[rows(ci), :]
            scores = _chunk_scores(q, k, b[ci], lf)
            o = jnp.dot(scores.astype(BF16), v, preferred_element_type=F32)
            o = o + _nt_dot(q_dec[ci], state_t.astype(BF16))
            k_dec = (k * jnp.exp(b[ci][CHUNK - 1:CHUNK, :] - b[ci])).astype(BF16)
            state_t = state_t * decay[ci] + _tn_dot(v, k_dec)
            outs.append(o)
        finish(outs)
        st_ref[...] = state_t


def _hg_rec(q, k, lf, v, sg, o_gain):
    bsz, n_heads, seq, _ = q.shape
    ts = _tile(seq, 1024)
    in_spec = pl.BlockSpec((None, None, ts, LANES), lambda b, h, s: (b, h, s, 0))
    return pl.pallas_call(
        functools.partial(_rec_kernel, n_chunks=ts // CHUNK),
        grid=(bsz, n_heads, seq // ts),
        in_specs=[in_spec] * 5 + [pl.BlockSpec((1, LANES), lambda b, h, s: (0, 0))],
        out_specs=pl.BlockSpec((None, ts, LANES), lambda b, h, s: (b, s, h)),
        out_shape=jax.ShapeDtypeStruct((bsz, seq, n_heads * LANES), BF16),
        scratch_shapes=[pltpu.VMEM((LANES, LANES), F32)],
        compiler_params=_params("parallel", "parallel", "arbitrary"),
        name="hg_rec",
    )(q, k, lf, v, sg, o_gain.reshape(1, LANES))


def _pair_sum_weights():
    row = lax.broadcasted_iota(jnp.int32, (QK_PAD, QK_PAD), 0)
    col = lax.broadcasted_iota(jnp.int32, (QK_PAD, QK_PAD), 1)
    return jnp.where((row < LANES) == (col < LANES), 1.0, 0.0).astype(BF16)


def _rope_pair(u):
    return u + pltpu.roll(u, ROPE, axis=1)


def _mla_kv_kernel(x_ref, gain_ref, shift_ref, scale_ref, wd_ref, lat_gain_ref, wu_ref,
                   gn_ref, gr_ref, cs_ref, k_ref, v_ref, *, n_heads, kv_lora):
    h = _norm_mod(x_ref[...], gain_ref[...], scale_ref[...], shift_ref[...]).astype(BF16)
    ckv = jnp.dot(h, wd_ref[...], preferred_element_type=F32)
    lat = ckv[:, :kv_lora]
    pe2 = ckv[:, kv_lora:]
    c_lat = (lat * lax.rsqrt(jnp.mean(lat * lat, axis=-1, keepdims=True) + EPS) * lat_gain_ref[...]).astype(BF16)
    pair_w = _pair_sum_weights()
    sq_pe = 0.5 * (pe2 * pe2)
    rot = _rope_pair(pe2 * gr_ref[...] * cs_ref[...])
    gn = gn_ref[...]

    def pair_kv(p):
        return [jnp.dot(c_lat, wu_ref[:, hh * QK_PAD:(hh + 1) * QK_PAD], preferred_element_type=F32)
                for hh in (2 * p, 2 * p + 1)]

    nxt = pair_kv(0)
    for p in range(n_heads // 2):
        kvs = nxt
        if p + 1 < n_heads // 2:
            nxt = pair_kv(p + 1)
        sq = jnp.concatenate([kv[:, :LANES] * kv[:, :LANES] + sq_pe for kv in kvs], axis=1)
        r2 = lax.rsqrt(jnp.dot(sq.astype(BF16), pair_w, preferred_element_type=F32) * (1.0 / QK) + EPS)
        for idx, kv in enumerate(kvs):
            hh = 2 * p + idx
            r = r2[:, idx * LANES:(idx + 1) * LANES]
            k_ref[hh, :, :LANES] = (kv[:, :LANES] * r * gn).astype(BF16)
            k_ref[hh, :, LANES:] = (rot * r).astype(BF16)
            v_ref[hh] = kv[:, LANES:].T.astype(BF16)


def _mla_kv(x, gain, mod, wd, lat_gain, wu, gn, gr, cs, bsz, seq):
    t, d = x.shape
    n_heads = d // LANES
    kv_lora = lat_gain.shape[0]
    tm = _tile(seq, ATTN_TILE)
    per_b = seq // tm
    mod_spec = lambda k: pl.BlockSpec((None, 1, d), lambda i: (i // per_b, 0, k))
    assert n_heads % 2 == 0, n_heads
    full = lambda a: pl.BlockSpec(a.shape, lambda i: (0,) * a.ndim)
    lat_gain = lat_gain.reshape(1, kv_lora)
    return pl.pallas_call(
        functools.partial(_mla_kv_kernel, n_heads=n_heads, kv_lora=kv_lora),
        grid=(t // tm,),
        in_specs=[
            pl.BlockSpec((tm, d), lambda i: (i, 0)),
            pl.BlockSpec((1, d), lambda i: (0, 0)),
            mod_spec(0), mod_spec(1),
            full(wd), full(lat_gain), full(wu), full(gn), full(gr),
            pl.BlockSpec((tm, LANES), lambda i: (i, 0)),
        ],
        out_specs=[
            pl.BlockSpec((None, n_heads, tm, QK_PAD), lambda i: (i // per_b, 0, i % per_b, 0)),
            pl.BlockSpec((None, n_heads, None, LANES, tm), lambda i: (i // per_b, 0, i % per_b, 0, 0)),
        ],
        out_shape=[jax.ShapeDtypeStruct((bsz, n_heads, seq, QK_PAD), BF16),
                   jax.ShapeDtypeStruct((bsz, n_heads, seq // tm, LANES, tm), BF16)],
        compiler_params=_params("parallel"),
        name="mla_kv",
    )(x, gain.reshape(1, d), mod, mod, wd, lat_gain, wu, gn, gr, cs)


def _mla_q_kernel(x_ref, gain_ref, shift_ref, scale_ref, wd_ref, lat_gain_ref, wu_ref,
                  gn_ref, gr_ref, cs_ref, q_ref, *, n_heads):
    h = _norm_mod(x_ref[...], gain_ref[...], scale_ref[...], shift_ref[...]).astype(BF16)
    cq = jnp.dot(h, wd_ref[...], preferred_element_type=F32)
    cq = (cq * lax.rsqrt(jnp.mean(cq * cq, axis=-1, keepdims=True) + EPS) * lat_gain_ref[...]).astype(BF16)
    gn = gn_ref[...]
    rope_tab = gr_ref[...] * cs_ref[...]
    pair_w = _pair_sum_weights()

    def pair_q(p):
        return [jnp.dot(cq, wu_ref[:, hh * QK_PAD:(hh + 1) * QK_PAD], preferred_element_type=F32)
                for hh in (2 * p, 2 * p + 1)]

    nxt = pair_q(0)
    for p in range(n_heads // 2):
        qhs = nxt
        if p + 1 < n_heads // 2:
            nxt = pair_q(p + 1)
        sq = jnp.concatenate([qh[:, :LANES] * qh[:, :LANES] + 0.5 * (qh[:, LANES:] * qh[:, LANES:])
                              for qh in qhs], axis=1)
        r2 = lax.rsqrt(jnp.dot(sq.astype(BF16), pair_w, preferred_element_type=F32) * (1.0 / QK) + EPS)
        for idx, qh in enumerate(qhs):
            hh = 2 * p + idx
            r = r2[:, idx * LANES:(idx + 1) * LANES]
            q_ref[hh, :, :LANES] = (qh[:, :LANES] * r * gn).astype(BF16)
            q_ref[hh, :, LANES:] = (qh[:, LANES:] * rope_tab * r).astype(BF16)


def _mla_q(x, gain, mod, wd, lat_gain, wu, gn, gr, cs, bsz, seq):
    t, d = x.shape
    n_heads = d // LANES
    q_lora = lat_gain.shape[0]
    tm = _tile(seq, 512)
    per_b = seq // tm
    mod_spec = lambda k: pl.BlockSpec((None, 1, d), lambda i: (i // per_b, 0, k))
    assert n_heads % 2 == 0, n_heads
    full = lambda a: pl.BlockSpec(a.shape, lambda i: (0,) * a.ndim)
    lat_gain = lat_gain.reshape(1, q_lora)
    return pl.pallas_call(
        functools.partial(_mla_q_kernel, n_heads=n_heads),
        grid=(t // tm,),
        in_specs=[
            pl.BlockSpec((tm, d), lambda i: (i, 0)),
            pl.BlockSpec((1, d), lambda i: (0, 0)),
            mod_spec(0), mod_spec(1),
            full(wd), full(lat_gain), full(wu), full(gn), full(gr),
            pl.BlockSpec((tm, LANES), lambda i: (i, 0)),
        ],
        out_specs=pl.BlockSpec((None, n_heads, tm, QK_PAD), lambda i: (i // per_b, 0, i % per_b, 0)),
        out_shape=jax.ShapeDtypeStruct((bsz, n_heads, seq, QK_PAD), BF16),
        compiler_params=_params("parallel"),
        name="mla_q",
    )(x, gain.reshape(1, d), mod, mod, wd, lat_gain, wu, gn, gr, cs)


def _flash_kernel(q_ref, k_ref, vt_ref, o_ref, s_ref, m_ref, l_ref, acc_ref, *, tq, heads_per_step):
    i = pl.program_id(2)
    qs = [q_ref[hh] for hh in range(heads_per_step)]
    m_ref[...] = jnp.full(m_ref.shape, -jnp.inf, F32)
    l_ref[...] = jnp.zeros_like(l_ref)
    acc_ref[...] = jnp.zeros_like(acc_ref)

    def key_rows(j):
        return pl.ds(pl.multiple_of(j * tq, tq), tq)

    def scores(hh, j):
        return _nt_dot(k_ref[hh, key_rows(j), :], qs[hh])

    def consume(hh, j, masked):
        s = s_ref[hh]
        if masked:
            ki = lax.broadcasted_iota(jnp.int32, (tq, tq), 0)
            qi = lax.broadcasted_iota(jnp.int32, (tq, tq), 1)
            s = jnp.where(ki <= qi, s, -jnp.inf)
        m = m_ref[hh]
        m_new = jnp.maximum(m, jnp.max(s, axis=0, keepdims=True))
        alpha = jnp.exp2(m - m_new)
        p = jnp.exp2(s - m_new)
        m_ref[hh] = m_new
        l_ref[hh] = alpha * l_ref[hh] + jnp.sum(p, axis=0, keepdims=True)
        pv = jnp.dot(vt_ref[hh, j], p.astype(BF16), preferred_element_type=F32)
        acc_ref[hh] = alpha * acc_ref[hh] + pv

    for hh in range(heads_per_step):
        s_ref[hh] = scores(hh, 0)

    def body(j, carry):
        for hh in range(heads_per_step):
            s_next = scores(hh, j + 1)
            consume(hh, j, False)
            s_ref[hh] = s_next
        return carry

    lax.fori_loop(0, i, body, 0)
    for hh in range(heads_per_step):
        consume(hh, i, True)
        o_ref[:, hh * LANES:(hh + 1) * LANES] = (acc_ref[hh] / l_ref[hh]).T.astype(BF16)


def _flash(q, k, vt):
    bsz, n_heads, seq, _ = q.shape
    tq = vt.shape[-1]
    hps = 4 if n_heads % 4 == 0 else (2 if n_heads % 2 == 0 else 1)
    return pl.pallas_call(
        functools.partial(_flash_kernel, tq=tq, heads_per_step=hps),
        grid=(bsz, n_heads // hps, seq // tq),
        in_specs=[
            pl.BlockSpec((None, hps, tq, QK_PAD), lambda b, h, i: (b, h, i, 0)),
            pl.BlockSpec((None, hps, seq, QK_PAD), lambda b, h, i: (b, h, 0, 0)),
            pl.BlockSpec((None, hps, seq // tq, LANES, tq), lambda b, h, i: (b, h, 0, 0, 0)),
        ],
        out_specs=pl.BlockSpec((None, tq, hps * LANES), lambda b, h, i: (b, i, h)),
        out_shape=jax.ShapeDtypeStruct((bsz, seq, n_heads * LANES), BF16),
        scratch_shapes=[pltpu.VMEM((hps, tq, tq), F32),
                        pltpu.VMEM((hps, 1, tq), F32), pltpu.VMEM((hps, 1, tq), F32),
                        pltpu.VMEM((hps, LANES, tq), F32)],
        compiler_params=_params("parallel", "parallel", "arbitrary"),
        name="flash",
    )(q, k, vt)


def _swap_halves(a):
    half = a.shape[-1] // 2
    return jnp.concatenate([a[..., half:], a[..., :half]], axis=-1)


def _rope_gain(g):
    return jnp.concatenate([g, _swap_halves(g)]).reshape(1, LANES)


def kernel(x, c, positions, ada_w, ada_b, norm_mix, norm_mlp, mlp_w1, mlp_w2, hg_w_in, hg_lower, hg_o_norm, hg_w_out, kv_ada_w, kv_ada_b, kv_in_norm, mla_w_dkv, mla_kv_norm, mla_w_ukv, mla_k_norm, mla_w_dq, mla_q_lat_norm, mla_w_uq, mla_q_norm, mla_w_o):
    bsz, seq, d = x.shape
    depth = ada_w.shape[0]
    n_a = hg_w_in.shape[0]
    n_heads = d // LANES
    kv_lora = mla_kv_norm.shape[0]
    q_lora = mla_w_dq.shape[2]
    t = bsz * seq

    mod = _ada(c, ada_w, ada_b).reshape(depth, bsz, 1, 6 * d)
    kv_mod = _ada(c, kv_ada_w[None], kv_ada_b[None]).reshape(bsz, 1, 2 * d)

    inv_freq = 1.0 / (ROPE_THETA ** (jnp.arange(0, ROPE, 2, dtype=F32) / ROPE))
    ang = positions.astype(F32)[..., None] * inv_freq
    cos, sin = jnp.cos(ang), jnp.sin(ang)
    cs = jnp.concatenate([cos, cos, -sin, sin], axis=-1).reshape(t, LANES)

    lb_all = jnp.cumsum(jax.nn.softmax(hg_lower.astype(F32), axis=0), axis=0)[:n_a]

    hg_w_in, hg_w_out, mla_w_dq, mla_w_o, mlp_w1, mlp_w2 = (
        w.astype(BF16) for w in (hg_w_in, hg_w_out, mla_w_dq, mla_w_o, mlp_w1, mlp_w2))

    xf = x.reshape(t, d)
    k_sh = v_sh = None
    for layer in range(depth):
        if layer == n_a:
            wd = jnp.concatenate([mla_w_dkv, _swap_halves(mla_w_dkv[:, kv_lora:])], axis=1).astype(BF16)
            k_sh, v_sh = _mla_kv(xf, kv_in_norm, kv_mod, wd, mla_kv_norm, mla_w_ukv.astype(BF16),
                                 mla_k_norm[:LANES].reshape(1, LANES), _rope_gain(mla_k_norm[LANES:]),
                                 cs, bsz, seq)
        if layer < n_a:
            q, k, lf, v, sg = _hg_in(xf, norm_mix[layer], mod[layer], lb_all[layer],
                                     hg_w_in[layer], bsz, seq)
            y = _hg_rec(q, k, lf, v, sg, hg_o_norm[layer]).reshape(t, d)
            xf = _proj_res(y, hg_w_out[layer], xf, mod[layer], 2, seq)
        else:
            j = layer - n_a
            wu = mla_w_uq[j].reshape(q_lora, n_heads, QK)
            wu = jnp.concatenate([wu, _swap_halves(wu[..., LANES:])], axis=-1)
            wu = wu.reshape(q_lora, n_heads * QK_PAD).astype(BF16)
            sm_scale = float(QK) ** -0.5 * LOG2E
            qg = mla_q_norm[j] * sm_scale
            qh = _mla_q(xf, norm_mix[layer], mod[layer], mla_w_dq[j], mla_q_lat_norm[j], wu,
                        qg[:LANES].reshape(1, LANES), _rope_gain(qg[LANES:]), cs, bsz, seq)
            y = _flash(qh, k_sh, v_sh).reshape(t, d)
            xf = _proj_res(y, mla_w_o[j], xf, mod[layer], 2, seq)
        xf = _mlp(xf, norm_mlp[layer], mod[layer], mlp_w1[layer], mlp_w2[layer], seq)
    return xf.reshape(bsz, seq, d)
```

```python
import functools

import jax
import jax.numpy as jnp
from jax import lax
from jax.experimental import pallas as pl
from jax.experimental.pallas import tpu as pltpu

F32 = jnp.float32
BF16 = jnp.bfloat16

EPS = 1e-6
LANES = 128
ROPE = 64
QK = LANES + ROPE
QK_PAD = 2 * LANES
CHUNK = 64
ATTN_TILE = 512
MAX_FACTORED_DECAY = 72.0
ROPE_THETA = 10000.0
LOG2E = 1.4426950408889634
VMEM_LIMIT = 56 * 1024 * 1024


def _params(*sem):
    return pltpu.CompilerParams(dimension_semantics=sem, vmem_limit_bytes=VMEM_LIMIT)


def _tile(n, pref):
    t = min(n, pref)
    while n % t:
        t -= LANES
    assert t > 0 and n % t == 0, (n, pref)
    return t


def _norm_mod(x, gain, scale, shift):
    r = lax.rsqrt(jnp.mean(x * x, axis=-1, keepdims=True) + EPS)
    return (x * r) * (gain * (1.0 + scale)) + shift


def _ada_kernel(c_ref, w_ref, b_ref, o_ref):
    c = c_ref[...]
    cond = (c / (1.0 + jnp.exp(-c))).astype(BF16)
    o_ref[...] = jnp.dot(cond, w_ref[...].astype(BF16), preferred_element_type=F32) + b_ref[...]


def _ada(c, w, b):
    n_layers, d, n = w.shape
    bsz = c.shape[0]
    tn = _tile(n, 1024)
    return pl.pallas_call(
        _ada_kernel,
        grid=(n_layers, n // tn),
        in_specs=[
            pl.BlockSpec((bsz, d), lambda l, j: (0, 0)),
            pl.BlockSpec((None, d, tn), lambda l, j: (l, 0, j)),
            pl.BlockSpec((None, 1, tn), lambda l, j: (l, 0, j)),
        ],
        out_specs=pl.BlockSpec((None, bsz, tn), lambda l, j: (l, 0, j)),
        out_shape=jax.ShapeDtypeStruct((n_layers, bsz, n), F32),
        compiler_params=_params("parallel", "parallel"),
        name="ada",
    )(c, w, b.reshape(n_layers, 1, n))


def _mlp_kernel(x_ref, gain_ref, shift_ref, scale_ref, gate_ref, w1_ref, w2_ref, o_ref, h_ref):
    j = pl.program_id(1)

    def gated_chunk():
        a = jnp.maximum(jnp.dot(h_ref[...], w1_ref[...], preferred_element_type=F32), 0.0)
        return gate_ref[...] * jnp.dot((a * a).astype(BF16), w2_ref[...], preferred_element_type=F32)

    @pl.when(j == 0)
    def _():
        h_ref[...] = _norm_mod(x_ref[...], gain_ref[...], scale_ref[...], shift_ref[...]).astype(BF16)
        o_ref[...] = x_ref[...] + gated_chunk()

    @pl.when(j > 0)
    def _():
        o_ref[...] += gated_chunk()


def _mlp(x, gain, mod, w1, w2, layer, seq):
    t, d = x.shape
    f = w1.shape[2]
    tm = _tile(seq, 1024)
    tf = _tile(f, 512)
    per_b = seq // tm
    mod_spec = lambda k: pl.BlockSpec((None, 1, d), lambda i, j: (i // per_b, 0, k))
    return pl.pallas_call(
        _mlp_kernel,
        grid=(t // tm, f // tf),
        in_specs=[
            pl.BlockSpec((tm, d), lambda i, j: (i, 0)),
            pl.BlockSpec((1, d), lambda i, j: (0, 0)),
            mod_spec(3), mod_spec(4), mod_spec(5),
            pl.BlockSpec((None, d, tf), lambda i, j: (layer, 0, j)),
            pl.BlockSpec((None, tf, d), lambda i, j: (layer, j, 0)),
        ],
        out_specs=pl.BlockSpec((tm, d), lambda i, j: (i, 0)),
        out_shape=jax.ShapeDtypeStruct((t, d), F32),
        scratch_shapes=[pltpu.VMEM((tm, d), BF16)],
        compiler_params=_params("parallel", "arbitrary"),
        name="mlp",
    )(x, gain.reshape(1, d), mod, mod, mod, w1, w2)


def _proj_res_kernel(y_ref, w_ref, x_ref, gate_ref, o_ref):
    o_ref[...] = x_ref[...] + gate_ref[...] * jnp.dot(y_ref[...], w_ref[...], preferred_element_type=F32)


def _proj_res(y, w, layer, x, mod, gate_idx, seq):
    t, d = x.shape
    k = y.shape[1]
    tm = _tile(seq, 512)
    tn = _tile(d, 2048)
    per_b = seq // tm
    return pl.pallas_call(
        _proj_res_kernel,
        grid=(t // tm, d // tn),
        in_specs=[
            pl.BlockSpec((tm, k), lambda i, j: (i, 0)),
            pl.BlockSpec((None, k, tn), lambda i, j: (layer, 0, j)),
            pl.BlockSpec((tm, tn), lambda i, j: (i, j)),
            pl.BlockSpec((None, 1, tn), lambda i, j: (i // per_b, 0, gate_idx * (d // tn) + j)),
        ],
        out_specs=pl.BlockSpec((tm, tn), lambda i, j: (i, j)),
        out_shape=jax.ShapeDtypeStruct((t, d), F32),
        compiler_params=_params("parallel", "parallel"),
        name="proj_res",
    )(y, w, x, mod)


def _hg_in_kernel(x_ref, gain_ref, shift_ref, scale_ref, llb_ref, l1mlb_ref, omlb_ref,
                  wq_ref, wf_ref, wi_ref, wg_ref,
                  q_ref, k_ref, lf_ref, v_ref, sg_ref, h_ref, *, heads_per_step, q_scale):
    def project():
        h = h_ref[...]
        z = jnp.dot(h, wf_ref[...], preferred_element_type=F32)
        q = jnp.dot(h, wq_ref[...], preferred_element_type=F32) * q_scale
        v = jnp.dot(h, wi_ref[...], preferred_element_type=F32)
        g = jnp.dot(h, wg_ref[...], preferred_element_type=F32)

        e = jnp.exp(-jnp.abs(z))
        log_sig = jnp.minimum(z, 0.0) - jnp.log(1.0 + e)
        a = llb_ref[...]
        c = l1mlb_ref[...] + log_sig
        lf = jnp.maximum(a, c) + jnp.log(1.0 + jnp.exp(-jnp.abs(a - c)))
        key = omlb_ref[...] * (jnp.where(z >= 0.0, e, 1.0) / (1.0 + e))
        sg = g / (1.0 + jnp.exp(-g))

        for hh in range(heads_per_step):
            cs = slice(hh * LANES, (hh + 1) * LANES)
            q_ref[hh] = q[:, cs].astype(BF16)
            k_ref[hh] = key[:, cs].astype(BF16)
            lf_ref[hh] = lf[:, cs]
            v_ref[hh] = v[:, cs].astype(BF16)
            sg_ref[hh] = sg[:, cs].astype(BF16)

    @pl.when(pl.program_id(1) == 0)
    def _():
        h_ref[...] = _norm_mod(x_ref[...], gain_ref[...], scale_ref[...], shift_ref[...]).astype(BF16)
        project()

    @pl.when(pl.program_id(1) > 0)
    def _():
        project()


def _hg_in(x, gain, mod, lb, w_in, layer, bsz, seq):
    t, d = x.shape
    n_heads = d // LANES
    tm = _tile(seq, 1024)
    tn = _tile(d, 2 * LANES)
    hps = tn // LANES
    nj = d // tn
    per_b = seq // tm
    mod_spec = lambda k: pl.BlockSpec((None, 1, d), lambda i, j: (i // per_b, 0, k))
    vec_spec = pl.BlockSpec((1, tn), lambda i, j: (0, j))
    w_spec = lambda grp: pl.BlockSpec((None, d, tn), lambda i, j: (layer, 0, grp * nj + j))
    out_spec = pl.BlockSpec((None, hps, tm, LANES), lambda i, j: (i // per_b, j, i % per_b, 0))
    hm = lambda dt: jax.ShapeDtypeStruct((bsz, n_heads, seq, LANES), dt)
    lbf = lb.reshape(1, d)
    kern = functools.partial(_hg_in_kernel, heads_per_step=hps, q_scale=float(LANES) ** -0.5)
    return pl.pallas_call(
        kern,
        grid=(t // tm, nj),
        in_specs=[
            pl.BlockSpec((tm, d), lambda i, j: (i, 0)),
            pl.BlockSpec((1, d), lambda i, j: (0, 0)),
            mod_spec(0), mod_spec(1),
            vec_spec, vec_spec, vec_spec,
            w_spec(0), w_spec(1), w_spec(2), w_spec(3),
        ],
        out_specs=[out_spec] * 5,
        out_shape=[hm(BF16), hm(BF16), hm(F32), hm(BF16), hm(BF16)],
        scratch_shapes=[pltpu.VMEM((tm, d), BF16)],
        compiler_params=_params("parallel", "arbitrary"),
        name="hg_in",
    )(x, gain.reshape(1, d), mod, mod, jnp.log(lbf), jnp.log1p(-lbf), 1.0 - lbf, w_in, w_in, w_in, w_in)


def _nt_dot(a, b):
    return lax.dot_general(a, b, (((1,), (1,)), ((), ())), preferred_element_type=F32)


def _tn_dot(a, b):
    return lax.dot_general(a, b, (((0,), (0,)), ((), ())), preferred_element_type=F32)


def _chunk_scores(q, k, b, lf):
    row = lax.broadcasted_iota(jnp.int32, (CHUNK, LANES), 0)
    ti = lax.broadcasted_iota(jnp.int32, (CHUNK, CHUNK), 0)
    si = lax.broadcasted_iota(jnp.int32, (CHUNK, CHUNK), 1)
    scores = jnp.where(ti == si, _nt_dot(q.astype(BF16), k.astype(BF16)), 0.0)
    start = b - lf
    c = 1
    while c < CHUNK:
        later = (row & c) != 0
        bound = jnp.where(later, start, pltpu.roll(start, CHUNK - c, axis=0))
        e = jnp.exp(jnp.where(later, b - bound, bound - b))
        m = jnp.where(later, q, k) * e
        qs = jnp.where(later, m, 0.0).astype(BF16)
        ks = jnp.where(later, 0.0, m).astype(BF16)
        same_parent = (ti & -(2 * c)) == (si & -(2 * c))
        scores = scores + jnp.where(same_parent, _nt_dot(qs, ks), 0.0)
        if 2 * c < CHUNK:
            start = jnp.where(later, pltpu.roll(start, c, axis=0), start)
        c *= 2
    return scores


def _cumsum_rows(lf, tri):
    hi = lf.astype(BF16)
    r1 = lf - hi.astype(F32)
    mid = r1.astype(BF16)
    lo = (r1 - mid.astype(F32)).astype(BF16)
    dot = lambda part: jnp.dot(tri, part, preferred_element_type=F32)
    return dot(hi) + dot(mid) + dot(lo)


def _rec_kernel(q_ref, k_ref, lf_ref, v_ref, sg_ref, og_ref, o_ref, st_ref, *, n_chunks):
    @pl.when(pl.program_id(2) == 0)
    def _():
        st_ref[...] = jnp.zeros_like(st_ref)

    ti = lax.broadcasted_iota(jnp.int32, (CHUNK, CHUNK), 0)
    si = lax.broadcasted_iota(jnp.int32, (CHUNK, CHUNK), 1)
    causal = si <= ti
    tri = jnp.where(causal, 1.0, 0.0).astype(BF16)
    og = og_ref[...]

    def rows(ci):
        return slice(ci * CHUNK, (ci + 1) * CHUNK)

    mean_w = jnp.full((LANES, LANES), 1.0 / LANES, BF16)

    def finish(outs):
        ms = [jnp.dot((o * o).astype(BF16), mean_w, preferred_element_type=F32) for o in outs]
        for ci, o in enumerate(outs):
            o = o * lax.rsqrt(ms[ci] + EPS) * og
            o_ref[rows(ci), :] = (o * sg_ref[rows(ci), :].astype(F32)).astype(BF16)

    cs = range(n_chunks)
    b = [_cumsum_rows(lf_ref[rows(ci), :], tri) for ci in cs]
    total = b[0][CHUNK - 1:CHUNK, :]
    for ci in range(1, n_chunks):
        total = jnp.minimum(total, b[ci][CHUNK - 1:CHUNK, :])
    mild = jnp.min(total) >= -MAX_FACTORED_DECAY
    q_dec = [(q_ref[rows(ci), :].astype(F32) * jnp.exp(b[ci])).astype(BF16) for ci in cs]
    decay = [jnp.exp(b[ci][CHUNK - 1:CHUNK, :]) for ci in cs]
    k_grow = [k_ref[rows(ci), :].astype(F32) * jnp.exp(-b[ci]) for ci in cs]
    scores = [jnp.where(causal, _nt_dot(q_dec[ci], k_grow[ci].astype(BF16)), 0.0).astype(BF16) for ci in cs]
    intra = [jnp.dot(scores[ci], v_ref[rows(ci), :], preferred_element_type=F32) for ci in cs]

    @pl.when(mild)
    def _():
        update = [_tn_dot(v_ref[rows(ci), :], (k_grow[ci] * decay[ci]).astype(BF16)) for ci in cs]
        states = [st_ref[...]]
        for ci in cs:
            states.append(states[ci] * decay[ci] + update[ci])
        finish([intra[ci] + _nt_dot(q_dec[ci], states[ci].astype(BF16)) for ci in cs])
        st_ref[...] = states[n_chunks]

    @pl.when(jnp.logical_not(mild))
    def _():
        state_t = st_ref[...]
        outs = []
        for ci in range(n_chunks):
            lf = lf_ref[rows(ci), :]
            q = q_ref[rows(ci), :].astype(F32)
            k = k_ref[rows(ci), :].astype(F32)
            v = v_ref[rows(ci), :]
            scores = _chunk_scores(q, k, b[ci], lf)
            o = jnp.dot(scores.astype(BF16), v, preferred_element_type=F32)
            o = o + _nt_dot(q_dec[ci], state_t.astype(BF16))
            k_dec = (k * jnp.exp(b[ci][CHUNK - 1:CHUNK, :] - b[ci])).astype(BF16)
            state_t = state_t * decay[ci] + _tn_dot(v, k_dec)
            outs.append(o)
        finish(outs)
        st_ref[...] = state_t


def _hg_rec(q, k, lf, v, sg, o_gain):
    bsz, n_heads, seq, _ = q.shape
    ts = _tile(seq, 1024)
    in_spec = pl.BlockSpec((None, None, ts, LANES), lambda b, h, s: (b, h, s, 0))
    return pl.pallas_call(
        functools.partial(_rec_kernel, n_chunks=ts // CHUNK),
        grid=(bsz, n_heads, seq // ts),
        in_specs=[in_spec] * 5 + [pl.BlockSpec((1, LANES), lambda b, h, s: (0, 0))],
        out_specs=pl.BlockSpec((None, ts, LANES), lambda b, h, s: (b, s, h)),
        out_shape=jax.ShapeDtypeStruct((bsz, seq, n_heads * LANES), BF16),
        scratch_shapes=[pltpu.VMEM((LANES, LANES), F32)],
        compiler_params=_params("parallel", "parallel", "arbitrary"),
        name="hg_rec",
    )(q, k, lf, v, sg, o_gain.reshape(1, LANES))


def _pair_sum_weights():
    row = lax.broadcasted_iota(jnp.int32, (QK_PAD, QK_PAD), 0)
    col = lax.broadcasted_iota(jnp.int32, (QK_PAD, QK_PAD), 1)
    return jnp.where((row < LANES) == (col < LANES), 1.0, 0.0).astype(BF16)


def _rope_pair(u):
    return u + pltpu.roll(u, ROPE, axis=1)


def _mla_kv_kernel(x_ref, gain_ref, shift_ref, scale_ref, wd_ref, lat_gain_ref, wu_ref,
                   gn_ref, gr_ref, cs_ref, k_ref, v_ref, *, n_heads, kv_lora):
    h = _norm_mod(x_ref[...], gain_ref[...], scale_ref[...], shift_ref[...]).astype(BF16)
    ckv = jnp.dot(h, wd_ref[...], preferred_element_type=F32)
    lat = ckv[:, :kv_lora]
    pe2 = ckv[:, kv_lora:]
    c_lat = (lat * lax.rsqrt(jnp.mean(lat * lat, axis=-1, keepdims=True) + EPS) * lat_gain_ref[...]).astype(BF16)
    pair_w = _pair_sum_weights()
    sq_pe = 0.5 * (pe2 * pe2)
    rot = _rope_pair(pe2 * gr_ref[...] * cs_ref[...])
    gn = gn_ref[...]

    def pair_kv(p):
        return [jnp.dot(c_lat, wu_ref[:, hh * QK_PAD:(hh + 1) * QK_PAD], preferred_element_type=F32)
                for hh in (2 * p, 2 * p + 1)]

    nxt = pair_kv(0)
    for p in range(n_heads // 2):
        kvs = nxt
        if p + 1 < n_heads // 2:
            nxt = pair_kv(p + 1)
        sq = jnp.concatenate([kv[:, :LANES] * kv[:, :LANES] + sq_pe for kv in kvs], axis=1)
        r2 = lax.rsqrt(jnp.dot(sq.astype(BF16), pair_w, preferred_element_type=F32) * (1.0 / QK) + EPS)
        for idx, kv in enumerate(kvs):
            hh = 2 * p + idx
            r = r2[:, idx * LANES:(idx + 1) * LANES]
            k_ref[hh, :, :LANES] = (kv[:, :LANES] * r * gn).astype(BF16)
            k_ref[hh, :, LANES:] = (rot * r).astype(BF16)
            v_ref[hh] = kv[:, LANES:].T.astype(BF16)


def _mla_kv(x, gain, mod, wd, lat_gain, wu, gn, gr, cs, bsz, seq):
    t, d = x.shape
    n_heads = d // LANES
    kv_lora = lat_gain.shape[0]
    tm = _tile(seq, ATTN_TILE)
    per_b = seq // tm
    mod_spec = lambda k: pl.BlockSpec((None, 1, d), lambda i: (i // per_b, 0, k))
    assert n_heads % 2 == 0, n_heads
    full = lambda a: pl.BlockSpec(a.shape, lambda i: (0,) * a.ndim)
    lat_gain = lat_gain.reshape(1, kv_lora)
    return pl.pallas_call(
        functools.partial(_mla_kv_kernel, n_heads=n_heads, kv_lora=kv_lora),
        grid=(t // tm,),
        in_specs=[
            pl.BlockSpec((tm, d), lambda i: (i, 0)),
            pl.BlockSpec((1, d), lambda i: (0, 0)),
            mod_spec(0), mod_spec(1),
            full(wd), full(lat_gain), full(wu), full(gn), full(gr),
            pl.BlockSpec((tm, LANES), lambda i: (i, 0)),
        ],
        out_specs=[
            pl.BlockSpec((None, n_heads, tm, QK_PAD), lambda i: (i // per_b, 0, i % per_b, 0)),
            pl.BlockSpec((None, n_heads, None, LANES, tm), lambda i: (i // per_b, 0, i % per_b, 0, 0)),
        ],
        out_shape=[jax.ShapeDtypeStruct((bsz, n_heads, seq, QK_PAD), BF16),
                   jax.ShapeDtypeStruct((bsz, n_heads, seq // tm, LANES, tm), BF16)],
        compiler_params=_params("parallel"),
        name="mla_kv",
    )(x, gain.reshape(1, d), mod, mod, wd, lat_gain, wu, gn, gr, cs)


def _mla_q_kernel(x_ref, gain_ref, shift_ref, scale_ref, wd_ref, lat_gain_ref, wu_ref,
                  gn_ref, gr_ref, cs_ref, q_ref, *, n_heads):
    h = _norm_mod(x_ref[...], gain_ref[...], scale_ref[...], shift_ref[...]).astype(BF16)
    cq = jnp.dot(h, wd_ref[...], preferred_element_type=F32)
    cq = (cq * lax.rsqrt(jnp.mean(cq * cq, axis=-1, keepdims=True) + EPS) * lat_gain_ref[...]).astype(BF16)
    gn = gn_ref[...]
    rope_tab = gr_ref[...] * cs_ref[...]
    pair_w = _pair_sum_weights()

    def pair_q(p):
        return [jnp.dot(cq, wu_ref[:, hh * QK_PAD:(hh + 1) * QK_PAD], preferred_element_type=F32)
                for hh in (2 * p, 2 * p + 1)]

    nxt = pair_q(0)
    for p in range(n_heads // 2):
        qhs = nxt
        if p + 1 < n_heads // 2:
            nxt = pair_q(p + 1)
        sq = jnp.concatenate([qh[:, :LANES] * qh[:, :LANES] + 0.5 * (qh[:, LANES:] * qh[:, LANES:])
                              for qh in qhs], axis=1)
        r2 = lax.rsqrt(jnp.dot(sq.astype(BF16), pair_w, preferred_element_type=F32) * (1.0 / QK) + EPS)
        for idx, qh in enumerate(qhs):
            hh = 2 * p + idx
            r = r2[:, idx * LANES:(idx + 1) * LANES]
            q_ref[hh, :, :LANES] = (qh[:, :LANES] * r * gn).astype(BF16)
            q_ref[hh, :, LANES:] = (qh[:, LANES:] * rope_tab * r).astype(BF16)


def _mla_q(x, gain, mod, wd, layer, lat_gain, wu, gn, gr, cs, bsz, seq):
    t, d = x.shape
    n_heads = d // LANES
    q_lora = lat_gain.shape[0]
    tm = _tile(seq, 512)
    per_b = seq // tm
    mod_spec = lambda k: pl.BlockSpec((None, 1, d), lambda i: (i // per_b, 0, k))
    assert n_heads % 2 == 0, n_heads
    full = lambda a: pl.BlockSpec(a.shape, lambda i: (0,) * a.ndim)
    lat_gain = lat_gain.reshape(1, q_lora)
    return pl.pallas_call(
        functools.partial(_mla_q_kernel, n_heads=n_heads),
        grid=(t // tm,),
        in_specs=[
            pl.BlockSpec((tm, d), lambda i: (i, 0)),
            pl.BlockSpec((1, d), lambda i: (0, 0)),
            mod_spec(0), mod_spec(1),
            pl.BlockSpec((None, d, q_lora), lambda i: (layer, 0, 0)),
            full(lat_gain), full(wu), full(gn), full(gr),
            pl.BlockSpec((tm, LANES), lambda i: (i, 0)),
        ],
        out_specs=pl.BlockSpec((None, n_heads, tm, QK_PAD), lambda i: (i // per_b, 0, i % per_b, 0)),
        out_shape=jax.ShapeDtypeStruct((bsz, n_heads, seq, QK_PAD), BF16),
        compiler_params=_params("parallel"),
        name="mla_q",
    )(x, gain.reshape(1, d), mod, mod, wd, lat_gain, wu, gn, gr, cs)


def _flash_kernel(q_ref, k_ref, vt_ref, o_ref, s_ref, m_ref, l_ref, acc_ref, *, tq, n_tiles, heads_per_step):
    def tile_rows(j):
        return pl.ds(pl.multiple_of(j * tq, tq), tq)

    def scores(hh, i, j):
        return _nt_dot(k_ref[hh, tile_rows(j), :], q_ref[hh, tile_rows(i), :])

    def consume(hh, j, masked):
        s = s_ref[hh]
        if masked:
            ki = lax.broadcasted_iota(jnp.int32, (tq, tq), 0)
            qi = lax.broadcasted_iota(jnp.int32, (tq, tq), 1)
            s = jnp.where(ki <= qi, s, -jnp.inf)
        m = m_ref[hh]
        m_new = jnp.maximum(m, jnp.max(s, axis=0, keepdims=True))
        alpha = jnp.exp2(m - m_new)
        p = jnp.exp2(s - m_new)
        m_ref[hh] = m_new
        l_ref[hh] = alpha * l_ref[hh] + jnp.sum(p, axis=0, keepdims=True)
        pv = jnp.dot(vt_ref[hh, j], p.astype(BF16), preferred_element_type=F32)
        acc_ref[hh] = alpha * acc_ref[hh] + pv

    for hh in range(heads_per_step):
        s_ref[hh] = scores(hh, 0, 0)

    def query_tile(i, carry):
        m_ref[...] = jnp.full(m_ref.shape, -jnp.inf, F32)
        l_ref[...] = jnp.zeros_like(l_ref)
        acc_ref[...] = jnp.zeros_like(acc_ref)

        def key_tile(j, c):
            for hh in range(heads_per_step):
                s_next = scores(hh, i, j + 1)
                consume(hh, j, False)
                s_ref[hh] = s_next
            return c

        lax.fori_loop(0, i, key_tile, 0)
        i_next = jnp.minimum(i + 1, n_tiles - 1)
        for hh in range(heads_per_step):
            s_next = scores(hh, i_next, 0)
            consume(hh, i, True)
            s_ref[hh] = s_next
            o_ref[tile_rows(i), hh * LANES:(hh + 1) * LANES] = (acc_ref[hh] / l_ref[hh]).T.astype(BF16)
        return carry

    lax.fori_loop(0, n_tiles, query_tile, 0)


def _flash(q, k, vt):
    bsz, n_heads, seq, _ = q.shape
    tq = vt.shape[-1]
    hps = 4 if n_heads % 4 == 0 else (2 if n_heads % 2 == 0 else 1)
    return pl.pallas_call(
        functools.partial(_flash_kernel, tq=tq, n_tiles=seq // tq, heads_per_step=hps),
        grid=(bsz, n_heads // hps),
        in_specs=[
            pl.BlockSpec((None, hps, seq, QK_PAD), lambda b, h: (b, h, 0, 0)),
            pl.BlockSpec((None, hps, seq, QK_PAD), lambda b, h: (b, h, 0, 0)),
            pl.BlockSpec((None, hps, seq // tq, LANES, tq), lambda b, h: (b, h, 0, 0, 0)),
        ],
        out_specs=pl.BlockSpec((None, seq, hps * LANES), lambda b, h: (b, 0, h)),
        out_shape=jax.ShapeDtypeStruct((bsz, seq, n_heads * LANES), BF16),
        scratch_shapes=[pltpu.VMEM((hps, tq, tq), F32),
                        pltpu.VMEM((hps, 1, tq), F32), pltpu.VMEM((hps, 1, tq), F32),
                        pltpu.VMEM((hps, LANES, tq), F32)],
        compiler_params=_params("parallel", "parallel"),
        name="flash",
    )(q, k, vt)


def _swap_halves(a):
    half = a.shape[-1] // 2
    return jnp.concatenate([a[..., half:], a[..., :half]], axis=-1)


def _rope_gain(g):
    return jnp.concatenate([g, _swap_halves(g)]).reshape(1, LANES)


def kernel(x, c, positions, ada_w, ada_b, norm_mix, norm_mlp, mlp_w1, mlp_w2, hg_w_in, hg_lower, hg_o_norm, hg_w_out, kv_ada_w, kv_ada_b, kv_in_norm, mla_w_dkv, mla_kv_norm, mla_w_ukv, mla_k_norm, mla_w_dq, mla_q_lat_norm, mla_w_uq, mla_q_norm, mla_w_o):
    bsz, seq, d = x.shape
    depth = ada_w.shape[0]
    n_a = hg_w_in.shape[0]
    n_heads = d // LANES
    kv_lora = mla_kv_norm.shape[0]
    q_lora = mla_w_dq.shape[2]
    t = bsz * seq

    mod = _ada(c, ada_w, ada_b).reshape(depth, bsz, 1, 6 * d)
    kv_mod = _ada(c, kv_ada_w[None], kv_ada_b[None]).reshape(bsz, 1, 2 * d)

    inv_freq = 1.0 / (ROPE_THETA ** (jnp.arange(0, ROPE, 2, dtype=F32) / ROPE))
    ang = positions.astype(F32)[..., None] * inv_freq
    cos, sin = jnp.cos(ang), jnp.sin(ang)
    cs = jnp.concatenate([cos, cos, -sin, sin], axis=-1).reshape(t, LANES)

    lb_all = jnp.cumsum(jax.nn.softmax(hg_lower.astype(F32), axis=0), axis=0)[:n_a]

    hg_w_in, hg_w_out, mla_w_dq, mla_w_o, mlp_w1, mlp_w2 = (
        w.astype(BF16) for w in (hg_w_in, hg_w_out, mla_w_dq, mla_w_o, mlp_w1, mlp_w2))

    xf = x.reshape(t, d)
    k_sh = v_sh = None
    for layer in range(depth):
        if layer == n_a:
            wd = jnp.concatenate([mla_w_dkv, _swap_halves(mla_w_dkv[:, kv_lora:])], axis=1).astype(BF16)
            k_sh, v_sh = _mla_kv(xf, kv_in_norm, kv_mod, wd, mla_kv_norm, mla_w_ukv.astype(BF16),
                                 mla_k_norm[:LANES].reshape(1, LANES), _rope_gain(mla_k_norm[LANES:]),
                                 cs, bsz, seq)
        if layer < n_a:
            q, k, lf, v, sg = _hg_in(xf, norm_mix[layer], mod[layer], lb_all[layer],
                                     hg_w_in, layer, bsz, seq)
            y = _hg_rec(q, k, lf, v, sg, hg_o_norm[layer]).reshape(t, d)
            xf = _proj_res(y, hg_w_out, layer, xf, mod[layer], 2, seq)
        else:
            j = layer - n_a
            wu = mla_w_uq[j].reshape(q_lora, n_heads, QK)
            wu = jnp.concatenate([wu, _swap_halves(wu[..., LANES:])], axis=-1)
            wu = wu.reshape(q_lora, n_heads * QK_PAD).astype(BF16)
            sm_scale = float(QK) ** -0.5 * LOG2E
            qg = mla_q_norm[j] * sm_scale
            qh = _mla_q(xf, norm_mix[layer], mod[layer], mla_w_dq, j, mla_q_lat_norm[j], wu,
                        qg[:LANES].reshape(1, LANES), _rope_gain(qg[LANES:]), cs, bsz, seq)
            y = _flash(qh, k_sh, v_sh).reshape(t, d)
            xf = _proj_res(y, mla_w_o, j, xf, mod[layer], 2, seq)
        xf = _mlp(xf, norm_mlp[layer], mod[layer], mlp_w1, mlp_w2, layer, seq)
    return xf.reshape(bsz, seq, d)
```

```python
import functools

import jax
import jax.numpy as jnp
from jax import lax
from jax.experimental import pallas as pl
from jax.experimental.pallas import tpu as pltpu

F32 = jnp.float32
BF16 = jnp.bfloat16

EPS = 1e-6
LANES = 128
ROPE = 64
QK = LANES + ROPE
QK_PAD = 2 * LANES
CHUNK = 64
ATTN_TILE = 512
MAX_FACTORED_DECAY = 72.0
ROPE_THETA = 10000.0
LOG2E = 1.4426950408889634
VMEM_LIMIT = 56 * 1024 * 1024


def _params(*sem):
    return pltpu.CompilerParams(dimension_semantics=sem, vmem_limit_bytes=VMEM_LIMIT)


def _tile(n, pref):
    t = min(n, pref)
    while n % t:
        t -= LANES
    assert t > 0 and n % t == 0, (n, pref)
    return t


def _norm_mod(x, gain, scale, shift):
    r = lax.rsqrt(jnp.mean(x * x, axis=-1, keepdims=True) + EPS)
    return (x * r) * (gain * (1.0 + scale)) + shift


def _ada_kernel(c_ref, w_ref, b_ref, o_ref):
    c = c_ref[...]
    cond = (c / (1.0 + jnp.exp(-c))).astype(BF16)
    o_ref[...] = jnp.dot(cond, w_ref[...].astype(BF16), preferred_element_type=F32) + b_ref[...]


def _ada(c, w, b):
    n_layers, d, n = w.shape
    bsz = c.shape[0]
    tn = _tile(n, 1024)
    return pl.pallas_call(
        _ada_kernel,
        grid=(n_layers, n // tn),
        in_specs=[
            pl.BlockSpec((bsz, d), lambda l, j: (0, 0)),
            pl.BlockSpec((None, d, tn), lambda l, j: (l, 0, j)),
            pl.BlockSpec((None, 1, tn), lambda l, j: (l, 0, j)),
        ],
        out_specs=pl.BlockSpec((None, bsz, tn), lambda l, j: (l, 0, j)),
        out_shape=jax.ShapeDtypeStruct((n_layers, bsz, n), F32),
        compiler_params=_params("parallel", "parallel"),
        name="ada",
    )(c, w, b.reshape(n_layers, 1, n))


def _mlp_kernel(x_ref, gain_ref, shift_ref, scale_ref, gate_ref, w1_ref, w2_ref, o_ref, h_ref):
    j = pl.program_id(1)

    def gated_chunk():
        a = jnp.maximum(jnp.dot(h_ref[...], w1_ref[...], preferred_element_type=F32), 0.0)
        return gate_ref[...] * jnp.dot((a * a).astype(BF16), w2_ref[...], preferred_element_type=F32)

    @pl.when(j == 0)
    def _():
        h_ref[...] = _norm_mod(x_ref[...], gain_ref[...], scale_ref[...], shift_ref[...]).astype(BF16)
        o_ref[...] = x_ref[...] + gated_chunk()

    @pl.when(j > 0)
    def _():
        o_ref[...] += gated_chunk()


def _mlp(x, gain, mod, w1, w2, layer, seq):
    t, d = x.shape
    f = w1.shape[2]
    tm = _tile(seq, 1024)
    tf = _tile(f, 512)
    per_b = seq // tm
    mod_spec = lambda k: pl.BlockSpec((None, 1, d), lambda i, j: (i // per_b, 0, k))
    return pl.pallas_call(
        _mlp_kernel,
        grid=(t // tm, f // tf),
        in_specs=[
            pl.BlockSpec((tm, d), lambda i, j: (i, 0)),
            pl.BlockSpec((1, d), lambda i, j: (0, 0)),
            mod_spec(3), mod_spec(4), mod_spec(5),
            pl.BlockSpec((None, d, tf), lambda i, j: (layer, 0, j)),
            pl.BlockSpec((None, tf, d), lambda i, j: (layer, j, 0)),
        ],
        out_specs=pl.BlockSpec((tm, d), lambda i, j: (i, 0)),
        out_shape=jax.ShapeDtypeStruct((t, d), F32),
        scratch_shapes=[pltpu.VMEM((tm, d), BF16)],
        compiler_params=_params("parallel", "arbitrary"),
        name="mlp",
    )(x, gain.reshape(1, d), mod, mod, mod, w1, w2)


def _proj_res_kernel(y_ref, w_ref, x_ref, gate_ref, o_ref):
    o_ref[...] = x_ref[...] + gate_ref[...] * jnp.dot(y_ref[...], w_ref[...], preferred_element_type=F32)


def _proj_res(y, w, layer, x, mod, gate_idx, seq):
    t, d = x.shape
    k = y.shape[1]
    tm = _tile(seq, 512)
    tn = _tile(d, 2048)
    per_b = seq // tm
    return pl.pallas_call(
        _proj_res_kernel,
        grid=(t // tm, d // tn),
        in_specs=[
            pl.BlockSpec((tm, k), lambda i, j: (i, 0)),
            pl.BlockSpec((None, k, tn), lambda i, j: (layer, 0, j)),
            pl.BlockSpec((tm, tn), lambda i, j: (i, j)),
            pl.BlockSpec((None, 1, tn), lambda i, j: (i // per_b, 0, gate_idx * (d // tn) + j)),
        ],
        out_specs=pl.BlockSpec((tm, tn), lambda i, j: (i, j)),
        out_shape=jax.ShapeDtypeStruct((t, d), F32),
        compiler_params=_params("parallel", "parallel"),
        name="proj_res",
    )(y, w, x, mod)


def _hg_in_kernel(x_ref, gain_ref, shift_ref, scale_ref, llb_ref, l1mlb_ref, omlb_ref,
                  wq_ref, wf_ref, wi_ref, wg_ref,
                  q_ref, k_ref, lf_ref, v_ref, sg_ref, h_ref, *, heads_per_step, q_scale):
    def project():
        h = h_ref[...]
        z = jnp.dot(h, wf_ref[...], preferred_element_type=F32)
        q = jnp.dot(h, wq_ref[...], preferred_element_type=F32) * q_scale
        v = jnp.dot(h, wi_ref[...], preferred_element_type=F32)
        g = jnp.dot(h, wg_ref[...], preferred_element_type=F32)

        e = jnp.exp(-jnp.abs(z))
        log_sig = jnp.minimum(z, 0.0) - jnp.log(1.0 + e)
        a = llb_ref[...]
        c = l1mlb_ref[...] + log_sig
        lf = jnp.maximum(a, c) + jnp.log(1.0 + jnp.exp(-jnp.abs(a - c)))
        key = omlb_ref[...] * (jnp.where(z >= 0.0, e, 1.0) / (1.0 + e))
        sg = g / (1.0 + jnp.exp(-g))

        for hh in range(heads_per_step):
            cs = slice(hh * LANES, (hh + 1) * LANES)
            q_ref[hh] = q[:, cs].astype(BF16)
            k_ref[hh] = key[:, cs].astype(BF16)
            lf_ref[hh] = lf[:, cs]
            v_ref[hh] = v[:, cs].astype(BF16)
            sg_ref[hh] = sg[:, cs].astype(BF16)

    @pl.when(pl.program_id(1) == 0)
    def _():
        h_ref[...] = _norm_mod(x_ref[...], gain_ref[...], scale_ref[...], shift_ref[...]).astype(BF16)
        project()

    @pl.when(pl.program_id(1) > 0)
    def _():
        project()


def _hg_in(x, gain, mod, lb, w_in, layer, bsz, seq):
    t, d = x.shape
    n_heads = d // LANES
    tm = _tile(seq, 1024)
    tn = _tile(d, 2 * LANES)
    hps = tn // LANES
    nj = d // tn
    per_b = seq // tm
    mod_spec = lambda k: pl.BlockSpec((None, 1, d), lambda i, j: (i // per_b, 0, k))
    vec_spec = pl.BlockSpec((1, tn), lambda i, j: (0, j))
    w_spec = lambda grp: pl.BlockSpec((None, d, tn), lambda i, j: (layer, 0, grp * nj + j))
    out_spec = pl.BlockSpec((None, hps, tm, LANES), lambda i, j: (i // per_b, j, i % per_b, 0))
    hm = lambda dt: jax.ShapeDtypeStruct((bsz, n_heads, seq, LANES), dt)
    lbf = lb.reshape(1, d)
    kern = functools.partial(_hg_in_kernel, heads_per_step=hps, q_scale=float(LANES) ** -0.5)
    return pl.pallas_call(
        kern,
        grid=(t // tm, nj),
        in_specs=[
            pl.BlockSpec((tm, d), lambda i, j: (i, 0)),
            pl.BlockSpec((1, d), lambda i, j: (0, 0)),
            mod_spec(0), mod_spec(1),
            vec_spec, vec_spec, vec_spec,
            w_spec(0), w_spec(1), w_spec(2), w_spec(3),
        ],
        out_specs=[out_spec] * 5,
        out_shape=[hm(BF16), hm(BF16), hm(F32), hm(BF16), hm(BF16)],
        scratch_shapes=[pltpu.VMEM((tm, d), BF16)],
        compiler_params=_params("parallel", "arbitrary"),
        name="hg_in",
    )(x, gain.reshape(1, d), mod, mod, jnp.log(lbf), jnp.log1p(-lbf), 1.0 - lbf, w_in, w_in, w_in, w_in)


def _nt_dot(a, b):
    return lax.dot_general(a, b, (((1,), (1,)), ((), ())), preferred_element_type=F32)


def _tn_dot(a, b):
    return lax.dot_general(a, b, (((0,), (0,)), ((), ())), preferred_element_type=F32)


def _chunk_scores(q, k, b, lf):
    row = lax.broadcasted_iota(jnp.int32, (CHUNK, LANES), 0)
    ti = lax.broadcasted_iota(jnp.int32, (CHUNK, CHUNK), 0)
    si = lax.broadcasted_iota(jnp.int32, (CHUNK, CHUNK), 1)
    scores = jnp.where(ti == si, _nt_dot(q.astype(BF16), k.astype(BF16)), 0.0)
    start = b - lf
    c = 1
    while c < CHUNK:
        later = (row & c) != 0
        bound = jnp.where(later, start, pltpu.roll(start, CHUNK - c, axis=0))
        e = jnp.exp(jnp.where(later, b - bound, bound - b))
        m = jnp.where(later, q, k) * e
        qs = jnp.where(later, m, 0.0).astype(BF16)
        ks = jnp.where(later, 0.0, m).astype(BF16)
        same_parent = (ti & -(2 * c)) == (si & -(2 * c))
        scores = scores + jnp.where(same_parent, _nt_dot(qs, ks), 0.0)
        if 2 * c < CHUNK:
            start = jnp.where(later, pltpu.roll(start, c, axis=0), start)
        c *= 2
    return scores


def _cumsum_rows(lf, tri):
    hi = lf.astype(BF16)
    r1 = lf - hi.astype(F32)
    mid = r1.astype(BF16)
    lo = (r1 - mid.astype(F32)).astype(BF16)
    dot = lambda part: jnp.dot(tri, part, preferred_element_type=F32)
    return dot(hi) + dot(mid) + dot(lo)


def _rec_kernel(q_ref, k_ref, lf_ref, v_ref, sg_ref, og_ref, o_ref, st_ref, *, n_chunks):
    @pl.when(pl.program_id(2) == 0)
    def _():
        st_ref[...] = jnp.zeros_like(st_ref)

    ti = lax.broadcasted_iota(jnp.int32, (CHUNK, CHUNK), 0)
    si = lax.broadcasted_iota(jnp.int32, (CHUNK, CHUNK), 1)
    causal = si <= ti
    tri = jnp.where(causal, 1.0, 0.0).astype(BF16)
    og = og_ref[...]

    def rows(ci):
        return slice(ci * CHUNK, (ci + 1) * CHUNK)

    mean_w = jnp.full((LANES, LANES), 1.0 / LANES, BF16)

    def finish(outs):
        ms = [jnp.dot((o * o).astype(BF16), mean_w, preferred_element_type=F32) for o in outs]
        for ci, o in enumerate(outs):
            o = o * lax.rsqrt(ms[ci] + EPS) * og
            o_ref[rows(ci), :] = (o * sg_ref[rows(ci), :].astype(F32)).astype(BF16)

    cs = range(n_chunks)
    b = [_cumsum_rows(lf_ref[rows(ci), :], tri) for ci in cs]
    total = b[0][CHUNK - 1:CHUNK, :]
    for ci in range(1, n_chunks):
        total = jnp.minimum(total, b[ci][CHUNK - 1:CHUNK, :])
    mild = jnp.min(total) >= -MAX_FACTORED_DECAY
    q_dec = [(q_ref[rows(ci), :].astype(F32) * jnp.exp(b[ci])).astype(BF16) for ci in cs]
    decay = [jnp.exp(b[ci][CHUNK - 1:CHUNK, :]) for ci in cs]
    k_grow = [k_ref[rows(ci), :].astype(F32) * jnp.exp(-b[ci]) for ci in cs]
    scores = [jnp.where(causal, _nt_dot(q_dec[ci], k_grow[ci].astype(BF16)), 0.0).astype(BF16) for ci in cs]
    intra = [jnp.dot(scores[ci], v_ref[rows(ci), :], preferred_element_type=F32) for ci in cs]

    @pl.when(mild)
    def _():
        update = [_tn_dot(v_ref[rows(ci), :], (k_grow[ci] * decay[ci]).astype(BF16)) for ci in cs]
        states = [st_ref[...]]
        for ci in cs:
            states.append(states[ci] * decay[ci] + update[ci])
        finish([intra[ci] + _nt_dot(q_dec[ci], states[ci].astype(BF16)) for ci in cs])
        st_ref[...] = states[n_chunks]

    @pl.when(jnp.logical_not(mild))
    def _():
        state_t = st_ref[...]
        outs = []
        for ci in range(n_chunks):
            lf = lf_ref[rows(ci), :]
            q = q_ref[rows(ci), :].astype(F32)
            k = k_ref[rows(ci), :].astype(F32)
            v = v_ref[rows(ci), :]
            scores = _chunk_scores(q, k, b[ci], lf)
            o = jnp.dot(scores.astype(BF16), v, preferred_element_type=F32)
            o = o + _nt_dot(q_dec[ci], state_t.astype(BF16))
            k_dec = (k * jnp.exp(b[ci][CHUNK - 1:CHUNK, :] - b[ci])).astype(BF16)
            state_t = state_t * decay[ci] + _tn_dot(v, k_dec)
            outs.append(o)
        finish(outs)
        st_ref[...] = state_t


def _hg_rec(q, k, lf, v, sg, o_gain):
    bsz, n_heads, seq, _ = q.shape
    ts = _tile(seq, 1024)
    in_spec = pl.BlockSpec((None, None, ts, LANES), lambda b, h, s: (b, h, s, 0))
    return pl.pallas_call(
        functools.partial(_rec_kernel, n_chunks=ts // CHUNK),
        grid=(bsz, n_heads, seq // ts),
        in_specs=[in_spec] * 5 + [pl.BlockSpec((1, LANES), lambda b, h, s: (0, 0))],
        out_specs=pl.BlockSpec((None, ts, LANES), lambda b, h, s: (b, s, h)),
        out_shape=jax.ShapeDtypeStruct((bsz, seq, n_heads * LANES), BF16),
        scratch_shapes=[pltpu.VMEM((LANES, LANES), F32)],
        compiler_params=_params("parallel", "parallel", "arbitrary"),
        name="hg_rec",
    )(q, k, lf, v, sg, o_gain.reshape(1, LANES))


def _pair_sum_weights():
    row = lax.broadcasted_iota(jnp.int32, (QK_PAD, QK_PAD), 0)
    col = lax.broadcasted_iota(jnp.int32, (QK_PAD, QK_PAD), 1)
    return jnp.where((row < LANES) == (col < LANES), 1.0, 0.0).astype(BF16)


def _rope_pair(u):
    return u + pltpu.roll(u, ROPE, axis=1)


def _mla_kv_kernel(x_ref, gain_ref, shift_ref, scale_ref, wd_ref, lat_gain_ref, wu_ref,
                   gn_ref, gr_ref, cs_ref, k_ref, v_ref, *, n_heads, kv_lora):
    h = _norm_mod(x_ref[...], gain_ref[...], scale_ref[...], shift_ref[...]).astype(BF16)
    ckv = jnp.dot(h, wd_ref[...], preferred_element_type=F32)
    lat = ckv[:, :kv_lora]
    pe2 = ckv[:, kv_lora:]
    c_lat = (lat * lax.rsqrt(jnp.mean(lat * lat, axis=-1, keepdims=True) + EPS) * lat_gain_ref[...]).astype(BF16)
    pair_w = _pair_sum_weights()
    sq_pe = 0.5 * (pe2 * pe2)
    rot = _rope_pair(pe2 * gr_ref[...] * cs_ref[...])
    gn = gn_ref[...]

    def pair_kv(p):
        return [jnp.dot(c_lat, wu_ref[:, hh * QK_PAD:(hh + 1) * QK_PAD], preferred_element_type=F32)
                for hh in (2 * p, 2 * p + 1)]

    nxt = pair_kv(0)
    for p in range(n_heads // 2):
        kvs = nxt
        if p + 1 < n_heads // 2:
            nxt = pair_kv(p + 1)
        sq = jnp.concatenate([kv[:, :LANES] * kv[:, :LANES] + sq_pe for kv in kvs], axis=1)
        r2 = lax.rsqrt(jnp.dot(sq.astype(BF16), pair_w, preferred_element_type=F32) * (1.0 / QK) + EPS)
        for idx, kv in enumerate(kvs):
            hh = 2 * p + idx
            r = r2[:, idx * LANES:(idx + 1) * LANES]
            k_ref[hh, :, :LANES] = (kv[:, :LANES] * r * gn).astype(BF16)
            k_ref[hh, :, LANES:] = (rot * r).astype(BF16)
            v_ref[hh] = kv[:, LANES:].T.astype(BF16)


def _mla_kv(x, gain, mod, wd, lat_gain, wu, gn, gr, cs, bsz, seq):
    t, d = x.shape
    n_heads = d // LANES
    kv_lora = lat_gain.shape[0]
    tm = _tile(seq, ATTN_TILE)
    per_b = seq // tm
    mod_spec = lambda k: pl.BlockSpec((None, 1, d), lambda i: (i // per_b, 0, k))
    assert n_heads % 2 == 0, n_heads
    full = lambda a: pl.BlockSpec(a.shape, lambda i: (0,) * a.ndim)
    lat_gain = lat_gain.reshape(1, kv_lora)
    return pl.pallas_call(
        functools.partial(_mla_kv_kernel, n_heads=n_heads, kv_lora=kv_lora),
        grid=(t // tm,),
        in_specs=[
            pl.BlockSpec((tm, d), lambda i: (i, 0)),
            pl.BlockSpec((1, d), lambda i: (0, 0)),
            mod_spec(0), mod_spec(1),
            full(wd), full(lat_gain), full(wu), full(gn), full(gr),
            pl.BlockSpec((tm, LANES), lambda i: (i, 0)),
        ],
        out_specs=[
            pl.BlockSpec((None, n_heads, tm, QK_PAD), lambda i: (i // per_b, 0, i % per_b, 0)),
            pl.BlockSpec((None, n_heads, None, LANES, tm), lambda i: (i // per_b, 0, i % per_b, 0, 0)),
        ],
        out_shape=[jax.ShapeDtypeStruct((bsz, n_heads, seq, QK_PAD), BF16),
                   jax.ShapeDtypeStruct((bsz, n_heads, seq // tm, LANES, tm), BF16)],
        compiler_params=_params("parallel"),
        name="mla_kv",
    )(x, gain.reshape(1, d), mod, mod, wd, lat_gain, wu, gn, gr, cs)


def _mla_q_kernel(x_ref, gain_ref, shift_ref, scale_ref, wd_ref, lat_gain_ref, wu_ref,
                  gn_ref, gr_ref, cs_ref, q_ref, *, n_heads):
    h = _norm_mod(x_ref[...], gain_ref[...], scale_ref[...], shift_ref[...]).astype(BF16)
    cq = jnp.dot(h, wd_ref[...], preferred_element_type=F32)
    cq = (cq * lax.rsqrt(jnp.mean(cq * cq, axis=-1, keepdims=True) + EPS) * lat_gain_ref[...]).astype(BF16)
    gn = gn_ref[...]
    rope_tab = gr_ref[...] * cs_ref[...]
    pair_w = _pair_sum_weights()

    def pair_q(p):
        return [jnp.dot(cq, wu_ref[:, hh * QK_PAD:(hh + 1) * QK_PAD], preferred_element_type=F32)
                for hh in (2 * p, 2 * p + 1)]

    nxt = pair_q(0)
    for p in range(n_heads // 2):
        qhs = nxt
        if p + 1 < n_heads // 2:
            nxt = pair_q(p + 1)
        sq = jnp.concatenate([qh[:, :LANES] * qh[:, :LANES] + 0.5 * (qh[:, LANES:] * qh[:, LANES:])
                              for qh in qhs], axis=1)
        r2 = lax.rsqrt(jnp.dot(sq.astype(BF16), pair_w, preferred_element_type=F32) * (1.0 / QK) + EPS)
        for idx, qh in enumerate(qhs):
            hh = 2 * p + idx
            r = r2[:, idx * LANES:(idx + 1) * LANES]
            q_ref[hh, :LANES, :] = (qh[:, :LANES] * r * gn).T.astype(BF16)
            q_ref[hh, LANES:, :] = (qh[:, LANES:] * rope_tab * r).T.astype(BF16)


def _mla_q(x, gain, mod, wd, layer, lat_gain, wu, gn, gr, cs, bsz, seq):
    t, d = x.shape
    n_heads = d // LANES
    q_lora = lat_gain.shape[0]
    tm = _tile(seq, ATTN_TILE)
    per_b = seq // tm
    mod_spec = lambda k: pl.BlockSpec((None, 1, d), lambda i: (i // per_b, 0, k))
    assert n_heads % 2 == 0, n_heads
    full = lambda a: pl.BlockSpec(a.shape, lambda i: (0,) * a.ndim)
    lat_gain = lat_gain.reshape(1, q_lora)
    return pl.pallas_call(
        functools.partial(_mla_q_kernel, n_heads=n_heads),
        grid=(t // tm,),
        in_specs=[
            pl.BlockSpec((tm, d), lambda i: (i, 0)),
            pl.BlockSpec((1, d), lambda i: (0, 0)),
            mod_spec(0), mod_spec(1),
            pl.BlockSpec((None, d, q_lora), lambda i: (layer, 0, 0)),
            full(lat_gain), full(wu), full(gn), full(gr),
            pl.BlockSpec((tm, LANES), lambda i: (i, 0)),
        ],
        out_specs=pl.BlockSpec((None, n_heads, None, QK_PAD, tm), lambda i: (i // per_b, 0, i % per_b, 0, 0)),
        out_shape=jax.ShapeDtypeStruct((bsz, n_heads, seq // tm, QK_PAD, tm), BF16),
        compiler_params=_params("parallel"),
        name="mla_q",
    )(x, gain.reshape(1, d), mod, mod, wd, lat_gain, wu, gn, gr, cs)


def _flash_kernel(qt_ref, k_ref, vt_ref, o_ref, s_ref, m_ref, l_ref, acc_ref, *, tq, n_tiles, heads_per_step):
    def tile_rows(j):
        return pl.ds(pl.multiple_of(j * tq, tq), tq)

    def scores(hh, i, j):
        return jnp.dot(k_ref[hh, tile_rows(j), :], qt_ref[hh, i], preferred_element_type=F32)

    def consume(hh, j, masked):
        s = s_ref[hh]
        if masked:
            ki = lax.broadcasted_iota(jnp.int32, (tq, tq), 0)
            qi = lax.broadcasted_iota(jnp.int32, (tq, tq), 1)
            s = jnp.where(ki <= qi, s, -jnp.inf)
        m = m_ref[hh]
        m_new = jnp.maximum(m, jnp.max(s, axis=0, keepdims=True))
        alpha = jnp.exp2(m - m_new)
        p = jnp.exp2(s - m_new)
        m_ref[hh] = m_new
        l_ref[hh] = alpha * l_ref[hh] + jnp.sum(p, axis=0, keepdims=True)
        pv = jnp.dot(vt_ref[hh, j], p.astype(BF16), preferred_element_type=F32)
        acc_ref[hh] = alpha * acc_ref[hh] + pv

    for hh in range(heads_per_step):
        s_ref[hh] = scores(hh, 0, 0)

    def query_tile(i, carry):
        m_ref[...] = jnp.full(m_ref.shape, -jnp.inf, F32)
        l_ref[...] = jnp.zeros_like(l_ref)
        acc_ref[...] = jnp.zeros_like(acc_ref)

        def key_tile(j, c):
            for hh in range(heads_per_step):
                s_next = scores(hh, i, j + 1)
                consume(hh, j, False)
                s_ref[hh] = s_next
            return c

        lax.fori_loop(0, i, key_tile, 0)
        i_next = jnp.minimum(i + 1, n_tiles - 1)
        for hh in range(heads_per_step):
            s_next = scores(hh, i_next, 0)
            consume(hh, i, True)
            s_ref[hh] = s_next
            o_ref[tile_rows(i), hh * LANES:(hh + 1) * LANES] = (acc_ref[hh] / l_ref[hh]).T.astype(BF16)
        return carry

    lax.fori_loop(0, n_tiles, query_tile, 0)


def _flash(qt, k, vt):
    bsz, n_heads, seq, _ = k.shape
    tq = vt.shape[-1]
    hps = 4 if n_heads % 4 == 0 else (2 if n_heads % 2 == 0 else 1)
    return pl.pallas_call(
        functools.partial(_flash_kernel, tq=tq, n_tiles=seq // tq, heads_per_step=hps),
        grid=(bsz, n_heads // hps),
        in_specs=[
            pl.BlockSpec((None, hps, seq // tq, QK_PAD, tq), lambda b, h: (b, h, 0, 0, 0)),
            pl.BlockSpec((None, hps, seq, QK_PAD), lambda b, h: (b, h, 0, 0)),
            pl.BlockSpec((None, hps, seq // tq, LANES, tq), lambda b, h: (b, h, 0, 0, 0)),
        ],
        out_specs=pl.BlockSpec((None, seq, hps * LANES), lambda b, h: (b, 0, h)),
        out_shape=jax.ShapeDtypeStruct((bsz, seq, n_heads * LANES), BF16),
        scratch_shapes=[pltpu.VMEM((hps, tq, tq), F32),
                        pltpu.VMEM((hps, 1, tq), F32), pltpu.VMEM((hps, 1, tq), F32),
                        pltpu.VMEM((hps, LANES, tq), F32)],
        compiler_params=_params("parallel", "parallel"),
        name="flash",
    )(qt, k, vt)


def _swap_halves(a):
    half = a.shape[-1] // 2
    return jnp.concatenate([a[..., half:], a[..., :half]], axis=-1)


def _rope_gain(g):
    return jnp.concatenate([g, _swap_halves(g)]).reshape(1, LANES)


def kernel(x, c, positions, ada_w, ada_b, norm_mix, norm_mlp, mlp_w1, mlp_w2, hg_w_in, hg_lower, hg_o_norm, hg_w_out, kv_ada_w, kv_ada_b, kv_in_norm, mla_w_dkv, mla_kv_norm, mla_w_ukv, mla_k_norm, mla_w_dq, mla_q_lat_norm, mla_w_uq, mla_q_norm, mla_w_o):
    bsz, seq, d = x.shape
    depth = ada_w.shape[0]
    n_a = hg_w_in.shape[0]
    n_heads = d // LANES
    kv_lora = mla_kv_norm.shape[0]
    q_lora = mla_w_dq.shape[2]
    t = bsz * seq

    mod = _ada(c, ada_w, ada_b).reshape(depth, bsz, 1, 6 * d)
    kv_mod = _ada(c, kv_ada_w[None], kv_ada_b[None]).reshape(bsz, 1, 2 * d)

    inv_freq = 1.0 / (ROPE_THETA ** (jnp.arange(0, ROPE, 2, dtype=F32) / ROPE))
    ang = positions.astype(F32)[..., None] * inv_freq
    cos, sin = jnp.cos(ang), jnp.sin(ang)
    cs = jnp.concatenate([cos, cos, -sin, sin], axis=-1).reshape(t, LANES)

    lb_all = jnp.cumsum(jax.nn.softmax(hg_lower.astype(F32), axis=0), axis=0)[:n_a]

    hg_w_in, hg_w_out, mla_w_dq, mla_w_o, mlp_w1, mlp_w2 = (
        w.astype(BF16) for w in (hg_w_in, hg_w_out, mla_w_dq, mla_w_o, mlp_w1, mlp_w2))

    xf = x.reshape(t, d)
    k_sh = v_sh = None
    for layer in range(depth):
        if layer == n_a:
            wd = jnp.concatenate([mla_w_dkv, _swap_halves(mla_w_dkv[:, kv_lora:])], axis=1).astype(BF16)
            k_sh, v_sh = _mla_kv(xf, kv_in_norm, kv_mod, wd, mla_kv_norm, mla_w_ukv.astype(BF16),
                                 mla_k_norm[:LANES].reshape(1, LANES), _rope_gain(mla_k_norm[LANES:]),
                                 cs, bsz, seq)
        if layer < n_a:
            q, k, lf, v, sg = _hg_in(xf, norm_mix[layer], mod[layer], lb_all[layer],
                                     hg_w_in, layer, bsz, seq)
            y = _hg_rec(q, k, lf, v, sg, hg_o_norm[layer]).reshape(t, d)
            xf = _proj_res(y, hg_w_out, layer, xf, mod[layer], 2, seq)
        else:
            j = layer - n_a
            wu = mla_w_uq[j].reshape(q_lora, n_heads, QK)
            wu = jnp.concatenate([wu, _swap_halves(wu[..., LANES:])], axis=-1)
            wu = wu.reshape(q_lora, n_heads * QK_PAD).astype(BF16)
            sm_scale = float(QK) ** -0.5 * LOG2E
            qg = mla_q_norm[j] * sm_scale
            qh = _mla_q(xf, norm_mix[layer], mod[layer], mla_w_dq, j, mla_q_lat_norm[j], wu,
                        qg[:LANES].reshape(1, LANES), _rope_gain(qg[LANES:]), cs, bsz, seq)
            y = _flash(qh, k_sh, v_sh).reshape(t, d)
            xf = _proj_res(y, mla_w_o, j, xf, mod[layer], 2, seq)
        xf = _mlp(xf, norm_mlp[layer], mod[layer], mlp_w1, mlp_w2, layer, seq)
    return xf.reshape(bsz, seq, d)
```

```python
import functools

import jax
import jax.numpy as jnp
from jax import lax
from jax.experimental import pallas as pl
from jax.experimental.pallas import tpu as pltpu

F32 = jnp.float32
BF16 = jnp.bfloat16

EPS = 1e-6
LANES = 128
ROPE = 64
QK = LANES + ROPE
QK_PAD = 2 * LANES
CHUNK = 64
ATTN_TILE = 512
MAX_FACTORED_DECAY = 72.0
ROPE_THETA = 10000.0
LOG2E = 1.4426950408889634
VMEM_LIMIT = 56 * 1024 * 1024


def _params(*sem):
    return pltpu.CompilerParams(dimension_semantics=sem, vmem_limit_bytes=VMEM_LIMIT)


def _tile(n, pref):
    t = min(n, pref)
    while n % t:
        t -= LANES
    assert t > 0 and n % t == 0, (n, pref)
    return t


def _norm_mod(x, gain, scale, shift):
    r = lax.rsqrt(jnp.mean(x * x, axis=-1, keepdims=True) + EPS)
    return (x * r) * (gain * (1.0 + scale)) + shift


def _ada_kernel(c_ref, w_ref, b_ref, o_ref):
    c = c_ref[...]
    cond = (c / (1.0 + jnp.exp(-c))).astype(BF16)
    o_ref[...] = jnp.dot(cond, w_ref[...].astype(BF16), preferred_element_type=F32) + b_ref[...]


def _ada(c, w, b):
    n_layers, d, n = w.shape
    bsz = c.shape[0]
    tn = _tile(n, 1024)
    return pl.pallas_call(
        _ada_kernel,
        grid=(n_layers, n // tn),
        in_specs=[
            pl.BlockSpec((bsz, d), lambda l, j: (0, 0)),
            pl.BlockSpec((None, d, tn), lambda l, j: (l, 0, j)),
            pl.BlockSpec((None, 1, tn), lambda l, j: (l, 0, j)),
        ],
        out_specs=pl.BlockSpec((None, bsz, tn), lambda l, j: (l, 0, j)),
        out_shape=jax.ShapeDtypeStruct((n_layers, bsz, n), F32),
        compiler_params=_params("parallel", "parallel"),
        name="ada",
    )(c, w, b.reshape(n_layers, 1, n))


def _mlp_kernel(x_ref, gain_ref, shift_ref, scale_ref, gate_ref, w1_ref, w2_ref, o_ref, h_ref):
    j = pl.program_id(1)

    def gated_chunk():
        a = jnp.maximum(jnp.dot(h_ref[...], w1_ref[...], preferred_element_type=F32), 0.0)
        return gate_ref[...] * jnp.dot((a * a).astype(BF16), w2_ref[...], preferred_element_type=F32)

    @pl.when(j == 0)
    def _():
        h_ref[...] = _norm_mod(x_ref[...], gain_ref[...], scale_ref[...], shift_ref[...]).astype(BF16)
        o_ref[...] = x_ref[...] + gated_chunk()

    @pl.when(j > 0)
    def _():
        o_ref[...] += gated_chunk()


def _mlp(x, gain, mod, w1, w2, layer, seq):
    t, d = x.shape
    f = w1.shape[2]
    tm = _tile(seq, 1024)
    tf = _tile(f, 512)
    per_b = seq // tm
    mod_spec = lambda k: pl.BlockSpec((None, 1, d), lambda i, j: (i // per_b, 0, k))
    return pl.pallas_call(
        _mlp_kernel,
        grid=(t // tm, f // tf),
        in_specs=[
            pl.BlockSpec((tm, d), lambda i, j: (i, 0)),
            pl.BlockSpec((1, d), lambda i, j: (0, 0)),
            mod_spec(3), mod_spec(4), mod_spec(5),
            pl.BlockSpec((None, d, tf), lambda i, j: (layer, 0, j)),
            pl.BlockSpec((None, tf, d), lambda i, j: (layer, j, 0)),
        ],
        out_specs=pl.BlockSpec((tm, d), lambda i, j: (i, 0)),
        out_shape=jax.ShapeDtypeStruct((t, d), F32),
        scratch_shapes=[pltpu.VMEM((tm, d), BF16)],
        compiler_params=_params("parallel", "arbitrary"),
        name="mlp",
    )(x, gain.reshape(1, d), mod, mod, mod, w1, w2)


def _proj_res_kernel(y_ref, w_ref, x_ref, gate_ref, o_ref):
    o_ref[...] = x_ref[...] + gate_ref[...] * jnp.dot(y_ref[...], w_ref[...], preferred_element_type=F32)


def _proj_res(y, w, layer, x, mod, gate_idx, seq):
    t, d = x.shape
    k = y.shape[1]
    tm = _tile(seq, 512)
    tn = _tile(d, 2048)
    per_b = seq // tm
    return pl.pallas_call(
        _proj_res_kernel,
        grid=(t // tm, d // tn),
        in_specs=[
            pl.BlockSpec((tm, k), lambda i, j: (i, 0)),
            pl.BlockSpec((None, k, tn), lambda i, j: (layer, 0, j)),
            pl.BlockSpec((tm, tn), lambda i, j: (i, j)),
            pl.BlockSpec((None, 1, tn), lambda i, j: (i // per_b, 0, gate_idx * (d // tn) + j)),
        ],
        out_specs=pl.BlockSpec((tm, tn), lambda i, j: (i, j)),
        out_shape=jax.ShapeDtypeStruct((t, d), F32),
        compiler_params=_params("parallel", "parallel"),
        name="proj_res",
    )(y, w, x, mod)


def _hg_in_kernel(x_ref, gain_ref, shift_ref, scale_ref, llb_ref, l1mlb_ref, omlb_ref,
                  wq_ref, wf_ref, wi_ref, wg_ref,
                  q_ref, k_ref, lf_ref, v_ref, sg_ref, h_ref, *, heads_per_step, q_scale):
    def project():
        h = h_ref[...]
        z = jnp.dot(h, wf_ref[...], preferred_element_type=F32)
        q = jnp.dot(h, wq_ref[...], preferred_element_type=F32) * q_scale
        v = jnp.dot(h, wi_ref[...], preferred_element_type=F32)
        g = jnp.dot(h, wg_ref[...], preferred_element_type=F32)

        e = jnp.exp(-jnp.abs(z))
        log_sig = jnp.minimum(z, 0.0) - jnp.log(1.0 + e)
        a = llb_ref[...]
        c = l1mlb_ref[...] + log_sig
        lf = jnp.maximum(a, c) + jnp.log(1.0 + jnp.exp(-jnp.abs(a - c)))
        key = omlb_ref[...] * (jnp.where(z >= 0.0, e, 1.0) / (1.0 + e))
        sg = g / (1.0 + jnp.exp(-g))

        for hh in range(heads_per_step):
            cs = slice(hh * LANES, (hh + 1) * LANES)
            q_ref[hh] = q[:, cs].astype(BF16)
            k_ref[hh] = key[:, cs].astype(BF16)
            lf_ref[hh] = lf[:, cs]
            v_ref[hh] = v[:, cs].astype(BF16)
            sg_ref[hh] = sg[:, cs].astype(BF16)

    @pl.when(pl.program_id(1) == 0)
    def _():
        h_ref[...] = _norm_mod(x_ref[...], gain_ref[...], scale_ref[...], shift_ref[...]).astype(BF16)
        project()

    @pl.when(pl.program_id(1) > 0)
    def _():
        project()


def _hg_in(x, gain, mod, lb, w_in, layer, bsz, seq):
    t, d = x.shape
    n_heads = d // LANES
    tm = _tile(seq, 1024)
    tn = _tile(d, 2 * LANES)
    hps = tn // LANES
    nj = d // tn
    per_b = seq // tm
    mod_spec = lambda k: pl.BlockSpec((None, 1, d), lambda i, j: (i // per_b, 0, k))
    vec_spec = pl.BlockSpec((1, tn), lambda i, j: (0, j))
    w_spec = lambda grp: pl.BlockSpec((None, d, tn), lambda i, j: (layer, 0, grp * nj + j))
    out_spec = pl.BlockSpec((None, hps, tm, LANES), lambda i, j: (i // per_b, j, i % per_b, 0))
    hm = lambda dt: jax.ShapeDtypeStruct((bsz, n_heads, seq, LANES), dt)
    lbf = lb.reshape(1, d)
    kern = functools.partial(_hg_in_kernel, heads_per_step=hps, q_scale=float(LANES) ** -0.5)
    return pl.pallas_call(
        kern,
        grid=(t // tm, nj),
        in_specs=[
            pl.BlockSpec((tm, d), lambda i, j: (i, 0)),
            pl.BlockSpec((1, d), lambda i, j: (0, 0)),
            mod_spec(0), mod_spec(1),
            vec_spec, vec_spec, vec_spec,
            w_spec(0), w_spec(1), w_spec(2), w_spec(3),
        ],
        out_specs=[out_spec] * 5,
        out_shape=[hm(BF16), hm(BF16), hm(F32), hm(BF16), hm(BF16)],
        scratch_shapes=[pltpu.VMEM((tm, d), BF16)],
        compiler_params=_params("parallel", "arbitrary"),
        name="hg_in",
    )(x, gain.reshape(1, d), mod, mod, jnp.log(lbf), jnp.log1p(-lbf), 1.0 - lbf, w_in, w_in, w_in, w_in)


def _nt_dot(a, b):
    return lax.dot_general(a, b, (((1,), (1,)), ((), ())), preferred_element_type=F32)


def _tn_dot(a, b):
    return lax.dot_general(a, b, (((0,), (0,)), ((), ())), preferred_element_type=F32)


def _chunk_scores(q, k, b, lf):
    row = lax.broadcasted_iota(jnp.int32, (CHUNK, LANES), 0)
    ti = lax.broadcasted_iota(jnp.int32, (CHUNK, CHUNK), 0)
    si = lax.broadcasted_iota(jnp.int32, (CHUNK, CHUNK), 1)
    scores = jnp.where(ti == si, _nt_dot(q.astype(BF16), k.astype(BF16)), 0.0)
    start = b - lf
    c = 1
    while c < CHUNK:
        later = (row & c) != 0
        bound = jnp.where(later, start, pltpu.roll(start, CHUNK - c, axis=0))
        e = jnp.exp(jnp.where(later, b - bound, bound - b))
        m = jnp.where(later, q, k) * e
        qs = jnp.where(later, m, 0.0).astype(BF16)
        ks = jnp.where(later, 0.0, m).astype(BF16)
        same_parent = (ti & -(2 * c)) == (si & -(2 * c))
        scores = scores + jnp.where(same_parent, _nt_dot(qs, ks), 0.0)
        if 2 * c < CHUNK:
            start = jnp.where(later, pltpu.roll(start, c, axis=0), start)
        c *= 2
    return scores


def _cumsum_rows(lf, tri):
    hi = lf.astype(BF16)
    r1 = lf - hi.astype(F32)
    mid = r1.astype(BF16)
    lo = (r1 - mid.astype(F32)).astype(BF16)
    dot = lambda part: jnp.dot(tri, part, preferred_element_type=F32)
    return dot(hi) + dot(mid) + dot(lo)


def _rec_kernel(q_ref, k_ref, lf_ref, v_ref, sg_ref, og_ref, o_ref, st_ref, *, n_chunks):
    @pl.when(pl.program_id(2) == 0)
    def _():
        st_ref[...] = jnp.zeros_like(st_ref)

    ti = lax.broadcasted_iota(jnp.int32, (CHUNK, CHUNK), 0)
    si = lax.broadcasted_iota(jnp.int32, (CHUNK, CHUNK), 1)
    causal = si <= ti
    tri = jnp.where(causal, 1.0, 0.0).astype(BF16)
    og = og_ref[...]

    def rows(ci):
        return slice(ci * CHUNK, (ci + 1) * CHUNK)

    mean_w = jnp.full((LANES, LANES), 1.0 / LANES, BF16)

    def finish(outs):
        ms = [jnp.dot((o * o).astype(BF16), mean_w, preferred_element_type=F32) for o in outs]
        for ci, o in enumerate(outs):
            o = o * lax.rsqrt(ms[ci] + EPS) * og
            o_ref[rows(ci), :] = (o * sg_ref[rows(ci), :].astype(F32)).astype(BF16)

    cs = range(n_chunks)
    b = [_cumsum_rows(lf_ref[rows(ci), :], tri) for ci in cs]
    total = b[0][CHUNK - 1:CHUNK, :]
    for ci in range(1, n_chunks):
        total = jnp.minimum(total, b[ci][CHUNK - 1:CHUNK, :])
    mild = jnp.min(total) >= -MAX_FACTORED_DECAY
    q_dec = [(q_ref[rows(ci), :].astype(F32) * jnp.exp(b[ci])).astype(BF16) for ci in cs]
    decay = [jnp.exp(b[ci][CHUNK - 1:CHUNK, :]) for ci in cs]
    k_grow = [k_ref[rows(ci), :].astype(F32) * jnp.exp(-b[ci]) for ci in cs]
    scores = [jnp.where(causal, jnp.dot(q_dec[ci], k_grow[ci].T.astype(BF16), preferred_element_type=F32),
                        0.0).astype(BF16) for ci in cs]
    intra = [jnp.dot(scores[ci], v_ref[rows(ci), :], preferred_element_type=F32) for ci in cs]

    @pl.when(mild)
    def _():
        update = [_tn_dot(v_ref[rows(ci), :], (k_grow[ci] * decay[ci]).astype(BF16)) for ci in cs]
        states = [st_ref[...]]
        for ci in cs:
            states.append(states[ci] * decay[ci] + update[ci])
        finish([intra[ci] + jnp.dot(q_dec[ci], states[ci].T.astype(BF16), preferred_element_type=F32)
                for ci in cs])
        st_ref[...] = states[n_chunks]

    @pl.when(jnp.logical_not(mild))
    def _():
        state_t = st_ref[...]
        outs = []
        for ci in range(n_chunks):
            lf = lf_ref[rows(ci), :]
            q = q_ref[rows(ci), :].astype(F32)
            k = k_ref[rows(ci), :].astype(F32)
            v = v_ref[rows(ci), :]
            scores = _chunk_scores(q, k, b[ci], lf)
            o = jnp.dot(scores.astype(BF16), v, preferred_element_type=F32)
            o = o + _nt_dot(q_dec[ci], state_t.astype(BF16))
            k_dec = (k * jnp.exp(b[ci][CHUNK - 1:CHUNK, :] - b[ci])).astype(BF16)
            state_t = state_t * decay[ci] + _tn_dot(v, k_dec)
            outs.append(o)
        finish(outs)
        st_ref[...] = state_t


def _hg_rec(q, k, lf, v, sg, o_gain):
    bsz, n_heads, seq, _ = q.shape
    ts = _tile(seq, 2048)
    in_spec = pl.BlockSpec((None, None, ts, LANES), lambda b, h, s: (b, h, s, 0))
    return pl.pallas_call(
        functools.partial(_rec_kernel, n_chunks=ts // CHUNK),
        grid=(bsz, n_heads, seq // ts),
        in_specs=[in_spec] * 5 + [pl.BlockSpec((1, LANES), lambda b, h, s: (0, 0))],
        out_specs=pl.BlockSpec((None, ts, LANES), lambda b, h, s: (b, s, h)),
        out_shape=jax.ShapeDtypeStruct((bsz, seq, n_heads * LANES), BF16),
        scratch_shapes=[pltpu.VMEM((LANES, LANES), F32)],
        compiler_params=_params("parallel", "parallel", "arbitrary"),
        name="hg_rec",
    )(q, k, lf, v, sg, o_gain.reshape(1, LANES))


def _pair_sum_weights():
    row = lax.broadcasted_iota(jnp.int32, (QK_PAD, QK_PAD), 0)
    col = lax.broadcasted_iota(jnp.int32, (QK_PAD, QK_PAD), 1)
    return jnp.where((row < LANES) == (col < LANES), 1.0, 0.0).astype(BF16)


def _rope_pair(u):
    return u + pltpu.roll(u, ROPE, axis=1)


def _mla_kv_kernel(x_ref, gain_ref, shift_ref, scale_ref, wd_ref, lat_gain_ref, wu_ref,
                   gn_ref, gr_ref, cs_ref, k_ref, v_ref, *, n_heads, kv_lora):
    h = _norm_mod(x_ref[...], gain_ref[...], scale_ref[...], shift_ref[...]).astype(BF16)
    ckv = jnp.dot(h, wd_ref[...], preferred_element_type=F32)
    lat = ckv[:, :kv_lora]
    pe2 = ckv[:, kv_lora:]
    c_lat = (lat * lax.rsqrt(jnp.mean(lat * lat, axis=-1, keepdims=True) + EPS) * lat_gain_ref[...]).astype(BF16)
    pair_w = _pair_sum_weights()
    sq_pe = 0.5 * (pe2 * pe2)
    rot = _rope_pair(pe2 * gr_ref[...] * cs_ref[...])
    gn = gn_ref[...]

    def pair_kv(p):
        return [jnp.dot(c_lat, wu_ref[:, hh * QK_PAD:(hh + 1) * QK_PAD], preferred_element_type=F32)
                for hh in (2 * p, 2 * p + 1)]

    nxt = pair_kv(0)
    for p in range(n_heads // 2):
        kvs = nxt
        if p + 1 < n_heads // 2:
            nxt = pair_kv(p + 1)
        sq = jnp.concatenate([kv[:, :LANES] * kv[:, :LANES] + sq_pe for kv in kvs], axis=1)
        r2 = lax.rsqrt(jnp.dot(sq.astype(BF16), pair_w, preferred_element_type=F32) * (1.0 / QK) + EPS)
        for idx, kv in enumerate(kvs):
            hh = 2 * p + idx
            r = r2[:, idx * LANES:(idx + 1) * LANES]
            k_ref[hh, :, :LANES] = (kv[:, :LANES] * r * gn).astype(BF16)
            k_ref[hh, :, LANES:] = (rot * r).astype(BF16)
            v_ref[hh] = kv[:, LANES:].T.astype(BF16)


def _mla_kv(x, gain, mod, wd, lat_gain, wu, gn, gr, cs, bsz, seq):
    t, d = x.shape
    n_heads = d // LANES
    kv_lora = lat_gain.shape[0]
    tm = _tile(seq, ATTN_TILE)
    per_b = seq // tm
    mod_spec = lambda k: pl.BlockSpec((None, 1, d), lambda i: (i // per_b, 0, k))
    assert n_heads % 2 == 0, n_heads
    full = lambda a: pl.BlockSpec(a.shape, lambda i: (0,) * a.ndim)
    lat_gain = lat_gain.reshape(1, kv_lora)
    return pl.pallas_call(
        functools.partial(_mla_kv_kernel, n_heads=n_heads, kv_lora=kv_lora),
        grid=(t // tm,),
        in_specs=[
            pl.BlockSpec((tm, d), lambda i: (i, 0)),
            pl.BlockSpec((1, d), lambda i: (0, 0)),
            mod_spec(0), mod_spec(1),
            full(wd), full(lat_gain), full(wu), full(gn), full(gr),
            pl.BlockSpec((tm, LANES), lambda i: (i, 0)),
        ],
        out_specs=[
            pl.BlockSpec((None, n_heads, tm, QK_PAD), lambda i: (i // per_b, 0, i % per_b, 0)),
            pl.BlockSpec((None, n_heads, None, LANES, tm), lambda i: (i // per_b, 0, i % per_b, 0, 0)),
        ],
        out_shape=[jax.ShapeDtypeStruct((bsz, n_heads, seq, QK_PAD), BF16),
                   jax.ShapeDtypeStruct((bsz, n_heads, seq // tm, LANES, tm), BF16)],
        compiler_params=_params("parallel"),
        name="mla_kv",
    )(x, gain.reshape(1, d), mod, mod, wd, lat_gain, wu, gn, gr, cs)


def _mla_q_kernel(x_ref, gain_ref, shift_ref, scale_ref, wd_ref, lat_gain_ref, wu_ref,
                  gn_ref, gr_ref, cs_ref, q_ref, *, n_heads):
    h = _norm_mod(x_ref[...], gain_ref[...], scale_ref[...], shift_ref[...]).astype(BF16)
    cq = jnp.dot(h, wd_ref[...], preferred_element_type=F32)
    cq = (cq * lax.rsqrt(jnp.mean(cq * cq, axis=-1, keepdims=True) + EPS) * lat_gain_ref[...]).astype(BF16)
    gn = gn_ref[...]
    rope_tab = gr_ref[...] * cs_ref[...]
    pair_w = _pair_sum_weights()

    def pair_q(p):
        return [jnp.dot(cq, wu_ref[:, hh * QK_PAD:(hh + 1) * QK_PAD], preferred_element_type=F32)
                for hh in (2 * p, 2 * p + 1)]

    nxt = pair_q(0)
    for p in range(n_heads // 2):
        qhs = nxt
        if p + 1 < n_heads // 2:
            nxt = pair_q(p + 1)
        sq = jnp.concatenate([qh[:, :LANES] * qh[:, :LANES] + 0.5 * (qh[:, LANES:] * qh[:, LANES:])
                              for qh in qhs], axis=1)
        r2 = lax.rsqrt(jnp.dot(sq.astype(BF16), pair_w, preferred_element_type=F32) * (1.0 / QK) + EPS)
        for idx, qh in enumerate(qhs):
            hh = 2 * p + idx
            r = r2[:, idx * LANES:(idx + 1) * LANES]
            q_ref[hh, :LANES, :] = (qh[:, :LANES] * r * gn).T.astype(BF16)
            q_ref[hh, LANES:, :] = (qh[:, LANES:] * rope_tab * r).T.astype(BF16)


def _mla_q(x, gain, mod, wd, layer, lat_gain, wu, gn, gr, cs, bsz, seq):
    t, d = x.shape
    n_heads = d // LANES
    q_lora = lat_gain.shape[0]
    tm = _tile(seq, ATTN_TILE)
    per_b = seq // tm
    mod_spec = lambda k: pl.BlockSpec((None, 1, d), lambda i: (i // per_b, 0, k))
    assert n_heads % 2 == 0, n_heads
    full = lambda a: pl.BlockSpec(a.shape, lambda i: (0,) * a.ndim)
    lat_gain = lat_gain.reshape(1, q_lora)
    return pl.pallas_call(
        functools.partial(_mla_q_kernel, n_heads=n_heads),
        grid=(t // tm,),
        in_specs=[
            pl.BlockSpec((tm, d), lambda i: (i, 0)),
            pl.BlockSpec((1, d), lambda i: (0, 0)),
            mod_spec(0), mod_spec(1),
            pl.BlockSpec((None, d, q_lora), lambda i: (layer, 0, 0)),
            full(lat_gain), full(wu), full(gn), full(gr),
            pl.BlockSpec((tm, LANES), lambda i: (i, 0)),
        ],
        out_specs=pl.BlockSpec((None, n_heads, None, QK_PAD, tm), lambda i: (i // per_b, 0, i % per_b, 0, 0)),
        out_shape=jax.ShapeDtypeStruct((bsz, n_heads, seq // tm, QK_PAD, tm), BF16),
        compiler_params=_params("parallel"),
        name="mla_q",
    )(x, gain.reshape(1, d), mod, mod, wd, lat_gain, wu, gn, gr, cs)


def _flash_kernel(qt_ref, k_ref, vt_ref, o_ref, s_ref, m_ref, l_ref, acc_ref, *, tq, n_tiles, heads_per_step):
    def tile_rows(j):
        return pl.ds(pl.multiple_of(j * tq, tq), tq)

    def scores(hh, i, j):
        return jnp.dot(k_ref[hh, tile_rows(j), :], qt_ref[hh, i], preferred_element_type=F32)

    def consume(hh, j, masked):
        s = s_ref[hh]
        if masked:
            ki = lax.broadcasted_iota(jnp.int32, (tq, tq), 0)
            qi = lax.broadcasted_iota(jnp.int32, (tq, tq), 1)
            s = jnp.where(ki <= qi, s, -jnp.inf)
        m = m_ref[hh]
        m_new = jnp.maximum(m, jnp.max(s, axis=0, keepdims=True))
        alpha = jnp.exp2(m - m_new)
        p = jnp.exp2(s - m_new)
        m_ref[hh] = m_new
        l_ref[hh] = alpha * l_ref[hh] + jnp.sum(p, axis=0, keepdims=True)
        pv = jnp.dot(vt_ref[hh, j], p.astype(BF16), preferred_element_type=F32)
        acc_ref[hh] = alpha * acc_ref[hh] + pv

    for hh in range(heads_per_step):
        s_ref[hh] = scores(hh, 0, 0)

    def query_tile(i, carry):
        m_ref[...] = jnp.full(m_ref.shape, -jnp.inf, F32)
        l_ref[...] = jnp.zeros_like(l_ref)
        acc_ref[...] = jnp.zeros_like(acc_ref)

        def key_tile(j, c):
            for hh in range(heads_per_step):
                s_next = scores(hh, i, j + 1)
                consume(hh, j, False)
                s_ref[hh] = s_next
            return c

        lax.fori_loop(0, i, key_tile, 0)
        i_next = jnp.minimum(i + 1, n_tiles - 1)
        for hh in range(heads_per_step):
            s_next = scores(hh, i_next, 0)
            consume(hh, i, True)
            s_ref[hh] = s_next
            o_ref[tile_rows(i), hh * LANES:(hh + 1) * LANES] = (acc_ref[hh] / l_ref[hh]).T.astype(BF16)
        return carry

    lax.fori_loop(0, n_tiles, query_tile, 0)


def _flash(qt, k, vt):
    bsz, n_heads, seq, _ = k.shape
    tq = vt.shape[-1]
    hps = 4 if n_heads % 4 == 0 else (2 if n_heads % 2 == 0 else 1)
    return pl.pallas_call(
        functools.partial(_flash_kernel, tq=tq, n_tiles=seq // tq, heads_per_step=hps),
        grid=(bsz, n_heads // hps),
        in_specs=[
            pl.BlockSpec((None, hps, seq // tq, QK_PAD, tq), lambda b, h: (b, h, 0, 0, 0)),
            pl.BlockSpec((None, hps, seq, QK_PAD), lambda b, h: (b, h, 0, 0)),
            pl.BlockSpec((None, hps, seq // tq, LANES, tq), lambda b, h: (b, h, 0, 0, 0)),
        ],
        out_specs=pl.BlockSpec((None, seq, hps * LANES), lambda b, h: (b, 0, h)),
        out_shape=jax.ShapeDtypeStruct((bsz, seq, n_heads * LANES), BF16),
        scratch_shapes=[pltpu.VMEM((hps, tq, tq), F32),
                        pltpu.VMEM((hps, 1, tq), F32), pltpu.VMEM((hps, 1, tq), F32),
                        pltpu.VMEM((hps, LANES, tq), F32)],
        compiler_params=_params("parallel", "parallel"),
        name="flash",
    )(qt, k, vt)


def _swap_halves(a):
    half = a.shape[-1] // 2
    return jnp.concatenate([a[..., half:], a[..., :half]], axis=-1)


def _rope_gain(g):
    return jnp.concatenate([g, _swap_halves(g)]).reshape(1, LANES)


def kernel(x, c, positions, ada_w, ada_b, norm_mix, norm_mlp, mlp_w1, mlp_w2, hg_w_in, hg_lower, hg_o_norm, hg_w_out, kv_ada_w, kv_ada_b, kv_in_norm, mla_w_dkv, mla_kv_norm, mla_w_ukv, mla_k_norm, mla_w_dq, mla_q_lat_norm, mla_w_uq, mla_q_norm, mla_w_o):
    bsz, seq, d = x.shape
    depth = ada_w.shape[0]
    n_a = hg_w_in.shape[0]
    n_heads = d // LANES
    kv_lora = mla_kv_norm.shape[0]
    q_lora = mla_w_dq.shape[2]
    t = bsz * seq

    mod = _ada(c, ada_w, ada_b).reshape(depth, bsz, 1, 6 * d)
    kv_mod = _ada(c, kv_ada_w[None], kv_ada_b[None]).reshape(bsz, 1, 2 * d)

    inv_freq = 1.0 / (ROPE_THETA ** (jnp.arange(0, ROPE, 2, dtype=F32) / ROPE))
    ang = positions.astype(F32)[..., None] * inv_freq
    cos, sin = jnp.cos(ang), jnp.sin(ang)
    cs = jnp.concatenate([cos, cos, -sin, sin], axis=-1).reshape(t, LANES)

    lb_all = jnp.cumsum(jax.nn.softmax(hg_lower.astype(F32), axis=0), axis=0)[:n_a]

    hg_w_in, hg_w_out, mla_w_dq, mla_w_o, mlp_w1, mlp_w2 = (
        w.astype(BF16) for w in (hg_w_in, hg_w_out, mla_w_dq, mla_w_o, mlp_w1, mlp_w2))

    xf = x.reshape(t, d)
    k_sh = v_sh = None
    for layer in range(depth):
        if layer == n_a:
            wd = jnp.concatenate([mla_w_dkv, _swap_halves(mla_w_dkv[:, kv_lora:])], axis=1).astype(BF16)
            k_sh, v_sh = _mla_kv(xf, kv_in_norm, kv_mod, wd, mla_kv_norm, mla_w_ukv.astype(BF16),
                                 mla_k_norm[:LANES].reshape(1, LANES), _rope_gain(mla_k_norm[LANES:]),
                                 cs, bsz, seq)
        if layer < n_a:
            q, k, lf, v, sg = _hg_in(xf, norm_mix[layer], mod[layer], lb_all[layer],
                                     hg_w_in, layer, bsz, seq)
            y = _hg_rec(q, k, lf, v, sg, hg_o_norm[layer]).reshape(t, d)
            xf = _proj_res(y, hg_w_out, layer, xf, mod[layer], 2, seq)
        else:
            j = layer - n_a
            wu = mla_w_uq[j].reshape(q_lora, n_heads, QK)
            wu = jnp.concatenate([wu, _swap_halves(wu[..., LANES:])], axis=-1)
            wu = wu.reshape(q_lora, n_heads * QK_PAD).astype(BF16)
            sm_scale = float(QK) ** -0.5 * LOG2E
            qg = mla_q_norm[j] * sm_scale
            qh = _mla_q(xf, norm_mix[layer], mod[layer], mla_w_dq, j, mla_q_lat_norm[j], wu,
                        qg[:LANES].reshape(1, LANES), _rope_gain(qg[LANES:]), cs, bsz, seq)
            y = _flash(qh, k_sh, v_sh).reshape(t, d)
            xf = _proj_res(y, mla_w_o, j, xf, mod[layer], 2, seq)
        xf = _mlp(xf, norm_mlp[layer], mod[layer], mlp_w1, mlp_w2, layer, seq)
    return xf.reshape(bsz, seq, d)
```

```python
import functools

import jax
import jax.numpy as jnp
from jax import lax
from jax.experimental import pallas as pl
from jax.experimental.pallas import tpu as pltpu

F32 = jnp.float32
BF16 = jnp.bfloat16

EPS = 1e-6
LANES = 128
ROPE = 64
QK = LANES + ROPE
QK_PAD = 2 * LANES
CHUNK = 64
ATTN_TILE = 512
HG_COL_TILE = 2 * LANES
MLP_HIDDEN_TILE = 512
MAX_FACTORED_DECAY = 72.0
ROPE_THETA = 10000.0
LOG2E = 1.4426950408889634
VMEM_LIMIT = 56 * 1024 * 1024


def _params(*sem):
    return pltpu.CompilerParams(dimension_semantics=sem, vmem_limit_bytes=VMEM_LIMIT)


def _tile(n, pref):
    t = min(n, pref)
    while n % t:
        t -= LANES
    assert t > 0 and n % t == 0, (n, pref)
    return t


def _norm_mod(x, gain, scale, shift):
    r = lax.rsqrt(jnp.mean(x * x, axis=-1, keepdims=True) + EPS)
    return (x * r) * (gain * (1.0 + scale)) + shift


def _ada_kernel(c_ref, w_ref, b_ref, o_ref):
    c = c_ref[...]
    cond = (c / (1.0 + jnp.exp(-c))).astype(BF16)
    o_ref[...] = jnp.dot(cond, w_ref[...].astype(BF16), preferred_element_type=F32) + b_ref[...]


def _ada(c, w, b):
    n_layers, d, n = w.shape
    bsz = c.shape[0]
    tn = _tile(n, 1024)
    return pl.pallas_call(
        _ada_kernel,
        grid=(n_layers, n // tn),
        in_specs=[
            pl.BlockSpec((bsz, d), lambda l, j: (0, 0)),
            pl.BlockSpec((None, d, tn), lambda l, j: (l, 0, j)),
            pl.BlockSpec((None, 1, tn), lambda l, j: (l, 0, j)),
        ],
        out_specs=pl.BlockSpec((None, bsz, tn), lambda l, j: (l, 0, j)),
        out_shape=jax.ShapeDtypeStruct((n_layers, bsz, n), F32),
        compiler_params=_params("parallel", "parallel"),
        name="ada",
    )(c, w, b.reshape(n_layers, 1, n))


def _mlp_kernel(x_ref, gain_ref, shift_ref, scale_ref, gate_ref, w1_ref, w2_ref, o_ref, h_ref):
    j = pl.program_id(1)

    def gated_chunk():
        a = jnp.maximum(jnp.dot(h_ref[...], w1_ref[...], preferred_element_type=F32), 0.0)
        return gate_ref[...] * jnp.dot((a * a).astype(BF16), w2_ref[...], preferred_element_type=F32)

    @pl.when(j == 0)
    def _():
        h_ref[...] = _norm_mod(x_ref[...], gain_ref[...], scale_ref[...], shift_ref[...]).astype(BF16)
        o_ref[...] = x_ref[...] + gated_chunk()

    @pl.when(j > 0)
    def _():
        o_ref[...] += gated_chunk()


def _mlp(x, gain, mod, w1, w2, layer, seq):
    t, d = x.shape
    tf = w1.shape[3]
    f = w1.shape[1] * tf
    tm = _tile(seq, 1024)
    per_b = seq // tm
    mod_spec = lambda k: pl.BlockSpec((None, 1, d), lambda i, j: (i // per_b, 0, k))
    return pl.pallas_call(
        _mlp_kernel,
        grid=(t // tm, f // tf),
        in_specs=[
            pl.BlockSpec((tm, d), lambda i, j: (i, 0)),
            pl.BlockSpec((1, d), lambda i, j: (0, 0)),
            mod_spec(3), mod_spec(4), mod_spec(5),
            pl.BlockSpec((None, None, d, tf), lambda i, j: (layer, j, 0, 0)),
            pl.BlockSpec((None, tf, d), lambda i, j: (layer, j, 0)),
        ],
        out_specs=pl.BlockSpec((tm, d), lambda i, j: (i, 0)),
        out_shape=jax.ShapeDtypeStruct((t, d), F32),
        scratch_shapes=[pltpu.VMEM((tm, d), BF16)],
        compiler_params=_params("parallel", "arbitrary"),
        name="mlp",
    )(x, gain.reshape(1, d), mod, mod, mod, w1, w2)


def _proj_res_kernel(y_ref, w_ref, x_ref, gate_ref, o_ref):
    o_ref[...] = x_ref[...] + gate_ref[...] * jnp.dot(y_ref[...], w_ref[...], preferred_element_type=F32)


def _proj_res(y, w, layer, x, mod, gate_idx, seq):
    t, d = x.shape
    k = y.shape[1]
    tm = _tile(seq, 512)
    tn = _tile(d, 2048)
    per_b = seq // tm
    return pl.pallas_call(
        _proj_res_kernel,
        grid=(t // tm, d // tn),
        in_specs=[
            pl.BlockSpec((tm, k), lambda i, j: (i, 0)),
            pl.BlockSpec((None, k, tn), lambda i, j: (layer, 0, j)),
            pl.BlockSpec((tm, tn), lambda i, j: (i, j)),
            pl.BlockSpec((None, 1, tn), lambda i, j: (i // per_b, 0, gate_idx * (d // tn) + j)),
        ],
        out_specs=pl.BlockSpec((tm, tn), lambda i, j: (i, j)),
        out_shape=jax.ShapeDtypeStruct((t, d), F32),
        compiler_params=_params("parallel", "parallel"),
        name="proj_res",
    )(y, w, x, mod)


def _hg_in_kernel(x_ref, gain_ref, shift_ref, scale_ref, llb_ref, l1mlb_ref, omlb_ref, w_ref,
                  q_ref, k_ref, lf_ref, v_ref, sg_ref, h_ref, *, heads_per_step, q_scale):
    def project():
        h = h_ref[...]
        z = jnp.dot(h, w_ref[1], preferred_element_type=F32)
        q = jnp.dot(h, w_ref[0], preferred_element_type=F32) * q_scale
        v = jnp.dot(h, w_ref[2], preferred_element_type=F32)
        g = jnp.dot(h, w_ref[3], preferred_element_type=F32)

        e = jnp.exp(-jnp.abs(z))
        log_sig = jnp.minimum(z, 0.0) - jnp.log(1.0 + e)
        a = llb_ref[...]
        c = l1mlb_ref[...] + log_sig
        lf = jnp.maximum(a, c) + jnp.log(1.0 + jnp.exp(-jnp.abs(a - c)))
        key = omlb_ref[...] * (jnp.where(z >= 0.0, e, 1.0) / (1.0 + e))
        sg = g / (1.0 + jnp.exp(-g))

        for hh in range(heads_per_step):
            cs = slice(hh * LANES, (hh + 1) * LANES)
            q_ref[hh] = q[:, cs].astype(BF16)
            k_ref[hh] = key[:, cs].astype(BF16)
            lf_ref[hh] = lf[:, cs]
            v_ref[hh] = v[:, cs].astype(BF16)
            sg_ref[hh] = sg[:, cs].astype(BF16)

    @pl.when(pl.program_id(1) == 0)
    def _():
        h_ref[...] = _norm_mod(x_ref[...], gain_ref[...], scale_ref[...], shift_ref[...]).astype(BF16)
        project()

    @pl.when(pl.program_id(1) > 0)
    def _():
        project()


def _hg_in(x, gain, mod, lb, w_in, layer, bsz, seq):
    t, d = x.shape
    n_heads = d // LANES
    tm = _tile(seq, 1024)
    nj, tn = w_in.shape[1], w_in.shape[4]
    hps = tn // LANES
    per_b = seq // tm
    mod_spec = lambda k: pl.BlockSpec((None, 1, d), lambda i, j: (i // per_b, 0, k))
    vec_spec = pl.BlockSpec((1, tn), lambda i, j: (0, j))
    out_spec = pl.BlockSpec((None, hps, tm, LANES), lambda i, j: (i // per_b, j, i % per_b, 0))
    hm = lambda dt: jax.ShapeDtypeStruct((bsz, n_heads, seq, LANES), dt)
    lbf = lb.reshape(1, d)
    kern = functools.partial(_hg_in_kernel, heads_per_step=hps, q_scale=float(LANES) ** -0.5)
    return pl.pallas_call(
        kern,
        grid=(t // tm, nj),
        in_specs=[
            pl.BlockSpec((tm, d), lambda i, j: (i, 0)),
            pl.BlockSpec((1, d), lambda i, j: (0, 0)),
            mod_spec(0), mod_spec(1),
            vec_spec, vec_spec, vec_spec,
            pl.BlockSpec((None, None, 4, d, tn), lambda i, j: (layer, j, 0, 0, 0)),
        ],
        out_specs=[out_spec] * 5,
        out_shape=[hm(BF16), hm(BF16), hm(F32), hm(BF16), hm(BF16)],
        scratch_shapes=[pltpu.VMEM((tm, d), BF16)],
        compiler_params=_params("parallel", "arbitrary"),
        name="hg_in",
    )(x, gain.reshape(1, d), mod, mod, jnp.log(lbf), jnp.log1p(-lbf), 1.0 - lbf, w_in)


def _nt_dot(a, b):
    return lax.dot_general(a, b, (((1,), (1,)), ((), ())), preferred_element_type=F32)


def _tn_dot(a, b):
    return lax.dot_general(a, b, (((0,), (0,)), ((), ())), preferred_element_type=F32)


def _chunk_scores(q, k, b, lf):
    row = lax.broadcasted_iota(jnp.int32, (CHUNK, LANES), 0)
    ti = lax.broadcasted_iota(jnp.int32, (CHUNK, CHUNK), 0)
    si = lax.broadcasted_iota(jnp.int32, (CHUNK, CHUNK), 1)
    scores = jnp.where(ti == si, _nt_dot(q.astype(BF16), k.astype(BF16)), 0.0)
    start = b - lf
    c = 1
    while c < CHUNK:
        later = (row & c) != 0
        bound = jnp.where(later, start, pltpu.roll(start, CHUNK - c, axis=0))
        e = jnp.exp(jnp.where(later, b - bound, bound - b))
        m = jnp.where(later, q, k) * e
        qs = jnp.where(later, m, 0.0).astype(BF16)
        ks = jnp.where(later, 0.0, m).astype(BF16)
        same_parent = (ti & -(2 * c)) == (si & -(2 * c))
        scores = scores + jnp.where(same_parent, _nt_dot(qs, ks), 0.0)
        if 2 * c < CHUNK:
            start = jnp.where(later, pltpu.roll(start, c, axis=0), start)
        c *= 2
    return scores


def _cumsum_rows(lf, tri):
    hi = lf.astype(BF16)
    r1 = lf - hi.astype(F32)
    mid = r1.astype(BF16)
    lo = (r1 - mid.astype(F32)).astype(BF16)
    dot = lambda part: jnp.dot(tri, part, preferred_element_type=F32)
    return dot(hi) + dot(mid) + dot(lo)


def _rec_kernel(q_ref, k_ref, lf_ref, v_ref, sg_ref, og_ref, o_ref, st_ref, *, n_chunks):
    @pl.when(pl.program_id(2) == 0)
    def _():
        st_ref[...] = jnp.zeros_like(st_ref)

    ti = lax.broadcasted_iota(jnp.int32, (CHUNK, CHUNK), 0)
    si = lax.broadcasted_iota(jnp.int32, (CHUNK, CHUNK), 1)
    causal = si <= ti
    tri = jnp.where(causal, 1.0, 0.0).astype(BF16)
    og = og_ref[...]

    def rows(ci):
        return slice(ci * CHUNK, (ci + 1) * CHUNK)

    mean_w = jnp.full((LANES, LANES), 1.0 / LANES, BF16)

    def finish(outs):
        ms = [jnp.dot((o * o).astype(BF16), mean_w, preferred_element_type=F32) for o in outs]
        for ci, o in enumerate(outs):
            o = o * lax.rsqrt(ms[ci] + EPS) * og
            o_ref[rows(ci), :] = (o * sg_ref[rows(ci), :].astype(F32)).astype(BF16)

    cs = range(n_chunks)
    b = [_cumsum_rows(lf_ref[rows(ci), :], tri) for ci in cs]
    total = b[0][CHUNK - 1:CHUNK, :]
    for ci in range(1, n_chunks):
        total = jnp.minimum(total, b[ci][CHUNK - 1:CHUNK, :])
    mild = jnp.min(total) >= -MAX_FACTORED_DECAY
    q_dec = [(q_ref[rows(ci), :].astype(F32) * jnp.exp(b[ci])).astype(BF16) for ci in cs]
    decay = [jnp.exp(b[ci][CHUNK - 1:CHUNK, :]) for ci in cs]
    k_grow = [k_ref[rows(ci), :].astype(F32) * jnp.exp(-b[ci]) for ci in cs]
    scores = [jnp.where(causal, jnp.dot(q_dec[ci], k_grow[ci].T.astype(BF16), preferred_element_type=F32),
                        0.0).astype(BF16) for ci in cs]
    intra = [jnp.dot(scores[ci], v_ref[rows(ci), :], preferred_element_type=F32) for ci in cs]

    @pl.when(mild)
    def _():
        update = [_tn_dot(v_ref[rows(ci), :], (k_grow[ci] * decay[ci]).astype(BF16)) for ci in cs]
        states = [st_ref[...]]
        for ci in cs:
            states.append(states[ci] * decay[ci] + update[ci])
        finish([intra[ci] + jnp.dot(q_dec[ci], states[ci].T.astype(BF16), preferred_element_type=F32)
                for ci in cs])
        st_ref[...] = states[n_chunks]

    @pl.when(jnp.logical_not(mild))
    def _():
        state_t = st_ref[...]
        outs = []
        for ci in range(n_chunks):
            lf = lf_ref[rows(ci), :]
            q = q_ref[rows(ci), :].astype(F32)
            k = k_ref[rows(ci), :].astype(F32)
            v = v_ref[rows(ci), :]
            scores = _chunk_scores(q, k, b[ci], lf)
            o = jnp.dot(scores.astype(BF16), v, preferred_element_type=F32)
            o = o + _nt_dot(q_dec[ci], state_t.astype(BF16))
            k_dec = (k * jnp.exp(b[ci][CHUNK - 1:CHUNK, :] - b[ci])).astype(BF16)
            state_t = state_t * decay[ci] + _tn_dot(v, k_dec)
            outs.append(o)
        finish(outs)
        st_ref[...] = state_t


def _hg_rec(q, k, lf, v, sg, o_gain):
    bsz, n_heads, seq, _ = q.shape
    ts = _tile(seq, 2048)
    in_spec = pl.BlockSpec((None, None, ts, LANES), lambda b, h, s: (b, h, s, 0))
    return pl.pallas_call(
        functools.partial(_rec_kernel, n_chunks=ts // CHUNK),
        grid=(bsz, n_heads, seq // ts),
        in_specs=[in_spec] * 5 + [pl.BlockSpec((1, LANES), lambda b, h, s: (0, 0))],
        out_specs=pl.BlockSpec((None, ts, LANES), lambda b, h, s: (b, s, h)),
        out_shape=jax.ShapeDtypeStruct((bsz, seq, n_heads * LANES), BF16),
        scratch_shapes=[pltpu.VMEM((LANES, LANES), F32)],
        compiler_params=_params("parallel", "parallel", "arbitrary"),
        name="hg_rec",
    )(q, k, lf, v, sg, o_gain.reshape(1, LANES))


def _pair_sum_weights():
    row = lax.broadcasted_iota(jnp.int32, (QK_PAD, QK_PAD), 0)
    col = lax.broadcasted_iota(jnp.int32, (QK_PAD, QK_PAD), 1)
    return jnp.where((row < LANES) == (col < LANES), 1.0, 0.0).astype(BF16)


def _rope_pair(u):
    return u + pltpu.roll(u, ROPE, axis=1)


def _mla_kv_kernel(x_ref, gain_ref, shift_ref, scale_ref, wd_ref, lat_gain_ref, wu_ref,
                   gn_ref, gr_ref, cs_ref, k_ref, v_ref, *, n_heads, kv_lora):
    h = _norm_mod(x_ref[...], gain_ref[...], scale_ref[...], shift_ref[...]).astype(BF16)
    ckv = jnp.dot(h, wd_ref[...], preferred_element_type=F32)
    lat = ckv[:, :kv_lora]
    pe2 = ckv[:, kv_lora:]
    c_lat = (lat * lax.rsqrt(jnp.mean(lat * lat, axis=-1, keepdims=True) + EPS) * lat_gain_ref[...]).astype(BF16)
    pair_w = _pair_sum_weights()
    sq_pe = 0.5 * (pe2 * pe2)
    rot = _rope_pair(pe2 * gr_ref[...] * cs_ref[...])
    gn = gn_ref[...]

    def pair_kv(p):
        return [jnp.dot(c_lat, wu_ref[:, hh * QK_PAD:(hh + 1) * QK_PAD], preferred_element_type=F32)
                for hh in (2 * p, 2 * p + 1)]

    nxt = pair_kv(0)
    for p in range(n_heads // 2):
        kvs = nxt
        if p + 1 < n_heads // 2:
            nxt = pair_kv(p + 1)
        sq = jnp.concatenate([kv[:, :LANES] * kv[:, :LANES] + sq_pe for kv in kvs], axis=1)
        r2 = lax.rsqrt(jnp.dot(sq.astype(BF16), pair_w, preferred_element_type=F32) * (1.0 / QK) + EPS)
        for idx, kv in enumerate(kvs):
            hh = 2 * p + idx
            r = r2[:, idx * LANES:(idx + 1) * LANES]
            k_ref[hh, :, :LANES] = (kv[:, :LANES] * r * gn).astype(BF16)
            k_ref[hh, :, LANES:] = (rot * r).astype(BF16)
            v_ref[hh] = kv[:, LANES:].T.astype(BF16)


def _mla_kv(x, gain, mod, wd, lat_gain, wu, gn, gr, cs, bsz, seq):
    t, d = x.shape
    n_heads = d // LANES
    kv_lora = lat_gain.shape[0]
    tm = _tile(seq, ATTN_TILE)
    per_b = seq // tm
    mod_spec = lambda k: pl.BlockSpec((None, 1, d), lambda i: (i // per_b, 0, k))
    assert n_heads % 2 == 0, n_heads
    full = lambda a: pl.BlockSpec(a.shape, lambda i: (0,) * a.ndim)
    lat_gain = lat_gain.reshape(1, kv_lora)
    return pl.pallas_call(
        functools.partial(_mla_kv_kernel, n_heads=n_heads, kv_lora=kv_lora),
        grid=(t // tm,),
        in_specs=[
            pl.BlockSpec((tm, d), lambda i: (i, 0)),
            pl.BlockSpec((1, d), lambda i: (0, 0)),
            mod_spec(0), mod_spec(1),
            full(wd), full(lat_gain), full(wu), full(gn), full(gr),
            pl.BlockSpec((tm, LANES), lambda i: (i, 0)),
        ],
        out_specs=[
            pl.BlockSpec((None, n_heads, tm, QK_PAD), lambda i: (i // per_b, 0, i % per_b, 0)),
            pl.BlockSpec((None, n_heads, None, LANES, tm), lambda i: (i // per_b, 0, i % per_b, 0, 0)),
        ],
        out_shape=[jax.ShapeDtypeStruct((bsz, n_heads, seq, QK_PAD), BF16),
                   jax.ShapeDtypeStruct((bsz, n_heads, seq // tm, LANES, tm), BF16)],
        compiler_params=_params("parallel"),
        name="mla_kv",
    )(x, gain.reshape(1, d), mod, mod, wd, lat_gain, wu, gn, gr, cs)


def _mla_q_kernel(x_ref, gain_ref, shift_ref, scale_ref, wd_ref, lat_gain_ref, wu_ref,
                  gn_ref, gr_ref, cs_ref, q_ref, *, n_heads):
    h = _norm_mod(x_ref[...], gain_ref[...], scale_ref[...], shift_ref[...]).astype(BF16)
    cq = jnp.dot(h, wd_ref[...], preferred_element_type=F32)
    cq = (cq * lax.rsqrt(jnp.mean(cq * cq, axis=-1, keepdims=True) + EPS) * lat_gain_ref[...]).astype(BF16)
    gn = gn_ref[...]
    rope_tab = gr_ref[...] * cs_ref[...]
    pair_w = _pair_sum_weights()

    def pair_q(p):
        return [jnp.dot(cq, wu_ref[:, hh * QK_PAD:(hh + 1) * QK_PAD], preferred_element_type=F32)
                for hh in (2 * p, 2 * p + 1)]

    nxt = pair_q(0)
    for p in range(n_heads // 2):
        qhs = nxt
        if p + 1 < n_heads // 2:
            nxt = pair_q(p + 1)
        sq = jnp.concatenate([qh[:, :LANES] * qh[:, :LANES] + 0.5 * (qh[:, LANES:] * qh[:, LANES:])
                              for qh in qhs], axis=1)
        r2 = lax.rsqrt(jnp.dot(sq.astype(BF16), pair_w, preferred_element_type=F32) * (1.0 / QK) + EPS)
        for idx, qh in enumerate(qhs):
            hh = 2 * p + idx
            r = r2[:, idx * LANES:(idx + 1) * LANES]
            q_ref[hh, :LANES, :] = (qh[:, :LANES] * r * gn).T.astype(BF16)
            q_ref[hh, LANES:, :] = (qh[:, LANES:] * rope_tab * r).T.astype(BF16)


def _mla_q(x, gain, mod, wd, layer, lat_gain, wu, gn, gr, cs, bsz, seq):
    t, d = x.shape
    n_heads = d // LANES
    q_lora = lat_gain.shape[0]
    tm = _tile(seq, ATTN_TILE)
    per_b = seq // tm
    mod_spec = lambda k: pl.BlockSpec((None, 1, d), lambda i: (i // per_b, 0, k))
    assert n_heads % 2 == 0, n_heads
    full = lambda a: pl.BlockSpec(a.shape, lambda i: (0,) * a.ndim)
    lat_gain = lat_gain.reshape(1, q_lora)
    return pl.pallas_call(
        functools.partial(_mla_q_kernel, n_heads=n_heads),
        grid=(t // tm,),
        in_specs=[
            pl.BlockSpec((tm, d), lambda i: (i, 0)),
            pl.BlockSpec((1, d), lambda i: (0, 0)),
            mod_spec(0), mod_spec(1),
            pl.BlockSpec((None, d, q_lora), lambda i: (layer, 0, 0)),
            full(lat_gain), full(wu), full(gn), full(gr),
            pl.BlockSpec((tm, LANES), lambda i: (i, 0)),
        ],
        out_specs=pl.BlockSpec((None, n_heads, None, QK_PAD, tm), lambda i: (i // per_b, 0, i % per_b, 0, 0)),
        out_shape=jax.ShapeDtypeStruct((bsz, n_heads, seq // tm, QK_PAD, tm), BF16),
        compiler_params=_params("parallel"),
        name="mla_q",
    )(x, gain.reshape(1, d), mod, mod, wd, lat_gain, wu, gn, gr, cs)


def _flash_kernel(qt_ref, k_ref, vt_ref, o_ref, s_ref, m_ref, l_ref, acc_ref, *, tq, n_tiles, heads_per_step):
    def tile_rows(j):
        return pl.ds(pl.multiple_of(j * tq, tq), tq)

    def scores(hh, i, j):
        return jnp.dot(k_ref[hh, tile_rows(j), :], qt_ref[hh, i], preferred_element_type=F32)

    def consume(hh, j, masked):
        s = s_ref[hh]
        if masked:
            ki = lax.broadcasted_iota(jnp.int32, (tq, tq), 0)
            qi = lax.broadcasted_iota(jnp.int32, (tq, tq), 1)
            s = jnp.where(ki <= qi, s, -jnp.inf)
        m = m_ref[hh]
        m_new = jnp.maximum(m, jnp.max(s, axis=0, keepdims=True))
        alpha = jnp.exp2(m - m_new)
        p = jnp.exp2(s - m_new)
        m_ref[hh] = m_new
        l_ref[hh] = alpha * l_ref[hh] + jnp.sum(p, axis=0, keepdims=True)
        pv = jnp.dot(vt_ref[hh, j], p.astype(BF16), preferred_element_type=F32)
        acc_ref[hh] = alpha * acc_ref[hh] + pv

    for hh in range(heads_per_step):
        s_ref[hh] = scores(hh, 0, 0)

    def query_tile(i, carry):
        m_ref[...] = jnp.full(m_ref.shape, -jnp.inf, F32)
        l_ref[...] = jnp.zeros_like(l_ref)
        acc_ref[...] = jnp.zeros_like(acc_ref)

        def key_tile(j, c):
            for hh in range(heads_per_step):
                s_next = scores(hh, i, j + 1)
                consume(hh, j, False)
                s_ref[hh] = s_next
            return c

        lax.fori_loop(0, i, key_tile, 0)
        i_next = jnp.minimum(i + 1, n_tiles - 1)
        for hh in range(heads_per_step):
            s_next = scores(hh, i_next, 0)
            consume(hh, i, True)
            s_ref[hh] = s_next
            o_ref[tile_rows(i), hh * LANES:(hh + 1) * LANES] = (acc_ref[hh] / l_ref[hh]).T.astype(BF16)
        return carry

    lax.fori_loop(0, n_tiles, query_tile, 0)


def _flash(qt, k, vt):
    bsz, n_heads, seq, _ = k.shape
    tq = vt.shape[-1]
    hps = 4 if n_heads % 4 == 0 else (2 if n_heads % 2 == 0 else 1)
    return pl.pallas_call(
        functools.partial(_flash_kernel, tq=tq, n_tiles=seq // tq, heads_per_step=hps),
        grid=(bsz, n_heads // hps),
        in_specs=[
            pl.BlockSpec((None, hps, seq // tq, QK_PAD, tq), lambda b, h: (b, h, 0, 0, 0)),
            pl.BlockSpec((None, hps, seq, QK_PAD), lambda b, h: (b, h, 0, 0)),
            pl.BlockSpec((None, hps, seq // tq, LANES, tq), lambda b, h: (b, h, 0, 0, 0)),
        ],
        out_specs=pl.BlockSpec((None, seq, hps * LANES), lambda b, h: (b, 0, h)),
        out_shape=jax.ShapeDtypeStruct((bsz, seq, n_heads * LANES), BF16),
        scratch_shapes=[pltpu.VMEM((hps, tq, tq), F32),
                        pltpu.VMEM((hps, 1, tq), F32), pltpu.VMEM((hps, 1, tq), F32),
                        pltpu.VMEM((hps, LANES, tq), F32)],
        compiler_params=_params("parallel", "parallel"),
        name="flash",
    )(qt, k, vt)


def _swap_halves(a):
    half = a.shape[-1] // 2
    return jnp.concatenate([a[..., half:], a[..., :half]], axis=-1)


def _rope_gain(g):
    return jnp.concatenate([g, _swap_halves(g)]).reshape(1, LANES)


def kernel(x, c, positions, ada_w, ada_b, norm_mix, norm_mlp, mlp_w1, mlp_w2, hg_w_in, hg_lower, hg_o_norm, hg_w_out, kv_ada_w, kv_ada_b, kv_in_norm, mla_w_dkv, mla_kv_norm, mla_w_ukv, mla_k_norm, mla_w_dq, mla_q_lat_norm, mla_w_uq, mla_q_norm, mla_w_o):
    bsz, seq, d = x.shape
    depth = ada_w.shape[0]
    n_a = hg_w_in.shape[0]
    n_heads = d // LANES
    kv_lora = mla_kv_norm.shape[0]
    q_lora = mla_w_dq.shape[2]
    t = bsz * seq

    mod = _ada(c, ada_w, ada_b).reshape(depth, bsz, 1, 6 * d)
    kv_mod = _ada(c, kv_ada_w[None], kv_ada_b[None]).reshape(bsz, 1, 2 * d)

    inv_freq = 1.0 / (ROPE_THETA ** (jnp.arange(0, ROPE, 2, dtype=F32) / ROPE))
    ang = positions.astype(F32)[..., None] * inv_freq
    cos, sin = jnp.cos(ang), jnp.sin(ang)
    cs = jnp.concatenate([cos, cos, -sin, sin], axis=-1).reshape(t, LANES)

    lb_all = jnp.cumsum(jax.nn.softmax(hg_lower.astype(F32), axis=0), axis=0)[:n_a]

    hg_w_in, hg_w_out, mla_w_dq, mla_w_o, mlp_w1, mlp_w2 = (
        w.astype(BF16) for w in (hg_w_in, hg_w_out, mla_w_dq, mla_w_o, mlp_w1, mlp_w2))
    tn = _tile(d, HG_COL_TILE)
    hg_w_in = hg_w_in.reshape(n_a, d, 4, d // tn, tn).transpose(0, 3, 2, 1, 4)
    d_ff = mlp_w1.shape[2]
    tf = _tile(d_ff, MLP_HIDDEN_TILE)
    mlp_w1 = mlp_w1.reshape(depth, d, d_ff // tf, tf).transpose(0, 2, 1, 3)

    xf = x.reshape(t, d)
    k_sh = v_sh = None
    for layer in range(depth):
        if layer == n_a:
            wd = jnp.concatenate([mla_w_dkv, _swap_halves(mla_w_dkv[:, kv_lora:])], axis=1).astype(BF16)
            k_sh, v_sh = _mla_kv(xf, kv_in_norm, kv_mod, wd, mla_kv_norm, mla_w_ukv.astype(BF16),
                                 mla_k_norm[:LANES].reshape(1, LANES), _rope_gain(mla_k_norm[LANES:]),
                                 cs, bsz, seq)
        if layer < n_a:
            q, k, lf, v, sg = _hg_in(xf, norm_mix[layer], mod[layer], lb_all[layer],
                                     hg_w_in, layer, bsz, seq)
            y = _hg_rec(q, k, lf, v, sg, hg_o_norm[layer]).reshape(t, d)
            xf = _proj_res(y, hg_w_out, layer, xf, mod[layer], 2, seq)
        else:
            j = layer - n_a
            wu = mla_w_uq[j].reshape(q_lora, n_heads, QK)
            wu = jnp.concatenate([wu, _swap_halves(wu[..., LANES:])], axis=-1)
            wu = wu.reshape(q_lora, n_heads * QK_PAD).astype(BF16)
            sm_scale = float(QK) ** -0.5 * LOG2E
            qg = mla_q_norm[j] * sm_scale
            qh = _mla_q(xf, norm_mix[layer], mod[layer], mla_w_dq, j, mla_q_lat_norm[j], wu,
                        qg[:LANES].reshape(1, LANES), _rope_gain(qg[LANES:]), cs, bsz, seq)
            y = _flash(qh, k_sh, v_sh).reshape(t, d)
            xf = _proj_res(y, mla_w_o, j, xf, mod[layer], 2, seq)
        xf = _mlp(xf, norm_mlp[layer], mod[layer], mlp_w1, mlp_w2, layer, seq)
    return xf.reshape(bsz, seq, d)
```

```python
import functools

import jax
import jax.numpy as jnp
from jax import lax
from jax.experimental import pallas as pl
from jax.experimental.pallas import tpu as pltpu

F32 = jnp.float32
BF16 = jnp.bfloat16

EPS = 1e-6
LANES = 128
ROPE = 64
QK = LANES + ROPE
QK_PAD = 2 * LANES
CHUNK = 64
ATTN_TILE = 512
MAX_FIXED_SOFTMAX_SHIFT = 40.0
MAX_FACTORED_DECAY = 72.0
ROPE_THETA = 10000.0
LOG2E = 1.4426950408889634
VMEM_LIMIT = 56 * 1024 * 1024


def _params(*sem):
    return pltpu.CompilerParams(dimension_semantics=sem, vmem_limit_bytes=VMEM_LIMIT)


def _tile(n, pref):
    t = min(n, pref)
    while n % t:
        t -= LANES
    assert t > 0 and n % t == 0, (n, pref)
    return t


def _norm_mod(x, gain, scale, shift):
    r = lax.rsqrt(jnp.mean(x * x, axis=-1, keepdims=True) + EPS)
    return (x * r) * (gain * (1.0 + scale)) + shift


def _ada_kernel(c_ref, w_ref, b_ref, o_ref):
    c = c_ref[...]
    cond = (c / (1.0 + jnp.exp(-c))).astype(BF16)
    o_ref[...] = jnp.dot(cond, w_ref[...].astype(BF16), preferred_element_type=F32) + b_ref[...]


def _ada(c, w, b):
    n_layers, d, n = w.shape
    bsz = c.shape[0]
    tn = _tile(n, 1024)
    return pl.pallas_call(
        _ada_kernel,
        grid=(n_layers, n // tn),
        in_specs=[
            pl.BlockSpec((bsz, d), lambda l, j: (0, 0)),
            pl.BlockSpec((None, d, tn), lambda l, j: (l, 0, j)),
            pl.BlockSpec((None, 1, tn), lambda l, j: (l, 0, j)),
        ],
        out_specs=pl.BlockSpec((None, bsz, tn), lambda l, j: (l, 0, j)),
        out_shape=jax.ShapeDtypeStruct((n_layers, bsz, n), F32),
        compiler_params=_params("parallel", "parallel"),
        name="ada",
    )(c, w, b.reshape(n_layers, 1, n))


def _mlp_kernel(x_ref, gain_ref, shift_ref, scale_ref, gate_ref, w1_ref, w2_ref, o_ref, h_ref):
    j = pl.program_id(1)

    def gated_chunk():
        a = jnp.maximum(jnp.dot(h_ref[...], w1_ref[...], preferred_element_type=F32), 0.0)
        return gate_ref[...] * jnp.dot((a * a).astype(BF16), w2_ref[...], preferred_element_type=F32)

    @pl.when(j == 0)
    def _():
        h_ref[...] = _norm_mod(x_ref[...], gain_ref[...], scale_ref[...], shift_ref[...]).astype(BF16)
        o_ref[...] = x_ref[...] + gated_chunk()

    @pl.when(j > 0)
    def _():
        o_ref[...] += gated_chunk()


def _mlp(x, gain, mod, w1, w2, layer, seq):
    t, d = x.shape
    f = w1.shape[2]
    tm = _tile(seq, 1024)
    tf = _tile(f, 512)
    per_b = seq // tm
    mod_spec = lambda k: pl.BlockSpec((None, 1, d), lambda i, j: (i // per_b, 0, k))
    return pl.pallas_call(
        _mlp_kernel,
        grid=(t // tm, f // tf),
        in_specs=[
            pl.BlockSpec((tm, d), lambda i, j: (i, 0)),
            pl.BlockSpec((1, d), lambda i, j: (0, 0)),
            mod_spec(3), mod_spec(4), mod_spec(5),
            pl.BlockSpec((None, d, tf), lambda i, j: (layer, 0, j)),
            pl.BlockSpec((None, tf, d), lambda i, j: (layer, j, 0)),
        ],
        out_specs=pl.BlockSpec((tm, d), lambda i, j: (i, 0)),
        out_shape=jax.ShapeDtypeStruct((t, d), F32),
        scratch_shapes=[pltpu.VMEM((tm, d), BF16)],
        compiler_params=_params("parallel", "arbitrary"),
        name="mlp",
    )(x, gain.reshape(1, d), mod, mod, mod, w1, w2)


def _proj_res_kernel(y_ref, w_ref, x_ref, gate_ref, o_ref):
    o_ref[...] = x_ref[...] + gate_ref[...] * jnp.dot(y_ref[...], w_ref[...], preferred_element_type=F32)


def _proj_res(y, w, layer, x, mod, gate_idx, seq):
    t, d = x.shape
    k = y.shape[1]
    tm = _tile(seq, 512)
    tn = _tile(d, 2048)
    per_b = seq // tm
    return pl.pallas_call(
        _proj_res_kernel,
        grid=(t // tm, d // tn),
        in_specs=[
            pl.BlockSpec((tm, k), lambda i, j: (i, 0)),
            pl.BlockSpec((None, k, tn), lambda i, j: (layer, 0, j)),
            pl.BlockSpec((tm, tn), lambda i, j: (i, j)),
            pl.BlockSpec((None, 1, tn), lambda i, j: (i // per_b, 0, gate_idx * (d // tn) + j)),
        ],
        out_specs=pl.BlockSpec((tm, tn), lambda i, j: (i, j)),
        out_shape=jax.ShapeDtypeStruct((t, d), F32),
        compiler_params=_params("parallel", "parallel"),
        name="proj_res",
    )(y, w, x, mod)


def _hg_in_kernel(x_ref, gain_ref, shift_ref, scale_ref, llb_ref, l1mlb_ref, omlb_ref,
                  wq_ref, wf_ref, wi_ref, wg_ref,
                  q_ref, k_ref, lf_ref, v_ref, sg_ref, h_ref, *, heads_per_step, q_scale):
    def project():
        h = h_ref[...]
        z = jnp.dot(h, wf_ref[...], preferred_element_type=F32)
        q = jnp.dot(h, wq_ref[...], preferred_element_type=F32) * q_scale
        v = jnp.dot(h, wi_ref[...], preferred_element_type=F32)
        g = jnp.dot(h, wg_ref[...], preferred_element_type=F32)

        e = jnp.exp(-jnp.abs(z))
        log_sig = jnp.minimum(z, 0.0) - jnp.log(1.0 + e)
        a = llb_ref[...]
        c = l1mlb_ref[...] + log_sig
        lf = jnp.maximum(a, c) + jnp.log(1.0 + jnp.exp(-jnp.abs(a - c)))
        key = omlb_ref[...] * (jnp.where(z >= 0.0, e, 1.0) / (1.0 + e))
        sg = g / (1.0 + jnp.exp(-g))

        for hh in range(heads_per_step):
            cs = slice(hh * LANES, (hh + 1) * LANES)
            q_ref[hh] = q[:, cs].astype(BF16)
            k_ref[hh] = key[:, cs].astype(BF16)
            lf_ref[hh] = lf[:, cs]
            v_ref[hh] = v[:, cs].astype(BF16)
            sg_ref[hh] = sg[:, cs].astype(BF16)

    @pl.when(pl.program_id(1) == 0)
    def _():
        h_ref[...] = _norm_mod(x_ref[...], gain_ref[...], scale_ref[...], shift_ref[...]).astype(BF16)
        project()

    @pl.when(pl.program_id(1) > 0)
    def _():
        project()


def _hg_in(x, gain, mod, lb, w_in, layer, bsz, seq):
    t, d = x.shape
    n_heads = d // LANES
    tm = _tile(seq, 1024)
    tn = _tile(d, 2 * LANES)
    hps = tn // LANES
    nj = d // tn
    per_b = seq // tm
    mod_spec = lambda k: pl.BlockSpec((None, 1, d), lambda i, j: (i // per_b, 0, k))
    vec_spec = pl.BlockSpec((1, tn), lambda i, j: (0, j))
    w_spec = lambda grp: pl.BlockSpec((None, d, tn), lambda i, j: (layer, 0, grp * nj + j))
    out_spec = pl.BlockSpec((None, hps, tm, LANES), lambda i, j: (i // per_b, j, i % per_b, 0))
    hm = lambda dt: jax.ShapeDtypeStruct((bsz, n_heads, seq, LANES), dt)
    lbf = lb.reshape(1, d)
    kern = functools.partial(_hg_in_kernel, heads_per_step=hps, q_scale=float(LANES) ** -0.5)
    return pl.pallas_call(
        kern,
        grid=(t // tm, nj),
        in_specs=[
            pl.BlockSpec((tm, d), lambda i, j: (i, 0)),
            pl.BlockSpec((1, d), lambda i, j: (0, 0)),
            mod_spec(0), mod_spec(1),
            vec_spec, vec_spec, vec_spec,
            w_spec(0), w_spec(1), w_spec(2), w_spec(3),
        ],
        out_specs=[out_spec] * 5,
        out_shape=[hm(BF16), hm(BF16), hm(F32), hm(BF16), hm(BF16)],
        scratch_shapes=[pltpu.VMEM((tm, d), BF16)],
        compiler_params=_params("parallel", "arbitrary"),
        name="hg_in",
    )(x, gain.reshape(1, d), mod, mod, jnp.log(lbf), jnp.log1p(-lbf), 1.0 - lbf, w_in, w_in, w_in, w_in)


def _nt_dot(a, b):
    return lax.dot_general(a, b, (((1,), (1,)), ((), ())), preferred_element_type=F32)


def _tn_dot(a, b):
    return lax.dot_general(a, b, (((0,), (0,)), ((), ())), preferred_element_type=F32)


def _chunk_scores(q, k, b, lf):
    row = lax.broadcasted_iota(jnp.int32, (CHUNK, LANES), 0)
    ti = lax.broadcasted_iota(jnp.int32, (CHUNK, CHUNK), 0)
    si = lax.broadcasted_iota(jnp.int32, (CHUNK, CHUNK), 1)
    scores = jnp.where(ti == si, _nt_dot(q.astype(BF16), k.astype(BF16)), 0.0)
    start = b - lf
    c = 1
    while c < CHUNK:
        later = (row & c) != 0
        bound = jnp.where(later, start, pltpu.roll(start, CHUNK - c, axis=0))
        e = jnp.exp(jnp.where(later, b - bound, bound - b))
        m = jnp.where(later, q, k) * e
        qs = jnp.where(later, m, 0.0).astype(BF16)
        ks = jnp.where(later, 0.0, m).astype(BF16)
        same_parent = (ti & -(2 * c)) == (si & -(2 * c))
        scores = scores + jnp.where(same_parent, _nt_dot(qs, ks), 0.0)
        if 2 * c < CHUNK:
            start = jnp.where(later, pltpu.roll(start, c, axis=0), start)
        c *= 2
    return scores


def _cumsum_rows(lf, tri):
    hi = lf.astype(BF16)
    r1 = lf - hi.astype(F32)
    mid = r1.astype(BF16)
    lo = (r1 - mid.astype(F32)).astype(BF16)
    dot = lambda part: jnp.dot(tri, part, preferred_element_type=F32)
    return dot(hi) + dot(mid) + dot(lo)


def _rec_kernel(q_ref, k_ref, lf_ref, v_ref, sg_ref, og_ref, o_ref, st_ref, *, n_chunks):
    @pl.when(pl.program_id(2) == 0)
    def _():
        st_ref[...] = jnp.zeros_like(st_ref)

    ti = lax.broadcasted_iota(jnp.int32, (CHUNK, CHUNK), 0)
    si = lax.broadcasted_iota(jnp.int32, (CHUNK, CHUNK), 1)
    causal = si <= ti
    tri = jnp.where(causal, 1.0, 0.0).astype(BF16)
    og = og_ref[...]

    def rows(ci):
        return slice(ci * CHUNK, (ci + 1) * CHUNK)

    mean_w = jnp.full((LANES, LANES), 1.0 / LANES, BF16)

    def finish(outs):
        ms = [jnp.dot((o * o).astype(BF16), mean_w, preferred_element_type=F32) for o in outs]
        for ci, o in enumerate(outs):
            o = o * lax.rsqrt(ms[ci] + EPS) * og
            o_ref[rows(ci), :] = (o * sg_ref[rows(ci), :].astype(F32)).astype(BF16)

    cs = range(n_chunks)
    b = [_cumsum_rows(lf_ref[rows(ci), :], tri) for ci in cs]
    total = b[0][CHUNK - 1:CHUNK, :]
    for ci in range(1, n_chunks):
        total = jnp.minimum(total, b[ci][CHUNK - 1:CHUNK, :])
    mild = jnp.min(total) >= -MAX_FACTORED_DECAY
    q_dec = [(q_ref[rows(ci), :].astype(F32) * jnp.exp(b[ci])).astype(BF16) for ci in cs]
    decay = [jnp.exp(b[ci][CHUNK - 1:CHUNK, :]) for ci in cs]
    k_grow = [k_ref[rows(ci), :].astype(F32) * jnp.exp(-b[ci]) for ci in cs]
    scores = [jnp.where(causal, jnp.dot(q_dec[ci], k_grow[ci].T.astype(BF16), preferred_element_type=F32),
                        0.0).astype(BF16) for ci in cs]
    intra = [jnp.dot(scores[ci], v_ref[rows(ci), :], preferred_element_type=F32) for ci in cs]

    @pl.when(mild)
    def _():
        update = [_tn_dot(v_ref[rows(ci), :], (k_grow[ci] * decay[ci]).astype(BF16)) for ci in cs]
        states = [st_ref[...]]
        for ci in cs:
            states.append(states[ci] * decay[ci] + update[ci])
        finish([intra[ci] + jnp.dot(q_dec[ci], states[ci].T.astype(BF16), preferred_element_type=F32)
                for ci in cs])
        st_ref[...] = states[n_chunks]

    @pl.when(jnp.logical_not(mild))
    def _():
        state_t = st_ref[...]
        outs = []
        for ci in range(n_chunks):
            lf = lf_ref[rows(ci), :]
            q = q_ref[rows(ci), :].astype(F32)
            k = k_ref[rows(ci), :].astype(F32)
            v = v_ref[rows(ci), :]
            scores = _chunk_scores(q, k, b[ci], lf)
            o = jnp.dot(scores.astype(BF16), v, preferred_element_type=F32)
            o = o + _nt_dot(q_dec[ci], state_t.astype(BF16))
            k_dec = (k * jnp.exp(b[ci][CHUNK - 1:CHUNK, :] - b[ci])).astype(BF16)
            state_t = state_t * decay[ci] + _tn_dot(v, k_dec)
            outs.append(o)
        finish(outs)
        st_ref[...] = state_t


def _hg_rec(q, k, lf, v, sg, o_gain):
    bsz, n_heads, seq, _ = q.shape
    ts = _tile(seq, 2048)
    in_spec = pl.BlockSpec((None, None, ts, LANES), lambda b, h, s: (b, h, s, 0))
    return pl.pallas_call(
        functools.partial(_rec_kernel, n_chunks=ts // CHUNK),
        grid=(bsz, n_heads, seq // ts),
        in_specs=[in_spec] * 5 + [pl.BlockSpec((1, LANES), lambda b, h, s: (0, 0))],
        out_specs=pl.BlockSpec((None, ts, LANES), lambda b, h, s: (b, s, h)),
        out_shape=jax.ShapeDtypeStruct((bsz, seq, n_heads * LANES), BF16),
        scratch_shapes=[pltpu.VMEM((LANES, LANES), F32)],
        compiler_params=_params("parallel", "parallel", "arbitrary"),
        name="hg_rec",
    )(q, k, lf, v, sg, o_gain.reshape(1, LANES))


def _pair_sum_weights():
    row = lax.broadcasted_iota(jnp.int32, (QK_PAD, QK_PAD), 0)
    col = lax.broadcasted_iota(jnp.int32, (QK_PAD, QK_PAD), 1)
    return jnp.where((row < LANES) == (col < LANES), 1.0, 0.0).astype(BF16)


def _rope_pair(u):
    return u + pltpu.roll(u, ROPE, axis=1)


def _mla_kv_kernel(x_ref, gain_ref, shift_ref, scale_ref, wd_ref, lat_gain_ref, wu_ref,
                   gn_ref, gr_ref, cs_ref, k_ref, v_ref, *, n_heads, kv_lora):
    h = _norm_mod(x_ref[...], gain_ref[...], scale_ref[...], shift_ref[...]).astype(BF16)
    ckv = jnp.dot(h, wd_ref[...], preferred_element_type=F32)
    lat = ckv[:, :kv_lora]
    pe2 = ckv[:, kv_lora:]
    c_lat = (lat * lax.rsqrt(jnp.mean(lat * lat, axis=-1, keepdims=True) + EPS) * lat_gain_ref[...]).astype(BF16)
    pair_w = _pair_sum_weights()
    sq_pe = 0.5 * (pe2 * pe2)
    rot = _rope_pair(pe2 * gr_ref[...] * cs_ref[...])
    gn = gn_ref[...]

    def pair_kv(p):
        return [jnp.dot(c_lat, wu_ref[:, hh * QK_PAD:(hh + 1) * QK_PAD], preferred_element_type=F32)
                for hh in (2 * p, 2 * p + 1)]

    nxt = pair_kv(0)
    for p in range(n_heads // 2):
        kvs = nxt
        if p + 1 < n_heads // 2:
            nxt = pair_kv(p + 1)
        sq = jnp.concatenate([kv[:, :LANES] * kv[:, :LANES] + sq_pe for kv in kvs], axis=1)
        r2 = lax.rsqrt(jnp.dot(sq.astype(BF16), pair_w, preferred_element_type=F32) * (1.0 / QK) + EPS)
        for idx, kv in enumerate(kvs):
            hh = 2 * p + idx
            r = r2[:, idx * LANES:(idx + 1) * LANES]
            k_ref[hh, :, :LANES] = (kv[:, :LANES] * r * gn).astype(BF16)
            k_ref[hh, :, LANES:] = (rot * r).astype(BF16)
            v_ref[hh] = kv[:, LANES:].T.astype(BF16)


def _mla_kv(x, gain, mod, wd, lat_gain, wu, gn, gr, cs, bsz, seq):
    t, d = x.shape
    n_heads = d // LANES
    kv_lora = lat_gain.shape[0]
    tm = _tile(seq, ATTN_TILE)
    per_b = seq // tm
    mod_spec = lambda k: pl.BlockSpec((None, 1, d), lambda i: (i // per_b, 0, k))
    assert n_heads % 2 == 0, n_heads
    full = lambda a: pl.BlockSpec(a.shape, lambda i: (0,) * a.ndim)
    lat_gain = lat_gain.reshape(1, kv_lora)
    return pl.pallas_call(
        functools.partial(_mla_kv_kernel, n_heads=n_heads, kv_lora=kv_lora),
        grid=(t // tm,),
        in_specs=[
            pl.BlockSpec((tm, d), lambda i: (i, 0)),
            pl.BlockSpec((1, d), lambda i: (0, 0)),
            mod_spec(0), mod_spec(1),
            full(wd), full(lat_gain), full(wu), full(gn), full(gr),
            pl.BlockSpec((tm, LANES), lambda i: (i, 0)),
        ],
        out_specs=[
            pl.BlockSpec((None, n_heads, tm, QK_PAD), lambda i: (i // per_b, 0, i % per_b, 0)),
            pl.BlockSpec((None, n_heads, None, LANES, tm), lambda i: (i // per_b, 0, i % per_b, 0, 0)),
        ],
        out_shape=[jax.ShapeDtypeStruct((bsz, n_heads, seq, QK_PAD), BF16),
                   jax.ShapeDtypeStruct((bsz, n_heads, seq // tm, LANES, tm), BF16)],
        compiler_params=_params("parallel"),
        name="mla_kv",
    )(x, gain.reshape(1, d), mod, mod, wd, lat_gain, wu, gn, gr, cs)


def _mla_q_kernel(x_ref, gain_ref, shift_ref, scale_ref, wd_ref, lat_gain_ref, wu_ref,
                  gn_ref, gr_ref, cs_ref, q_ref, *, n_heads):
    h = _norm_mod(x_ref[...], gain_ref[...], scale_ref[...], shift_ref[...]).astype(BF16)
    cq = jnp.dot(h, wd_ref[...], preferred_element_type=F32)
    cq = (cq * lax.rsqrt(jnp.mean(cq * cq, axis=-1, keepdims=True) + EPS) * lat_gain_ref[...]).astype(BF16)
    gn = gn_ref[...]
    rope_tab = gr_ref[...] * cs_ref[...]
    pair_w = _pair_sum_weights()

    def pair_q(p):
        return [jnp.dot(cq, wu_ref[:, hh * QK_PAD:(hh + 1) * QK_PAD], preferred_element_type=F32)
                for hh in (2 * p, 2 * p + 1)]

    nxt = pair_q(0)
    for p in range(n_heads // 2):
        qhs = nxt
        if p + 1 < n_heads // 2:
            nxt = pair_q(p + 1)
        sq = jnp.concatenate([qh[:, :LANES] * qh[:, :LANES] + 0.5 * (qh[:, LANES:] * qh[:, LANES:])
                              for qh in qhs], axis=1)
        r2 = lax.rsqrt(jnp.dot(sq.astype(BF16), pair_w, preferred_element_type=F32) * (1.0 / QK) + EPS)
        for idx, qh in enumerate(qhs):
            hh = 2 * p + idx
            r = r2[:, idx * LANES:(idx + 1) * LANES]
            q_ref[hh, :LANES, :] = (qh[:, :LANES] * r * gn).T.astype(BF16)
            q_ref[hh, LANES:, :] = (qh[:, LANES:] * rope_tab * r).T.astype(BF16)


def _mla_q(x, gain, mod, wd, layer, lat_gain, wu, gn, gr, cs, bsz, seq):
    t, d = x.shape
    n_heads = d // LANES
    q_lora = lat_gain.shape[0]
    tm = _tile(seq, ATTN_TILE)
    per_b = seq // tm
    mod_spec = lambda k: pl.BlockSpec((None, 1, d), lambda i: (i // per_b, 0, k))
    assert n_heads % 2 == 0, n_heads
    full = lambda a: pl.BlockSpec(a.shape, lambda i: (0,) * a.ndim)
    lat_gain = lat_gain.reshape(1, q_lora)
    return pl.pallas_call(
        functools.partial(_mla_q_kernel, n_heads=n_heads),
        grid=(t // tm,),
        in_specs=[
            pl.BlockSpec((tm, d), lambda i: (i, 0)),
            pl.BlockSpec((1, d), lambda i: (0, 0)),
            mod_spec(0), mod_spec(1),
            pl.BlockSpec((None, d, q_lora), lambda i: (layer, 0, 0)),
            full(lat_gain), full(wu), full(gn), full(gr),
            pl.BlockSpec((tm, LANES), lambda i: (i, 0)),
        ],
        out_specs=pl.BlockSpec((None, n_heads, None, QK_PAD, tm), lambda i: (i // per_b, 0, i % per_b, 0, 0)),
        out_shape=jax.ShapeDtypeStruct((bsz, n_heads, seq // tm, QK_PAD, tm), BF16),
        compiler_params=_params("parallel"),
        name="mla_q",
    )(x, gain.reshape(1, d), mod, mod, wd, lat_gain, wu, gn, gr, cs)


def _flash_kernel(bound_ref, qt_ref, k_ref, vt_ref, o_ref, s_ref, m_ref, l_ref, acc_ref, *,
                  tq, n_tiles, heads_per_step):
    def tile_rows(j):
        return pl.ds(pl.multiple_of(j * tq, tq), tq)

    def scores(hh, i, j):
        return jnp.dot(k_ref[hh, tile_rows(j), :], qt_ref[hh, i], preferred_element_type=F32)

    bound = bound_ref[0, 0]

    def consume(hh, j, masked, fixed_shift):
        s = s_ref[hh]
        if masked:
            ki = lax.broadcasted_iota(jnp.int32, (tq, tq), 0)
            qi = lax.broadcasted_iota(jnp.int32, (tq, tq), 1)
            s = jnp.where(ki <= qi, s, -jnp.inf)
        if fixed_shift:
            p = jnp.exp2(s - bound)
            l_ref[hh] += jnp.sum(p, axis=0, keepdims=True)
            acc_ref[hh] += jnp.dot(vt_ref[hh, j], p.astype(BF16), preferred_element_type=F32)
            return
        m = m_ref[hh]
        m_new = jnp.maximum(m, jnp.max(s, axis=0, keepdims=True))
        alpha = jnp.exp2(m - m_new)
        p = jnp.exp2(s - m_new)
        m_ref[hh] = m_new
        l_ref[hh] = alpha * l_ref[hh] + jnp.sum(p, axis=0, keepdims=True)
        pv = jnp.dot(vt_ref[hh, j], p.astype(BF16), preferred_element_type=F32)
        acc_ref[hh] = alpha * acc_ref[hh] + pv

    def attend(fixed_shift):
        for hh in range(heads_per_step):
            s_ref[hh] = scores(hh, 0, 0)

        def query_tile(i, carry):
            m_ref[...] = jnp.full(m_ref.shape, -jnp.inf, F32)
            l_ref[...] = jnp.zeros_like(l_ref)
            acc_ref[...] = jnp.zeros_like(acc_ref)

            def key_tile(j, c):
                for hh in range(heads_per_step):
                    s_next = scores(hh, i, j + 1)
                    consume(hh, j, False, fixed_shift)
                    s_ref[hh] = s_next
                return c

            lax.fori_loop(0, i, key_tile, 0)
            i_next = jnp.minimum(i + 1, n_tiles - 1)
            for hh in range(heads_per_step):
                s_next = scores(hh, i_next, 0)
                consume(hh, i, True, fixed_shift)
                s_ref[hh] = s_next
                o_ref[tile_rows(i), hh * LANES:(hh + 1) * LANES] = (acc_ref[hh] / l_ref[hh]).T.astype(BF16)
            return carry

        lax.fori_loop(0, n_tiles, query_tile, 0)

    @pl.when(bound <= MAX_FIXED_SOFTMAX_SHIFT)
    def _():
        attend(True)

    @pl.when(jnp.logical_not(bound <= MAX_FIXED_SOFTMAX_SHIFT))
    def _():
        attend(False)


def _flash(qt, k, vt, score_bound):
    bsz, n_heads, seq, _ = k.shape
    tq = vt.shape[-1]
    hps = 4 if n_heads % 4 == 0 else (2 if n_heads % 2 == 0 else 1)
    return pl.pallas_call(
        functools.partial(_flash_kernel, tq=tq, n_tiles=seq // tq, heads_per_step=hps),
        grid=(bsz, n_heads // hps),
        in_specs=[
            pl.BlockSpec(memory_space=pltpu.SMEM),
            pl.BlockSpec((None, hps, seq // tq, QK_PAD, tq), lambda b, h: (b, h, 0, 0, 0)),
            pl.BlockSpec((None, hps, seq, QK_PAD), lambda b, h: (b, h, 0, 0)),
            pl.BlockSpec((None, hps, seq // tq, LANES, tq), lambda b, h: (b, h, 0, 0, 0)),
        ],
        out_specs=pl.BlockSpec((None, seq, hps * LANES), lambda b, h: (b, 0, h)),
        out_shape=jax.ShapeDtypeStruct((bsz, seq, n_heads * LANES), BF16),
        scratch_shapes=[pltpu.VMEM((hps, tq, tq), F32),
                        pltpu.VMEM((hps, 1, tq), F32), pltpu.VMEM((hps, 1, tq), F32),
                        pltpu.VMEM((hps, LANES, tq), F32)],
        compiler_params=_params("parallel", "parallel"),
        name="flash",
    )(score_bound, qt, k, vt)


def _swap_halves(a):
    half = a.shape[-1] // 2
    return jnp.concatenate([a[..., half:], a[..., :half]], axis=-1)


def _rope_gain(g):
    return jnp.concatenate([g, _swap_halves(g)]).reshape(1, LANES)


def kernel(x, c, positions, ada_w, ada_b, norm_mix, norm_mlp, mlp_w1, mlp_w2, hg_w_in, hg_lower, hg_o_norm, hg_w_out, kv_ada_w, kv_ada_b, kv_in_norm, mla_w_dkv, mla_kv_norm, mla_w_ukv, mla_k_norm, mla_w_dq, mla_q_lat_norm, mla_w_uq, mla_q_norm, mla_w_o):
    bsz, seq, d = x.shape
    depth = ada_w.shape[0]
    n_a = hg_w_in.shape[0]
    n_heads = d // LANES
    kv_lora = mla_kv_norm.shape[0]
    q_lora = mla_w_dq.shape[2]
    t = bsz * seq

    mod = _ada(c, ada_w, ada_b).reshape(depth, bsz, 1, 6 * d)
    kv_mod = _ada(c, kv_ada_w[None], kv_ada_b[None]).reshape(bsz, 1, 2 * d)

    inv_freq = 1.0 / (ROPE_THETA ** (jnp.arange(0, ROPE, 2, dtype=F32) / ROPE))
    ang = positions.astype(F32)[..., None] * inv_freq
    cos, sin = jnp.cos(ang), jnp.sin(ang)
    cs = jnp.concatenate([cos, cos, -sin, sin], axis=-1).reshape(t, LANES)

    lb_all = jnp.cumsum(jax.nn.softmax(hg_lower.astype(F32), axis=0), axis=0)[:n_a]

    hg_w_in, hg_w_out, mla_w_dq, mla_w_o, mlp_w1, mlp_w2 = (
        w.astype(BF16) for w in (hg_w_in, hg_w_out, mla_w_dq, mla_w_o, mlp_w1, mlp_w2))

    xf = x.reshape(t, d)
    k_sh = v_sh = None
    for layer in range(depth):
        if layer == n_a:
            wd = jnp.concatenate([mla_w_dkv, _swap_halves(mla_w_dkv[:, kv_lora:])], axis=1).astype(BF16)
            k_sh, v_sh = _mla_kv(xf, kv_in_norm, kv_mod, wd, mla_kv_norm, mla_w_ukv.astype(BF16),
                                 mla_k_norm[:LANES].reshape(1, LANES), _rope_gain(mla_k_norm[LANES:]),
                                 cs, bsz, seq)
        if layer < n_a:
            q, k, lf, v, sg = _hg_in(xf, norm_mix[layer], mod[layer], lb_all[layer],
                                     hg_w_in, layer, bsz, seq)
            y = _hg_rec(q, k, lf, v, sg, hg_o_norm[layer]).reshape(t, d)
            xf = _proj_res(y, hg_w_out, layer, xf, mod[layer], 2, seq)
        else:
            j = layer - n_a
            wu = mla_w_uq[j].reshape(q_lora, n_heads, QK)
            wu = jnp.concatenate([wu, _swap_halves(wu[..., LANES:])], axis=-1)
            wu = wu.reshape(q_lora, n_heads * QK_PAD).astype(BF16)
            sm_scale = float(QK) ** -0.5 * LOG2E
            qg = mla_q_norm[j] * sm_scale
            qh = _mla_q(xf, norm_mix[layer], mod[layer], mla_w_dq, j, mla_q_lat_norm[j], wu,
                        qg[:LANES].reshape(1, LANES), _rope_gain(qg[LANES:]), cs, bsz, seq)
            score_bound = (1.03 * QK * 2.0 ** 0.5) * jnp.max(jnp.abs(qg)) * jnp.max(jnp.abs(mla_k_norm))
            y = _flash(qh, k_sh, v_sh, score_bound.reshape(1, 1).astype(F32)).reshape(t, d)
            xf = _proj_res(y, mla_w_o, j, xf, mod[layer], 2, seq)
        xf = _mlp(xf, norm_mlp[layer], mod[layer], mlp_w1, mlp_w2, layer, seq)
    return xf.reshape(bsz, seq, d)
```

```python
import functools

import jax
import jax.numpy as jnp
from jax import lax
from jax.experimental import pallas as pl
from jax.experimental.pallas import tpu as pltpu

F32 = jnp.float32
BF16 = jnp.bfloat16

EPS = 1e-6
LANES = 128
ROPE = 64
QK = LANES + ROPE
QK_PAD = 2 * LANES
CHUNK = 64
ATTN_TILE = 512
MAX_FIXED_SOFTMAX_SHIFT = 40.0
MAX_FACTORED_DECAY = 72.0
ROPE_THETA = 10000.0
LOG2E = 1.4426950408889634
VMEM_LIMIT = 56 * 1024 * 1024


def _params(*sem):
    return pltpu.CompilerParams(dimension_semantics=sem, vmem_limit_bytes=VMEM_LIMIT)


def _tile(n, pref):
    t = min(n, pref)
    while n % t:
        t -= LANES
    assert t > 0 and n % t == 0, (n, pref)
    return t


def _norm_mod(x, gain, scale, shift):
    r = lax.rsqrt(jnp.mean(x * x, axis=-1, keepdims=True) + EPS)
    return (x * r) * (gain * (1.0 + scale)) + shift


def _ada_kernel(c_ref, w_ref, b_ref, o_ref):
    c = c_ref[...]
    cond = (c / (1.0 + jnp.exp(-c))).astype(BF16)
    o_ref[...] = jnp.dot(cond, w_ref[...].astype(BF16), preferred_element_type=F32) + b_ref[...]


def _ada(c, w, b):
    n_layers, d, n = w.shape
    bsz = c.shape[0]
    tn = _tile(n, 1024)
    return pl.pallas_call(
        _ada_kernel,
        grid=(n_layers, n // tn),
        in_specs=[
            pl.BlockSpec((bsz, d), lambda l, j: (0, 0)),
            pl.BlockSpec((None, d, tn), lambda l, j: (l, 0, j)),
            pl.BlockSpec((None, 1, tn), lambda l, j: (l, 0, j)),
        ],
        out_specs=pl.BlockSpec((None, bsz, tn), lambda l, j: (l, 0, j)),
        out_shape=jax.ShapeDtypeStruct((n_layers, bsz, n), F32),
        compiler_params=_params("parallel", "parallel"),
        name="ada",
    )(c, w, b.reshape(n_layers, 1, n))


def _mlp_kernel(x_ref, gain_ref, shift_ref, scale_ref, gate_ref, w1_ref, w2_ref, o_ref, h_ref, *, n_sub):
    j = pl.program_id(1)
    sub = w1_ref.shape[1] // n_sub

    def gated_piece(c):
        cols = slice(c * sub, (c + 1) * sub)
        a = jnp.maximum(jnp.dot(h_ref[...], w1_ref[:, cols], preferred_element_type=F32), 0.0)
        return gate_ref[...] * jnp.dot((a * a).astype(BF16), w2_ref[cols, :], preferred_element_type=F32)

    @pl.when(j == 0)
    def _():
        h_ref[...] = _norm_mod(x_ref[...], gain_ref[...], scale_ref[...], shift_ref[...]).astype(BF16)
        o_ref[...] = x_ref[...] + gated_piece(0)
        for c in range(1, n_sub):
            o_ref[...] += gated_piece(c)

    @pl.when(j > 0)
    def _():
        for c in range(n_sub):
            o_ref[...] += gated_piece(c)


def _mlp(x, gain, mod, w1, w2, layer, seq):
    t, d = x.shape
    f = w1.shape[2]
    tm = _tile(seq, 1024)
    tf = _tile(f, 1024)
    n_sub = 2 if tf % (2 * LANES) == 0 else 1
    per_b = seq // tm
    mod_spec = lambda k: pl.BlockSpec((None, 1, d), lambda i, j: (i // per_b, 0, k))
    return pl.pallas_call(
        functools.partial(_mlp_kernel, n_sub=n_sub),
        grid=(t // tm, f // tf),
        in_specs=[
            pl.BlockSpec((tm, d), lambda i, j: (i, 0)),
            pl.BlockSpec((1, d), lambda i, j: (0, 0)),
            mod_spec(3), mod_spec(4), mod_spec(5),
            pl.BlockSpec((None, d, tf), lambda i, j: (layer, 0, j)),
            pl.BlockSpec((None, tf, d), lambda i, j: (layer, j, 0)),
        ],
        out_specs=pl.BlockSpec((tm, d), lambda i, j: (i, 0)),
        out_shape=jax.ShapeDtypeStruct((t, d), F32),
        scratch_shapes=[pltpu.VMEM((tm, d), BF16)],
        compiler_params=_params("parallel", "arbitrary"),
        name="mlp",
    )(x, gain.reshape(1, d), mod, mod, mod, w1, w2)


def _proj_res_kernel(y_ref, w_ref, x_ref, gate_ref, o_ref):
    o_ref[...] = x_ref[...] + gate_ref[...] * jnp.dot(y_ref[...], w_ref[...], preferred_element_type=F32)


def _proj_res(y, w, layer, x, mod, gate_idx, seq):
    t, d = x.shape
    k = y.shape[1]
    tm = _tile(seq, 512)
    tn = _tile(d, 2048)
    per_b = seq // tm
    return pl.pallas_call(
        _proj_res_kernel,
        grid=(t // tm, d // tn),
        in_specs=[
            pl.BlockSpec((tm, k), lambda i, j: (i, 0)),
            pl.BlockSpec((None, k, tn), lambda i, j: (layer, 0, j)),
            pl.BlockSpec((tm, tn), lambda i, j: (i, j)),
            pl.BlockSpec((None, 1, tn), lambda i, j: (i // per_b, 0, gate_idx * (d // tn) + j)),
        ],
        out_specs=pl.BlockSpec((tm, tn), lambda i, j: (i, j)),
        out_shape=jax.ShapeDtypeStruct((t, d), F32),
        compiler_params=_params("parallel", "parallel"),
        name="proj_res",
    )(y, w, x, mod)


def _hg_in_kernel(x_ref, gain_ref, shift_ref, scale_ref, llb_ref, l1mlb_ref, omlb_ref,
                  wq_ref, wf_ref, wi_ref, wg_ref,
                  q_ref, k_ref, lf_ref, v_ref, sg_ref, h_ref, *, heads_per_step, q_scale):
    def project():
        h = h_ref[...]
        z = jnp.dot(h, wf_ref[...], preferred_element_type=F32)
        q = jnp.dot(h, wq_ref[...], preferred_element_type=F32) * q_scale
        v = jnp.dot(h, wi_ref[...], preferred_element_type=F32)
        g = jnp.dot(h, wg_ref[...], preferred_element_type=F32)

        e = jnp.exp(-jnp.abs(z))
        log_sig = jnp.minimum(z, 0.0) - jnp.log(1.0 + e)
        a = llb_ref[...]
        c = l1mlb_ref[...] + log_sig
        lf = jnp.maximum(a, c) + jnp.log(1.0 + jnp.exp(-jnp.abs(a - c)))
        key = omlb_ref[...] * (jnp.where(z >= 0.0, e, 1.0) / (1.0 + e))
        sg = g / (1.0 + jnp.exp(-g))

        for hh in range(heads_per_step):
            cs = slice(hh * LANES, (hh + 1) * LANES)
            q_ref[hh] = q[:, cs].astype(BF16)
            k_ref[hh] = key[:, cs].astype(BF16)
            lf_ref[hh] = lf[:, cs]
            v_ref[hh] = v[:, cs].astype(BF16)
            sg_ref[hh] = sg[:, cs].astype(BF16)

    @pl.when(pl.program_id(1) == 0)
    def _():
        h_ref[...] = _norm_mod(x_ref[...], gain_ref[...], scale_ref[...], shift_ref[...]).astype(BF16)
        project()

    @pl.when(pl.program_id(1) > 0)
    def _():
        project()


def _hg_in(x, gain, mod, lb, w_in, layer, bsz, seq):
    t, d = x.shape
    n_heads = d // LANES
    tm = _tile(seq, 1024)
    tn = _tile(d, 2 * LANES)
    hps = tn // LANES
    nj = d // tn
    per_b = seq // tm
    mod_spec = lambda k: pl.BlockSpec((None, 1, d), lambda i, j: (i // per_b, 0, k))
    vec_spec = pl.BlockSpec((1, tn), lambda i, j: (0, j))
    w_spec = lambda grp: pl.BlockSpec((None, d, tn), lambda i, j: (layer, 0, grp * nj + j))
    out_spec = pl.BlockSpec((None, hps, tm, LANES), lambda i, j: (i // per_b, j, i % per_b, 0))
    hm = lambda dt: jax.ShapeDtypeStruct((bsz, n_heads, seq, LANES), dt)
    lbf = lb.reshape(1, d)
    kern = functools.partial(_hg_in_kernel, heads_per_step=hps, q_scale=float(LANES) ** -0.5)
    return pl.pallas_call(
        kern,
        grid=(t // tm, nj),
        in_specs=[
            pl.BlockSpec((tm, d), lambda i, j: (i, 0)),
            pl.BlockSpec((1, d), lambda i, j: (0, 0)),
            mod_spec(0), mod_spec(1),
            vec_spec, vec_spec, vec_spec,
            w_spec(0), w_spec(1), w_spec(2), w_spec(3),
        ],
        out_specs=[out_spec] * 5,
        out_shape=[hm(BF16), hm(BF16), hm(F32), hm(BF16), hm(BF16)],
        scratch_shapes=[pltpu.VMEM((tm, d), BF16)],
        compiler_params=_params("parallel", "arbitrary"),
        name="hg_in",
    )(x, gain.reshape(1, d), mod, mod, jnp.log(lbf), jnp.log1p(-lbf), 1.0 - lbf, w_in, w_in, w_in, w_in)


def _nt_dot(a, b):
    return lax.dot_general(a, b, (((1,), (1,)), ((), ())), preferred_element_type=F32)


def _tn_dot(a, b):
    return lax.dot_general(a, b, (((0,), (0,)), ((), ())), preferred_element_type=F32)


def _chunk_scores(q, k, b, lf):
    row = lax.broadcasted_iota(jnp.int32, (CHUNK, LANES), 0)
    ti = lax.broadcasted_iota(jnp.int32, (CHUNK, CHUNK), 0)
    si = lax.broadcasted_iota(jnp.int32, (CHUNK, CHUNK), 1)
    scores = jnp.where(ti == si, _nt_dot(q.astype(BF16), k.astype(BF16)), 0.0)
    start = b - lf
    c = 1
    while c < CHUNK:
        later = (row & c) != 0
        bound = jnp.where(later, start, pltpu.roll(start, CHUNK - c, axis=0))
        e = jnp.exp(jnp.where(later, b - bound, bound - b))
        m = jnp.where(later, q, k) * e
        qs = jnp.where(later, m, 0.0).astype(BF16)
        ks = jnp.where(later, 0.0, m).astype(BF16)
        same_parent = (ti & -(2 * c)) == (si & -(2 * c))
        scores = scores + jnp.where(same_parent, _nt_dot(qs, ks), 0.0)
        if 2 * c < CHUNK:
            start = jnp.where(later, pltpu.roll(start, c, axis=0), start)
        c *= 2
    return scores


def _cumsum_rows(lf, tri):
    hi = lf.astype(BF16)
    r1 = lf - hi.astype(F32)
    mid = r1.astype(BF16)
    lo = (r1 - mid.astype(F32)).astype(BF16)
    dot = lambda part: jnp.dot(tri, part, preferred_element_type=F32)
    return dot(hi) + dot(mid) + dot(lo)


def _rec_kernel(q_ref, k_ref, lf_ref, v_ref, sg_ref, og_ref, o_ref, st_ref, *, n_chunks):
    @pl.when(pl.program_id(2) == 0)
    def _():
        st_ref[...] = jnp.zeros_like(st_ref)

    ti = lax.broadcasted_iota(jnp.int32, (CHUNK, CHUNK), 0)
    si = lax.broadcasted_iota(jnp.int32, (CHUNK, CHUNK), 1)
    causal = si <= ti
    tri = jnp.where(causal, 1.0, 0.0).astype(BF16)
    og = og_ref[...]

    def rows(ci):
        return slice(ci * CHUNK, (ci + 1) * CHUNK)

    mean_w = jnp.full((LANES, LANES), 1.0 / LANES, BF16)

    def finish(outs):
        ms = [jnp.dot((o * o).astype(BF16), mean_w, preferred_element_type=F32) for o in outs]
        for ci, o in enumerate(outs):
            o = o * lax.rsqrt(ms[ci] + EPS) * og
            o_ref[rows(ci), :] = (o * sg_ref[rows(ci), :].astype(F32)).astype(BF16)

    cs = range(n_chunks)
    b = [_cumsum_rows(lf_ref[rows(ci), :], tri) for ci in cs]
    total = b[0][CHUNK - 1:CHUNK, :]
    for ci in range(1, n_chunks):
        total = jnp.minimum(total, b[ci][CHUNK - 1:CHUNK, :])
    mild = jnp.min(total) >= -MAX_FACTORED_DECAY
    q_dec = [(q_ref[rows(ci), :].astype(F32) * jnp.exp(b[ci])).astype(BF16) for ci in cs]
    decay = [jnp.exp(b[ci][CHUNK - 1:CHUNK, :]) for ci in cs]
    k_grow = [k_ref[rows(ci), :].astype(F32) * jnp.exp(-b[ci]) for ci in cs]
    scores = [jnp.where(causal, jnp.dot(q_dec[ci], k_grow[ci].T.astype(BF16), preferred_element_type=F32),
                        0.0).astype(BF16) for ci in cs]
    intra = [jnp.dot(scores[ci], v_ref[rows(ci), :], preferred_element_type=F32) for ci in cs]

    @pl.when(mild)
    def _():
        update = [_tn_dot(v_ref[rows(ci), :], (k_grow[ci] * decay[ci]).astype(BF16)) for ci in cs]
        states = [st_ref[...]]
        for ci in cs:
            states.append(states[ci] * decay[ci] + update[ci])
        finish([intra[ci] + jnp.dot(q_dec[ci], states[ci].T.astype(BF16), preferred_element_type=F32)
                for ci in cs])
        st_ref[...] = states[n_chunks]

    @pl.when(jnp.logical_not(mild))
    def _():
        state_t = st_ref[...]
        outs = []
        for ci in range(n_chunks):
            lf = lf_ref[rows(ci), :]
            q = q_ref[rows(ci), :].astype(F32)
            k = k_ref[rows(ci), :].astype(F32)
            v = v_ref[rows(ci), :]
            scores = _chunk_scores(q, k, b[ci], lf)
            o = jnp.dot(scores.astype(BF16), v, preferred_element_type=F32)
            o = o + _nt_dot(q_dec[ci], state_t.astype(BF16))
            k_dec = (k * jnp.exp(b[ci][CHUNK - 1:CHUNK, :] - b[ci])).astype(BF16)
            state_t = state_t * decay[ci] + _tn_dot(v, k_dec)
            outs.append(o)
        finish(outs)
        st_ref[...] = state_t


def _hg_rec(q, k, lf, v, sg, o_gain):
    bsz, n_heads, seq, _ = q.shape
    ts = _tile(seq, 2048)
    in_spec = pl.BlockSpec((None, None, ts, LANES), lambda b, h, s: (b, h, s, 0))
    return pl.pallas_call(
        functools.partial(_rec_kernel, n_chunks=ts // CHUNK),
        grid=(bsz, n_heads, seq // ts),
        in_specs=[in_spec] * 5 + [pl.BlockSpec((1, LANES), lambda b, h, s: (0, 0))],
        out_specs=pl.BlockSpec((None, ts, LANES), lambda b, h, s: (b, s, h)),
        out_shape=jax.ShapeDtypeStruct((bsz, seq, n_heads * LANES), BF16),
        scratch_shapes=[pltpu.VMEM((LANES, LANES), F32)],
        compiler_params=_params("parallel", "parallel", "arbitrary"),
        name="hg_rec",
    )(q, k, lf, v, sg, o_gain.reshape(1, LANES))


def _pair_sum_weights():
    row = lax.broadcasted_iota(jnp.int32, (QK_PAD, QK_PAD), 0)
    col = lax.broadcasted_iota(jnp.int32, (QK_PAD, QK_PAD), 1)
    return jnp.where((row < LANES) == (col < LANES), 1.0, 0.0).astype(BF16)


def _rope_pair(u):
    return u + pltpu.roll(u, ROPE, axis=1)


def _mla_kv_kernel(x_ref, gain_ref, shift_ref, scale_ref, wd_ref, lat_gain_ref, wu_ref,
                   gn_ref, gr_ref, cs_ref, k_ref, v_ref, *, n_heads, kv_lora):
    h = _norm_mod(x_ref[...], gain_ref[...], scale_ref[...], shift_ref[...]).astype(BF16)
    ckv = jnp.dot(h, wd_ref[...], preferred_element_type=F32)
    lat = ckv[:, :kv_lora]
    pe2 = ckv[:, kv_lora:]
    c_lat = (lat * lax.rsqrt(jnp.mean(lat * lat, axis=-1, keepdims=True) + EPS) * lat_gain_ref[...]).astype(BF16)
    pair_w = _pair_sum_weights()
    sq_pe = 0.5 * (pe2 * pe2)
    rot = _rope_pair(pe2 * gr_ref[...] * cs_ref[...])
    gn = gn_ref[...]

    def pair_kv(p):
        return [jnp.dot(c_lat, wu_ref[:, hh * QK_PAD:(hh + 1) * QK_PAD], preferred_element_type=F32)
                for hh in (2 * p, 2 * p + 1)]

    nxt = pair_kv(0)
    for p in range(n_heads // 2):
        kvs = nxt
        if p + 1 < n_heads // 2:
            nxt = pair_kv(p + 1)
        sq = jnp.concatenate([kv[:, :LANES] * kv[:, :LANES] + sq_pe for kv in kvs], axis=1)
        r2 = lax.rsqrt(jnp.dot(sq.astype(BF16), pair_w, preferred_element_type=F32) * (1.0 / QK) + EPS)
        for idx, kv in enumerate(kvs):
            hh = 2 * p + idx
            r = r2[:, idx * LANES:(idx + 1) * LANES]
            k_ref[hh, :, :LANES] = (kv[:, :LANES] * r * gn).astype(BF16)
            k_ref[hh, :, LANES:] = (rot * r).astype(BF16)
            v_ref[hh] = kv[:, LANES:].T.astype(BF16)


def _mla_kv(x, gain, mod, wd, lat_gain, wu, gn, gr, cs, bsz, seq):
    t, d = x.shape
    n_heads = d // LANES
    kv_lora = lat_gain.shape[0]
    tm = _tile(seq, ATTN_TILE)
    per_b = seq // tm
    mod_spec = lambda k: pl.BlockSpec((None, 1, d), lambda i: (i // per_b, 0, k))
    assert n_heads % 2 == 0, n_heads
    full = lambda a: pl.BlockSpec(a.shape, lambda i: (0,) * a.ndim)
    lat_gain = lat_gain.reshape(1, kv_lora)
    return pl.pallas_call(
        functools.partial(_mla_kv_kernel, n_heads=n_heads, kv_lora=kv_lora),
        grid=(t // tm,),
        in_specs=[
            pl.BlockSpec((tm, d), lambda i: (i, 0)),
            pl.BlockSpec((1, d), lambda i: (0, 0)),
            mod_spec(0), mod_spec(1),
            full(wd), full(lat_gain), full(wu), full(gn), full(gr),
            pl.BlockSpec((tm, LANES), lambda i: (i, 0)),
        ],
        out_specs=[
            pl.BlockSpec((None, n_heads, tm, QK_PAD), lambda i: (i // per_b, 0, i % per_b, 0)),
            pl.BlockSpec((None, n_heads, None, LANES, tm), lambda i: (i // per_b, 0, i % per_b, 0, 0)),
        ],
        out_shape=[jax.ShapeDtypeStruct((bsz, n_heads, seq, QK_PAD), BF16),
                   jax.ShapeDtypeStruct((bsz, n_heads, seq // tm, LANES, tm), BF16)],
        compiler_params=_params("parallel"),
        name="mla_kv",
    )(x, gain.reshape(1, d), mod, mod, wd, lat_gain, wu, gn, gr, cs)


def _mla_q_kernel(x_ref, gain_ref, shift_ref, scale_ref, wd_ref, lat_gain_ref, wu_ref,
                  gn_ref, gr_ref, cs_ref, q_ref, *, n_heads):
    h = _norm_mod(x_ref[...], gain_ref[...], scale_ref[...], shift_ref[...]).astype(BF16)
    cq = jnp.dot(h, wd_ref[...], preferred_element_type=F32)
    cq = (cq * lax.rsqrt(jnp.mean(cq * cq, axis=-1, keepdims=True) + EPS) * lat_gain_ref[...]).astype(BF16)
    gn = gn_ref[...]
    rope_tab = gr_ref[...] * cs_ref[...]
    pair_w = _pair_sum_weights()

    def pair_q(p):
        return [jnp.dot(cq, wu_ref[:, hh * QK_PAD:(hh + 1) * QK_PAD], preferred_element_type=F32)
                for hh in (2 * p, 2 * p + 1)]

    nxt = pair_q(0)
    for p in range(n_heads // 2):
        qhs = nxt
        if p + 1 < n_heads // 2:
            nxt = pair_q(p + 1)
        sq = jnp.concatenate([qh[:, :LANES] * qh[:, :LANES] + 0.5 * (qh[:, LANES:] * qh[:, LANES:])
                              for qh in qhs], axis=1)
        r2 = lax.rsqrt(jnp.dot(sq.astype(BF16), pair_w, preferred_element_type=F32) * (1.0 / QK) + EPS)
        for idx, qh in enumerate(qhs):
            hh = 2 * p + idx
            r = r2[:, idx * LANES:(idx + 1) * LANES]
            q_ref[hh, :LANES, :] = (qh[:, :LANES] * r * gn).T.astype(BF16)
            q_ref[hh, LANES:, :] = (qh[:, LANES:] * rope_tab * r).T.astype(BF16)


def _mla_q(x, gain, mod, wd, layer, lat_gain, wu, gn, gr, cs, bsz, seq):
    t, d = x.shape
    n_heads = d // LANES
    q_lora = lat_gain.shape[0]
    tm = _tile(seq, ATTN_TILE)
    per_b = seq // tm
    mod_spec = lambda k: pl.BlockSpec((None, 1, d), lambda i: (i // per_b, 0, k))
    assert n_heads % 2 == 0, n_heads
    full = lambda a: pl.BlockSpec(a.shape, lambda i: (0,) * a.ndim)
    lat_gain = lat_gain.reshape(1, q_lora)
    return pl.pallas_call(
        functools.partial(_mla_q_kernel, n_heads=n_heads),
        grid=(t // tm,),
        in_specs=[
            pl.BlockSpec((tm, d), lambda i: (i, 0)),
            pl.BlockSpec((1, d), lambda i: (0, 0)),
            mod_spec(0), mod_spec(1),
            pl.BlockSpec((None, d, q_lora), lambda i: (layer, 0, 0)),
            full(lat_gain), full(wu), full(gn), full(gr),
            pl.BlockSpec((tm, LANES), lambda i: (i, 0)),
        ],
        out_specs=pl.BlockSpec((None, n_heads, None, QK_PAD, tm), lambda i: (i // per_b, 0, i % per_b, 0, 0)),
        out_shape=jax.ShapeDtypeStruct((bsz, n_heads, seq // tm, QK_PAD, tm), BF16),
        compiler_params=_params("parallel"),
        name="mla_q",
    )(x, gain.reshape(1, d), mod, mod, wd, lat_gain, wu, gn, gr, cs)


def _flash_kernel(bound_ref, qt_ref, k_ref, vt_ref, o_ref, s_ref, m_ref, l_ref, acc_ref, *,
                  tq, n_tiles, heads_per_step):
    def tile_rows(j):
        return pl.ds(pl.multiple_of(j * tq, tq), tq)

    def scores(hh, i, j):
        return jnp.dot(k_ref[hh, tile_rows(j), :], qt_ref[hh, i], preferred_element_type=F32)

    bound = bound_ref[0, 0]

    def consume(hh, j, masked, fixed_shift):
        s = s_ref[hh]
        if masked:
            ki = lax.broadcasted_iota(jnp.int32, (tq, tq), 0)
            qi = lax.broadcasted_iota(jnp.int32, (tq, tq), 1)
            s = jnp.where(ki <= qi, s, -jnp.inf)
        if fixed_shift:
            p = jnp.exp2(s)
            l_ref[hh] += jnp.sum(p, axis=0, keepdims=True)
            acc_ref[hh] += jnp.dot(vt_ref[hh, j], p.astype(BF16), preferred_element_type=F32)
            return
        m = m_ref[hh]
        m_new = jnp.maximum(m, jnp.max(s, axis=0, keepdims=True))
        alpha = jnp.exp2(m - m_new)
        p = jnp.exp2(s - m_new)
        m_ref[hh] = m_new
        l_ref[hh] = alpha * l_ref[hh] + jnp.sum(p, axis=0, keepdims=True)
        pv = jnp.dot(vt_ref[hh, j], p.astype(BF16), preferred_element_type=F32)
        acc_ref[hh] = alpha * acc_ref[hh] + pv

    def attend(fixed_shift):
        for hh in range(heads_per_step):
            s_ref[hh] = scores(hh, 0, 0)

        def query_tile(i, carry):
            m_ref[...] = jnp.full(m_ref.shape, -jnp.inf, F32)
            l_ref[...] = jnp.zeros_like(l_ref)
            acc_ref[...] = jnp.zeros_like(acc_ref)

            def key_tile(j, c):
                for hh in range(heads_per_step):
                    s_next = scores(hh, i, j + 1)
                    consume(hh, j, False, fixed_shift)
                    s_ref[hh] = s_next
                return c

            lax.fori_loop(0, i, key_tile, 0)
            i_next = jnp.minimum(i + 1, n_tiles - 1)
            for hh in range(heads_per_step):
                s_next = scores(hh, i_next, 0)
                consume(hh, i, True, fixed_shift)
                s_ref[hh] = s_next
                o_ref[tile_rows(i), hh * LANES:(hh + 1) * LANES] = (acc_ref[hh] / l_ref[hh]).T.astype(BF16)
            return carry

        lax.fori_loop(0, n_tiles, query_tile, 0)

    @pl.when(bound <= MAX_FIXED_SOFTMAX_SHIFT)
    def _():
        attend(True)

    @pl.when(jnp.logical_not(bound <= MAX_FIXED_SOFTMAX_SHIFT))
    def _():
        attend(False)


def _flash(qt, k, vt, score_bound):
    bsz, n_heads, seq, _ = k.shape
    tq = vt.shape[-1]
    hps = 4 if n_heads % 4 == 0 else (2 if n_heads % 2 == 0 else 1)
    return pl.pallas_call(
        functools.partial(_flash_kernel, tq=tq, n_tiles=seq // tq, heads_per_step=hps),
        grid=(bsz, n_heads // hps),
        in_specs=[
            pl.BlockSpec(memory_space=pltpu.SMEM),
            pl.BlockSpec((None, hps, seq // tq, QK_PAD, tq), lambda b, h: (b, h, 0, 0, 0)),
            pl.BlockSpec((None, hps, seq, QK_PAD), lambda b, h: (b, h, 0, 0)),
            pl.BlockSpec((None, hps, seq // tq, LANES, tq), lambda b, h: (b, h, 0, 0, 0)),
        ],
        out_specs=pl.BlockSpec((None, seq, hps * LANES), lambda b, h: (b, 0, h)),
        out_shape=jax.ShapeDtypeStruct((bsz, seq, n_heads * LANES), BF16),
        scratch_shapes=[pltpu.VMEM((hps, tq, tq), F32),
                        pltpu.VMEM((hps, 1, tq), F32), pltpu.VMEM((hps, 1, tq), F32),
                        pltpu.VMEM((hps, LANES, tq), F32)],
        compiler_params=_params("parallel", "parallel"),
        name="flash",
    )(score_bound, qt, k, vt)


def _swap_halves(a):
    half = a.shape[-1] // 2
    return jnp.concatenate([a[..., half:], a[..., :half]], axis=-1)


def _rope_gain(g):
    return jnp.concatenate([g, _swap_halves(g)]).reshape(1, LANES)


def kernel(x, c, positions, ada_w, ada_b, norm_mix, norm_mlp, mlp_w1, mlp_w2, hg_w_in, hg_lower, hg_o_norm, hg_w_out, kv_ada_w, kv_ada_b, kv_in_norm, mla_w_dkv, mla_kv_norm, mla_w_ukv, mla_k_norm, mla_w_dq, mla_q_lat_norm, mla_w_uq, mla_q_norm, mla_w_o):
    bsz, seq, d = x.shape
    depth = ada_w.shape[0]
    n_a = hg_w_in.shape[0]
    n_heads = d // LANES
    kv_lora = mla_kv_norm.shape[0]
    q_lora = mla_w_dq.shape[2]
    t = bsz * seq

    mod = _ada(c, ada_w, ada_b).reshape(depth, bsz, 1, 6 * d)
    kv_mod = _ada(c, kv_ada_w[None], kv_ada_b[None]).reshape(bsz, 1, 2 * d)

    inv_freq = 1.0 / (ROPE_THETA ** (jnp.arange(0, ROPE, 2, dtype=F32) / ROPE))
    ang = positions.astype(F32)[..., None] * inv_freq
    cos, sin = jnp.cos(ang), jnp.sin(ang)
    cs = jnp.concatenate([cos, cos, -sin, sin], axis=-1).reshape(t, LANES)

    lb_all = jnp.cumsum(jax.nn.softmax(hg_lower.astype(F32), axis=0), axis=0)[:n_a]

    hg_w_in, hg_w_out, mla_w_dq, mla_w_o, mlp_w1, mlp_w2 = (
        w.astype(BF16) for w in (hg_w_in, hg_w_out, mla_w_dq, mla_w_o, mlp_w1, mlp_w2))

    xf = x.reshape(t, d)
    k_sh = v_sh = None
    for layer in range(depth):
        if layer == n_a:
            wd = jnp.concatenate([mla_w_dkv, _swap_halves(mla_w_dkv[:, kv_lora:])], axis=1).astype(BF16)
            k_sh, v_sh = _mla_kv(xf, kv_in_norm, kv_mod, wd, mla_kv_norm, mla_w_ukv.astype(BF16),
                                 mla_k_norm[:LANES].reshape(1, LANES), _rope_gain(mla_k_norm[LANES:]),
                                 cs, bsz, seq)
        if layer < n_a:
            q, k, lf, v, sg = _hg_in(xf, norm_mix[layer], mod[layer], lb_all[layer],
                                     hg_w_in, layer, bsz, seq)
            y = _hg_rec(q, k, lf, v, sg, hg_o_norm[layer]).reshape(t, d)
            xf = _proj_res(y, hg_w_out, layer, xf, mod[layer], 2, seq)
        else:
            j = layer - n_a
            wu = mla_w_uq[j].reshape(q_lora, n_heads, QK)
            wu = jnp.concatenate([wu, _swap_halves(wu[..., LANES:])], axis=-1)
            wu = wu.reshape(q_lora, n_heads * QK_PAD).astype(BF16)
            sm_scale = float(QK) ** -0.5 * LOG2E
            qg = mla_q_norm[j] * sm_scale
            qh = _mla_q(xf, norm_mix[layer], mod[layer], mla_w_dq, j, mla_q_lat_norm[j], wu,
                        qg[:LANES].reshape(1, LANES), _rope_gain(qg[LANES:]), cs, bsz, seq)
            score_bound = (1.03 * QK * 2.0 ** 0.5) * jnp.max(jnp.abs(qg)) * jnp.max(jnp.abs(mla_k_norm))
            y = _flash(qh, k_sh, v_sh, score_bound.reshape(1, 1).astype(F32)).reshape(t, d)
            xf = _proj_res(y, mla_w_o, j, xf, mod[layer], 2, seq)
        xf = _mlp(xf, norm_mlp[layer], mod[layer], mlp_w1, mlp_w2, layer, seq)
    return xf.reshape(bsz, seq, d)
```

```python
import functools

import jax
import jax.numpy as jnp
from jax import lax
from jax.experimental import pallas as pl
from jax.experimental.pallas import tpu as pltpu

F32 = jnp.float32
BF16 = jnp.bfloat16

EPS = 1e-6
LANES = 128
ROPE = 64
QK = LANES + ROPE
QK_PAD = 2 * LANES
CHUNK = 64
ATTN_TILE = 512
MAX_FIXED_SOFTMAX_SHIFT = 40.0
MAX_FACTORED_DECAY = 72.0
ROPE_THETA = 10000.0
LOG2E = 1.4426950408889634
VMEM_LIMIT = 56 * 1024 * 1024


def _params(*sem):
    return pltpu.CompilerParams(dimension_semantics=sem, vmem_limit_bytes=VMEM_LIMIT)


def _tile(n, pref):
    t = min(n, pref)
    while n % t:
        t -= LANES
    assert t > 0 and n % t == 0, (n, pref)
    return t


def _norm_mod(x, gain, scale, shift):
    r = lax.rsqrt(jnp.mean(x * x, axis=-1, keepdims=True) + EPS)
    return (x * r) * (gain * (1.0 + scale)) + shift


def _ada_kernel(c_ref, w_ref, b_ref, o_ref):
    c = c_ref[...]
    cond = (c / (1.0 + jnp.exp(-c))).astype(BF16)
    o_ref[...] = jnp.dot(cond, w_ref[...].astype(BF16), preferred_element_type=F32) + b_ref[...]


def _ada(c, w, b):
    n_layers, d, n = w.shape
    bsz = c.shape[0]
    tn = _tile(n, 1024)
    return pl.pallas_call(
        _ada_kernel,
        grid=(n_layers, n // tn),
        in_specs=[
            pl.BlockSpec((bsz, d), lambda l, j: (0, 0)),
            pl.BlockSpec((None, d, tn), lambda l, j: (l, 0, j)),
            pl.BlockSpec((None, 1, tn), lambda l, j: (l, 0, j)),
        ],
        out_specs=pl.BlockSpec((None, bsz, tn), lambda l, j: (l, 0, j)),
        out_shape=jax.ShapeDtypeStruct((n_layers, bsz, n), F32),
        compiler_params=_params("parallel", "parallel"),
        name="ada",
    )(c, w, b.reshape(n_layers, 1, n))


def _mlp_kernel(x_ref, gain_ref, shift_ref, scale_ref, gate_ref, w1_ref, w2_ref, o_ref, h_ref, *, n_sub):
    j = pl.program_id(1)
    sub = w1_ref.shape[1] // n_sub

    def gated_piece(c):
        cols = slice(c * sub, (c + 1) * sub)
        a = jnp.maximum(jnp.dot(h_ref[...], w1_ref[:, cols], preferred_element_type=F32), 0.0)
        return gate_ref[...] * jnp.dot((a * a).astype(BF16), w2_ref[cols, :], preferred_element_type=F32)

    @pl.when(j == 0)
    def _():
        h_ref[...] = _norm_mod(x_ref[...], gain_ref[...], scale_ref[...], shift_ref[...]).astype(BF16)
        o_ref[...] = x_ref[...] + gated_piece(0)
        for c in range(1, n_sub):
            o_ref[...] += gated_piece(c)

    @pl.when(j > 0)
    def _():
        for c in range(n_sub):
            o_ref[...] += gated_piece(c)


def _mlp(x, gain, mod, w1, w2, layer, seq):
    t, d = x.shape
    f = w1.shape[2]
    tm = _tile(seq, 1024)
    tf = _tile(f, 1024)
    n_sub = 2 if tf % (2 * LANES) == 0 else 1
    per_b = seq // tm
    mod_spec = lambda k: pl.BlockSpec((None, 1, d), lambda i, j: (i // per_b, 0, k))
    return pl.pallas_call(
        functools.partial(_mlp_kernel, n_sub=n_sub),
        grid=(t // tm, f // tf),
        in_specs=[
            pl.BlockSpec((tm, d), lambda i, j: (i, 0)),
            pl.BlockSpec((1, d), lambda i, j: (0, 0)),
            mod_spec(3), mod_spec(4), mod_spec(5),
            pl.BlockSpec((None, d, tf), lambda i, j: (layer, 0, j)),
            pl.BlockSpec((None, tf, d), lambda i, j: (layer, j, 0)),
        ],
        out_specs=pl.BlockSpec((tm, d), lambda i, j: (i, 0)),
        out_shape=jax.ShapeDtypeStruct((t, d), F32),
        scratch_shapes=[pltpu.VMEM((tm, d), BF16)],
        compiler_params=_params("parallel", "arbitrary"),
        name="mlp",
    )(x, gain.reshape(1, d), mod, mod, mod, w1, w2)


def _proj_res_kernel(y_ref, w_ref, x_ref, gate_ref, o_ref):
    o_ref[...] = x_ref[...] + gate_ref[...] * jnp.dot(y_ref[...], w_ref[...], preferred_element_type=F32)


def _proj_res(y, w, layer, x, mod, gate_idx, seq):
    t, d = x.shape
    k = y.shape[1]
    tm = _tile(seq, 512)
    tn = _tile(d, 2048)
    per_b = seq // tm
    return pl.pallas_call(
        _proj_res_kernel,
        grid=(t // tm, d // tn),
        in_specs=[
            pl.BlockSpec((tm, k), lambda i, j: (i, 0)),
            pl.BlockSpec((None, k, tn), lambda i, j: (layer, 0, j)),
            pl.BlockSpec((tm, tn), lambda i, j: (i, j)),
            pl.BlockSpec((None, 1, tn), lambda i, j: (i // per_b, 0, gate_idx * (d // tn) + j)),
        ],
        out_specs=pl.BlockSpec((tm, tn), lambda i, j: (i, j)),
        out_shape=jax.ShapeDtypeStruct((t, d), F32),
        compiler_params=_params("parallel", "parallel"),
        name="proj_res",
    )(y, w, x, mod)


def _hg_in_kernel(x_ref, gain_ref, shift_ref, scale_ref, llb_ref, l1mlb_ref, omlb_ref,
                  wq_ref, wf_ref, wi_ref, wg_ref,
                  q_ref, k_ref, lf_ref, v_ref, sg_ref, h_ref, *, heads_per_step, heads_per_piece, q_scale):
    def project_piece(first_head):
        cols = slice(first_head * LANES, (first_head + heads_per_piece) * LANES)
        h = h_ref[...]
        z = jnp.dot(h, wf_ref[:, cols], preferred_element_type=F32)
        q = jnp.dot(h, wq_ref[:, cols], preferred_element_type=F32) * q_scale
        v = jnp.dot(h, wi_ref[:, cols], preferred_element_type=F32)
        g = jnp.dot(h, wg_ref[:, cols], preferred_element_type=F32)

        e = jnp.exp(-jnp.abs(z))
        log_sig = jnp.minimum(z, 0.0) - jnp.log(1.0 + e)
        a = llb_ref[:, cols]
        c = l1mlb_ref[:, cols] + log_sig
        lf = jnp.maximum(a, c) + jnp.log(1.0 + jnp.exp(-jnp.abs(a - c)))
        key = omlb_ref[:, cols] * (jnp.where(z >= 0.0, e, 1.0) / (1.0 + e))
        sg = g / (1.0 + jnp.exp(-g))

        for hh in range(heads_per_piece):
            cs = slice(hh * LANES, (hh + 1) * LANES)
            q_ref[first_head + hh] = q[:, cs].astype(BF16)
            k_ref[first_head + hh] = key[:, cs].astype(BF16)
            lf_ref[first_head + hh] = lf[:, cs]
            v_ref[first_head + hh] = v[:, cs].astype(BF16)
            sg_ref[first_head + hh] = sg[:, cs].astype(BF16)

    def project():
        for first_head in range(0, heads_per_step, heads_per_piece):
            project_piece(first_head)

    @pl.when(pl.program_id(1) == 0)
    def _():
        h_ref[...] = _norm_mod(x_ref[...], gain_ref[...], scale_ref[...], shift_ref[...]).astype(BF16)
        project()

    @pl.when(pl.program_id(1) > 0)
    def _():
        project()


def _hg_in(x, gain, mod, lb, w_in, layer, bsz, seq):
    t, d = x.shape
    n_heads = d // LANES
    tm = _tile(seq, 1024)
    tn = _tile(d, 4 * LANES)
    hps = tn // LANES
    hpp = 2 if hps % 2 == 0 else 1
    nj = d // tn
    per_b = seq // tm
    mod_spec = lambda k: pl.BlockSpec((None, 1, d), lambda i, j: (i // per_b, 0, k))
    vec_spec = pl.BlockSpec((1, tn), lambda i, j: (0, j))
    w_spec = lambda grp: pl.BlockSpec((None, d, tn), lambda i, j: (layer, 0, grp * nj + j))
    out_spec = pl.BlockSpec((None, hps, tm, LANES), lambda i, j: (i // per_b, j, i % per_b, 0))
    hm = lambda dt: jax.ShapeDtypeStruct((bsz, n_heads, seq, LANES), dt)
    lbf = lb.reshape(1, d)
    kern = functools.partial(_hg_in_kernel, heads_per_step=hps, heads_per_piece=hpp, q_scale=float(LANES) ** -0.5)
    return pl.pallas_call(
        kern,
        grid=(t // tm, nj),
        in_specs=[
            pl.BlockSpec((tm, d), lambda i, j: (i, 0)),
            pl.BlockSpec((1, d), lambda i, j: (0, 0)),
            mod_spec(0), mod_spec(1),
            vec_spec, vec_spec, vec_spec,
            w_spec(0), w_spec(1), w_spec(2), w_spec(3),
        ],
        out_specs=[out_spec] * 5,
        out_shape=[hm(BF16), hm(BF16), hm(F32), hm(BF16), hm(BF16)],
        scratch_shapes=[pltpu.VMEM((tm, d), BF16)],
        compiler_params=_params("parallel", "arbitrary"),
        name="hg_in",
    )(x, gain.reshape(1, d), mod, mod, jnp.log(lbf), jnp.log1p(-lbf), 1.0 - lbf, w_in, w_in, w_in, w_in)


def _nt_dot(a, b):
    return lax.dot_general(a, b, (((1,), (1,)), ((), ())), preferred_element_type=F32)


def _tn_dot(a, b):
    return lax.dot_general(a, b, (((0,), (0,)), ((), ())), preferred_element_type=F32)


def _chunk_scores(q, k, b, lf):
    row = lax.broadcasted_iota(jnp.int32, (CHUNK, LANES), 0)
    ti = lax.broadcasted_iota(jnp.int32, (CHUNK, CHUNK), 0)
    si = lax.broadcasted_iota(jnp.int32, (CHUNK, CHUNK), 1)
    scores = jnp.where(ti == si, _nt_dot(q.astype(BF16), k.astype(BF16)), 0.0)
    start = b - lf
    c = 1
    while c < CHUNK:
        later = (row & c) != 0
        bound = jnp.where(later, start, pltpu.roll(start, CHUNK - c, axis=0))
        e = jnp.exp(jnp.where(later, b - bound, bound - b))
        m = jnp.where(later, q, k) * e
        qs = jnp.where(later, m, 0.0).astype(BF16)
        ks = jnp.where(later, 0.0, m).astype(BF16)
        same_parent = (ti & -(2 * c)) == (si & -(2 * c))
        scores = scores + jnp.where(same_parent, _nt_dot(qs, ks), 0.0)
        if 2 * c < CHUNK:
            start = jnp.where(later, pltpu.roll(start, c, axis=0), start)
        c *= 2
    return scores


def _cumsum_rows(lf, tri):
    hi = lf.astype(BF16)
    r1 = lf - hi.astype(F32)
    mid = r1.astype(BF16)
    lo = (r1 - mid.astype(F32)).astype(BF16)
    dot = lambda part: jnp.dot(tri, part, preferred_element_type=F32)
    return dot(hi) + dot(mid) + dot(lo)


def _rec_kernel(q_ref, k_ref, lf_ref, v_ref, sg_ref, og_ref, o_ref, st_ref, *, n_chunks):
    @pl.when(pl.program_id(2) == 0)
    def _():
        st_ref[...] = jnp.zeros_like(st_ref)

    ti = lax.broadcasted_iota(jnp.int32, (CHUNK, CHUNK), 0)
    si = lax.broadcasted_iota(jnp.int32, (CHUNK, CHUNK), 1)
    causal = si <= ti
    tri = jnp.where(causal, 1.0, 0.0).astype(BF16)
    og = og_ref[...]

    def rows(ci):
        return slice(ci * CHUNK, (ci + 1) * CHUNK)

    mean_w = jnp.full((LANES, LANES), 1.0 / LANES, BF16)

    def finish(outs):
        ms = [jnp.dot((o * o).astype(BF16), mean_w, preferred_element_type=F32) for o in outs]
        for ci, o in enumerate(outs):
            o = o * lax.rsqrt(ms[ci] + EPS) * og
            o_ref[rows(ci), :] = (o * sg_ref[rows(ci), :].astype(F32)).astype(BF16)

    cs = range(n_chunks)
    b = [_cumsum_rows(lf_ref[rows(ci), :], tri) for ci in cs]
    total = b[0][CHUNK - 1:CHUNK, :]
    for ci in range(1, n_chunks):
        total = jnp.minimum(total, b[ci][CHUNK - 1:CHUNK, :])
    mild = jnp.min(total) >= -MAX_FACTORED_DECAY
    q_dec = [(q_ref[rows(ci), :].astype(F32) * jnp.exp(b[ci])).astype(BF16) for ci in cs]
    decay = [jnp.exp(b[ci][CHUNK - 1:CHUNK, :]) for ci in cs]
    k_grow = [k_ref[rows(ci), :].astype(F32) * jnp.exp(-b[ci]) for ci in cs]
    scores = [jnp.where(causal, jnp.dot(q_dec[ci], k_grow[ci].T.astype(BF16), preferred_element_type=F32),
                        0.0).astype(BF16) for ci in cs]
    intra = [jnp.dot(scores[ci], v_ref[rows(ci), :], preferred_element_type=F32) for ci in cs]

    @pl.when(mild)
    def _():
        update = [_tn_dot(v_ref[rows(ci), :], (k_grow[ci] * decay[ci]).astype(BF16)) for ci in cs]
        states = [st_ref[...]]
        for ci in cs:
            states.append(states[ci] * decay[ci] + update[ci])
        finish([intra[ci] + jnp.dot(q_dec[ci], states[ci].T.astype(BF16), preferred_element_type=F32)
                for ci in cs])
        st_ref[...] = states[n_chunks]

    @pl.when(jnp.logical_not(mild))
    def _():
        state_t = st_ref[...]
        outs = []
        for ci in range(n_chunks):
            lf = lf_ref[rows(ci), :]
            q = q_ref[rows(ci), :].astype(F32)
            k = k_ref[rows(ci), :].astype(F32)
            v = v_ref[rows(ci), :]
            scores = _chunk_scores(q, k, b[ci], lf)
            o = jnp.dot(scores.astype(BF16), v, preferred_element_type=F32)
            o = o + _nt_dot(q_dec[ci], state_t.astype(BF16))
            k_dec = (k * jnp.exp(b[ci][CHUNK - 1:CHUNK, :] - b[ci])).astype(BF16)
            state_t = state_t * decay[ci] + _tn_dot(v, k_dec)
            outs.append(o)
        finish(outs)
        st_ref[...] = state_t


def _hg_rec(q, k, lf, v, sg, o_gain):
    bsz, n_heads, seq, _ = q.shape
    ts = _tile(seq, 2048)
    in_spec = pl.BlockSpec((None, None, ts, LANES), lambda b, h, s: (b, h, s, 0))
    return pl.pallas_call(
        functools.partial(_rec_kernel, n_chunks=ts // CHUNK),
        grid=(bsz, n_heads, seq // ts),
        in_specs=[in_spec] * 5 + [pl.BlockSpec((1, LANES), lambda b, h, s: (0, 0))],
        out_specs=pl.BlockSpec((None, ts, LANES), lambda b, h, s: (b, s, h)),
        out_shape=jax.ShapeDtypeStruct((bsz, seq, n_heads * LANES), BF16),
        scratch_shapes=[pltpu.VMEM((LANES, LANES), F32)],
        compiler_params=_params("parallel", "parallel", "arbitrary"),
        name="hg_rec",
    )(q, k, lf, v, sg, o_gain.reshape(1, LANES))


def _pair_sum_weights():
    row = lax.broadcasted_iota(jnp.int32, (QK_PAD, QK_PAD), 0)
    col = lax.broadcasted_iota(jnp.int32, (QK_PAD, QK_PAD), 1)
    return jnp.where((row < LANES) == (col < LANES), 1.0, 0.0).astype(BF16)


def _rope_pair(u):
    return u + pltpu.roll(u, ROPE, axis=1)


def _mla_kv_kernel(x_ref, gain_ref, shift_ref, scale_ref, wd_ref, lat_gain_ref, wu_ref,
                   gn_ref, gr_ref, cs_ref, k_ref, v_ref, *, n_heads, kv_lora):
    h = _norm_mod(x_ref[...], gain_ref[...], scale_ref[...], shift_ref[...]).astype(BF16)
    ckv = jnp.dot(h, wd_ref[...], preferred_element_type=F32)
    lat = ckv[:, :kv_lora]
    pe2 = ckv[:, kv_lora:]
    c_lat = (lat * lax.rsqrt(jnp.mean(lat * lat, axis=-1, keepdims=True) + EPS) * lat_gain_ref[...]).astype(BF16)
    pair_w = _pair_sum_weights()
    sq_pe = 0.5 * (pe2 * pe2)
    rot = _rope_pair(pe2 * gr_ref[...] * cs_ref[...])
    gn = gn_ref[...]

    def pair_kv(p):
        return [jnp.dot(c_lat, wu_ref[:, hh * QK_PAD:(hh + 1) * QK_PAD], preferred_element_type=F32)
                for hh in (2 * p, 2 * p + 1)]

    nxt = pair_kv(0)
    for p in range(n_heads // 2):
        kvs = nxt
        if p + 1 < n_heads // 2:
            nxt = pair_kv(p + 1)
        sq = jnp.concatenate([kv[:, :LANES] * kv[:, :LANES] + sq_pe for kv in kvs], axis=1)
        r2 = lax.rsqrt(jnp.dot(sq.astype(BF16), pair_w, preferred_element_type=F32) * (1.0 / QK) + EPS)
        for idx, kv in enumerate(kvs):
            hh = 2 * p + idx
            r = r2[:, idx * LANES:(idx + 1) * LANES]
            k_ref[hh, :, :LANES] = (kv[:, :LANES] * r * gn).astype(BF16)
            k_ref[hh, :, LANES:] = (rot * r).astype(BF16)
            v_ref[hh] = kv[:, LANES:].T.astype(BF16)


def _mla_kv(x, gain, mod, wd, lat_gain, wu, gn, gr, cs, bsz, seq):
    t, d = x.shape
    n_heads = d // LANES
    kv_lora = lat_gain.shape[0]
    tm = _tile(seq, ATTN_TILE)
    per_b = seq // tm
    mod_spec = lambda k: pl.BlockSpec((None, 1, d), lambda i: (i // per_b, 0, k))
    assert n_heads % 2 == 0, n_heads
    full = lambda a: pl.BlockSpec(a.shape, lambda i: (0,) * a.ndim)
    lat_gain = lat_gain.reshape(1, kv_lora)
    return pl.pallas_call(
        functools.partial(_mla_kv_kernel, n_heads=n_heads, kv_lora=kv_lora),
        grid=(t // tm,),
        in_specs=[
            pl.BlockSpec((tm, d), lambda i: (i, 0)),
            pl.BlockSpec((1, d), lambda i: (0, 0)),
            mod_spec(0), mod_spec(1),
            full(wd), full(lat_gain), full(wu), full(gn), full(gr),
            pl.BlockSpec((tm, LANES), lambda i: (i, 0)),
        ],
        out_specs=[
            pl.BlockSpec((None, n_heads, tm, QK_PAD), lambda i: (i // per_b, 0, i % per_b, 0)),
            pl.BlockSpec((None, n_heads, None, LANES, tm), lambda i: (i // per_b, 0, i % per_b, 0, 0)),
        ],
        out_shape=[jax.ShapeDtypeStruct((bsz, n_heads, seq, QK_PAD), BF16),
                   jax.ShapeDtypeStruct((bsz, n_heads, seq // tm, LANES, tm), BF16)],
        compiler_params=_params("parallel"),
        name="mla_kv",
    )(x, gain.reshape(1, d), mod, mod, wd, lat_gain, wu, gn, gr, cs)


def _mla_q_kernel(x_ref, gain_ref, shift_ref, scale_ref, wd_ref, lat_gain_ref, wu_ref,
                  gn_ref, gr_ref, cs_ref, q_ref, *, n_heads):
    h = _norm_mod(x_ref[...], gain_ref[...], scale_ref[...], shift_ref[...]).astype(BF16)
    cq = jnp.dot(h, wd_ref[...], preferred_element_type=F32)
    cq = (cq * lax.rsqrt(jnp.mean(cq * cq, axis=-1, keepdims=True) + EPS) * lat_gain_ref[...]).astype(BF16)
    gn = gn_ref[...]
    rope_tab = gr_ref[...] * cs_ref[...]
    pair_w = _pair_sum_weights()

    def pair_q(p):
        return [jnp.dot(cq, wu_ref[:, hh * QK_PAD:(hh + 1) * QK_PAD], preferred_element_type=F32)
                for hh in (2 * p, 2 * p + 1)]

    nxt = pair_q(0)
    for p in range(n_heads // 2):
        qhs = nxt
        if p + 1 < n_heads // 2:
            nxt = pair_q(p + 1)
        sq = jnp.concatenate([qh[:, :LANES] * qh[:, :LANES] + 0.5 * (qh[:, LANES:] * qh[:, LANES:])
                              for qh in qhs], axis=1)
        r2 = lax.rsqrt(jnp.dot(sq.astype(BF16), pair_w, preferred_element_type=F32) * (1.0 / QK) + EPS)
        for idx, qh in enumerate(qhs):
            hh = 2 * p + idx
            r = r2[:, idx * LANES:(idx + 1) * LANES]
            q_ref[hh, :LANES, :] = (qh[:, :LANES] * r * gn).T.astype(BF16)
            q_ref[hh, LANES:, :] = (qh[:, LANES:] * rope_tab * r).T.astype(BF16)


def _mla_q(x, gain, mod, wd, layer, lat_gain, wu, gn, gr, cs, bsz, seq):
    t, d = x.shape
    n_heads = d // LANES
    q_lora = lat_gain.shape[0]
    tm = _tile(seq, ATTN_TILE)
    per_b = seq // tm
    mod_spec = lambda k: pl.BlockSpec((None, 1, d), lambda i: (i // per_b, 0, k))
    assert n_heads % 2 == 0, n_heads
    full = lambda a: pl.BlockSpec(a.shape, lambda i: (0,) * a.ndim)
    lat_gain = lat_gain.reshape(1, q_lora)
    return pl.pallas_call(
        functools.partial(_mla_q_kernel, n_heads=n_heads),
        grid=(t // tm,),
        in_specs=[
            pl.BlockSpec((tm, d), lambda i: (i, 0)),
            pl.BlockSpec((1, d), lambda i: (0, 0)),
            mod_spec(0), mod_spec(1),
            pl.BlockSpec((None, d, q_lora), lambda i: (layer, 0, 0)),
            full(lat_gain), full(wu), full(gn), full(gr),
            pl.BlockSpec((tm, LANES), lambda i: (i, 0)),
        ],
        out_specs=pl.BlockSpec((None, n_heads, None, QK_PAD, tm), lambda i: (i // per_b, 0, i % per_b, 0, 0)),
        out_shape=jax.ShapeDtypeStruct((bsz, n_heads, seq // tm, QK_PAD, tm), BF16),
        compiler_params=_params("parallel"),
        name="mla_q",
    )(x, gain.reshape(1, d), mod, mod, wd, lat_gain, wu, gn, gr, cs)


def _flash_kernel(bound_ref, qt_ref, k_ref, vt_ref, o_ref, s_ref, m_ref, l_ref, acc_ref, *,
                  tq, n_tiles, heads_per_step):
    def tile_rows(j):
        return pl.ds(pl.multiple_of(j * tq, tq), tq)

    def scores(hh, i, j):
        return jnp.dot(k_ref[hh, tile_rows(j), :], qt_ref[hh, i], preferred_element_type=F32)

    bound = bound_ref[0, 0]

    def consume(hh, j, masked, fixed_shift):
        s = s_ref[hh]
        if masked:
            ki = lax.broadcasted_iota(jnp.int32, (tq, tq), 0)
            qi = lax.broadcasted_iota(jnp.int32, (tq, tq), 1)
            s = jnp.where(ki <= qi, s, -jnp.inf)
        if fixed_shift:
            p = jnp.exp2(s)
            l_ref[hh] += jnp.sum(p, axis=0, keepdims=True)
            acc_ref[hh] += jnp.dot(vt_ref[hh, j], p.astype(BF16), preferred_element_type=F32)
            return
        m = m_ref[hh]
        m_new = jnp.maximum(m, jnp.max(s, axis=0, keepdims=True))
        alpha = jnp.exp2(m - m_new)
        p = jnp.exp2(s - m_new)
        m_ref[hh] = m_new
        l_ref[hh] = alpha * l_ref[hh] + jnp.sum(p, axis=0, keepdims=True)
        pv = jnp.dot(vt_ref[hh, j], p.astype(BF16), preferred_element_type=F32)
        acc_ref[hh] = alpha * acc_ref[hh] + pv

    def attend(fixed_shift):
        for hh in range(heads_per_step):
            s_ref[hh] = scores(hh, 0, 0)

        def query_tile(i, carry):
            m_ref[...] = jnp.full(m_ref.shape, -jnp.inf, F32)
            l_ref[...] = jnp.zeros_like(l_ref)
            acc_ref[...] = jnp.zeros_like(acc_ref)

            def key_tile(j, c):
                for hh in range(heads_per_step):
                    s_next = scores(hh, i, j + 1)
                    consume(hh, j, False, fixed_shift)
                    s_ref[hh] = s_next
                return c

            lax.fori_loop(0, i, key_tile, 0)
            i_next = jnp.minimum(i + 1, n_tiles - 1)
            for hh in range(heads_per_step):
                s_next = scores(hh, i_next, 0)
                consume(hh, i, True, fixed_shift)
                s_ref[hh] = s_next
                o_ref[tile_rows(i), hh * LANES:(hh + 1) * LANES] = (acc_ref[hh] / l_ref[hh]).T.astype(BF16)
            return carry

        lax.fori_loop(0, n_tiles, query_tile, 0)

    @pl.when(bound <= MAX_FIXED_SOFTMAX_SHIFT)
    def _():
        attend(True)

    @pl.when(jnp.logical_not(bound <= MAX_FIXED_SOFTMAX_SHIFT))
    def _():
        attend(False)


def _flash(qt, k, vt, score_bound):
    bsz, n_heads, seq, _ = k.shape
    tq = vt.shape[-1]
    hps = 4 if n_heads % 4 == 0 else (2 if n_heads % 2 == 0 else 1)
    return pl.pallas_call(
        functools.partial(_flash_kernel, tq=tq, n_tiles=seq // tq, heads_per_step=hps),
        grid=(bsz, n_heads // hps),
        in_specs=[
            pl.BlockSpec(memory_space=pltpu.SMEM),
            pl.BlockSpec((None, hps, seq // tq, QK_PAD, tq), lambda b, h: (b, h, 0, 0, 0)),
            pl.BlockSpec((None, hps, seq, QK_PAD), lambda b, h: (b, h, 0, 0)),
            pl.BlockSpec((None, hps, seq // tq, LANES, tq), lambda b, h: (b, h, 0, 0, 0)),
        ],
        out_specs=pl.BlockSpec((None, seq, hps * LANES), lambda b, h: (b, 0, h)),
        out_shape=jax.ShapeDtypeStruct((bsz, seq, n_heads * LANES), BF16),
        scratch_shapes=[pltpu.VMEM((hps, tq, tq), F32),
                        pltpu.VMEM((hps, 1, tq), F32), pltpu.VMEM((hps, 1, tq), F32),
                        pltpu.VMEM((hps, LANES, tq), F32)],
        compiler_params=_params("parallel", "parallel"),
        name="flash",
    )(score_bound, qt, k, vt)


def _swap_halves(a):
    half = a.shape[-1] // 2
    return jnp.concatenate([a[..., half:], a[..., :half]], axis=-1)


def _rope_gain(g):
    return jnp.concatenate([g, _swap_halves(g)]).reshape(1, LANES)


def kernel(x, c, positions, ada_w, ada_b, norm_mix, norm_mlp, mlp_w1, mlp_w2, hg_w_in, hg_lower, hg_o_norm, hg_w_out, kv_ada_w, kv_ada_b, kv_in_norm, mla_w_dkv, mla_kv_norm, mla_w_ukv, mla_k_norm, mla_w_dq, mla_q_lat_norm, mla_w_uq, mla_q_norm, mla_w_o):
    bsz, seq, d = x.shape
    depth = ada_w.shape[0]
    n_a = hg_w_in.shape[0]
    n_heads = d // LANES
    kv_lora = mla_kv_norm.shape[0]
    q_lora = mla_w_dq.shape[2]
    t = bsz * seq

    mod = _ada(c, ada_w, ada_b).reshape(depth, bsz, 1, 6 * d)
    kv_mod = _ada(c, kv_ada_w[None], kv_ada_b[None]).reshape(bsz, 1, 2 * d)

    inv_freq = 1.0 / (ROPE_THETA ** (jnp.arange(0, ROPE, 2, dtype=F32) / ROPE))
    ang = positions.astype(F32)[..., None] * inv_freq
    cos, sin = jnp.cos(ang), jnp.sin(ang)
    cs = jnp.concatenate([cos, cos, -sin, sin], axis=-1).reshape(t, LANES)

    lb_all = jnp.cumsum(jax.nn.softmax(hg_lower.astype(F32), axis=0), axis=0)[:n_a]

    hg_w_in, hg_w_out, mla_w_dq, mla_w_o, mlp_w1, mlp_w2 = (
        w.astype(BF16) for w in (hg_w_in, hg_w_out, mla_w_dq, mla_w_o, mlp_w1, mlp_w2))

    xf = x.reshape(t, d)
    k_sh = v_sh = None
    for layer in range(depth):
        if layer == n_a:
            wd = jnp.concatenate([mla_w_dkv, _swap_halves(mla_w_dkv[:, kv_lora:])], axis=1).astype(BF16)
            k_sh, v_sh = _mla_kv(xf, kv_in_norm, kv_mod, wd, mla_kv_norm, mla_w_ukv.astype(BF16),
                                 mla_k_norm[:LANES].reshape(1, LANES), _rope_gain(mla_k_norm[LANES:]),
                                 cs, bsz, seq)
        if layer < n_a:
            q, k, lf, v, sg = _hg_in(xf, norm_mix[layer], mod[layer], lb_all[layer],
                                     hg_w_in, layer, bsz, seq)
            y = _hg_rec(q, k, lf, v, sg, hg_o_norm[layer]).reshape(t, d)
            xf = _proj_res(y, hg_w_out, layer, xf, mod[layer], 2, seq)
        else:
            j = layer - n_a
            wu = mla_w_uq[j].reshape(q_lora, n_heads, QK)
            wu = jnp.concatenate([wu, _swap_halves(wu[..., LANES:])], axis=-1)
            wu = wu.reshape(q_lora, n_heads * QK_PAD).astype(BF16)
            sm_scale = float(QK) ** -0.5 * LOG2E
            qg = mla_q_norm[j] * sm_scale
            qh = _mla_q(xf, norm_mix[layer], mod[layer], mla_w_dq, j, mla_q_lat_norm[j], wu,
                        qg[:LANES].reshape(1, LANES), _rope_gain(qg[LANES:]), cs, bsz, seq)
            score_bound = (1.03 * QK * 2.0 ** 0.5) * jnp.max(jnp.abs(qg)) * jnp.max(jnp.abs(mla_k_norm))
            y = _flash(qh, k_sh, v_sh, score_bound.reshape(1, 1).astype(F32)).reshape(t, d)
            xf = _proj_res(y, mla_w_o, j, xf, mod[layer], 2, seq)
        xf = _mlp(xf, norm_mlp[layer], mod[layer], mlp_w1, mlp_w2, layer, seq)
    return xf.reshape(bsz, seq, d)
```

```python
import functools

import jax
import jax.numpy as jnp
from jax import lax
from jax.experimental import pallas as pl
from jax.experimental.pallas import tpu as pltpu

F32 = jnp.float32
BF16 = jnp.bfloat16

EPS = 1e-6
LANES = 128
ROPE = 64
QK = LANES + ROPE
QK_PAD = 2 * LANES
CHUNK = 64
ATTN_TILE = 512
MAX_FIXED_SOFTMAX_SHIFT = 40.0
MAX_FACTORED_DECAY = 72.0
ROPE_THETA = 10000.0
LOG2E = 1.4426950408889634
VMEM_LIMIT = 56 * 1024 * 1024


def _params(*sem):
    return pltpu.CompilerParams(dimension_semantics=sem, vmem_limit_bytes=VMEM_LIMIT)


def _tile(n, pref):
    t = min(n, pref)
    while n % t:
        t -= LANES
    assert t > 0 and n % t == 0, (n, pref)
    return t


def _norm_mod(x, gain, scale, shift):
    r = lax.rsqrt(jnp.mean(x * x, axis=-1, keepdims=True) + EPS)
    return (x * r) * (gain * (1.0 + scale)) + shift


def _ada_kernel(c_ref, w_ref, b_ref, o_ref):
    c = c_ref[...]
    cond = (c / (1.0 + jnp.exp(-c))).astype(BF16)
    o_ref[...] = jnp.dot(cond, w_ref[...].astype(BF16), preferred_element_type=F32) + b_ref[...]


def _ada(c, w, b):
    n_layers, d, n = w.shape
    bsz = c.shape[0]
    tn = _tile(n, 1024)
    return pl.pallas_call(
        _ada_kernel,
        grid=(n_layers, n // tn),
        in_specs=[
            pl.BlockSpec((bsz, d), lambda l, j: (0, 0)),
            pl.BlockSpec((None, d, tn), lambda l, j: (l, 0, j)),
            pl.BlockSpec((None, 1, tn), lambda l, j: (l, 0, j)),
        ],
        out_specs=pl.BlockSpec((None, bsz, tn), lambda l, j: (l, 0, j)),
        out_shape=jax.ShapeDtypeStruct((n_layers, bsz, n), F32),
        compiler_params=_params("parallel", "parallel"),
        name="ada",
    )(c, w, b.reshape(n_layers, 1, n))


def _mlp_kernel(x_ref, gain_ref, shift_ref, scale_ref, gate_ref, w1_ref, w2_ref, o_ref, h_ref, *, n_sub):
    j = pl.program_id(1)
    sub = w1_ref.shape[1] // n_sub

    def gated_piece(c):
        cols = slice(c * sub, (c + 1) * sub)
        a = jnp.maximum(jnp.dot(h_ref[...], w1_ref[:, cols], preferred_element_type=F32), 0.0)
        return gate_ref[...] * jnp.dot((a * a).astype(BF16), w2_ref[cols, :], preferred_element_type=F32)

    @pl.when(j == 0)
    def _():
        h_ref[...] = _norm_mod(x_ref[...], gain_ref[...], scale_ref[...], shift_ref[...]).astype(BF16)
        o_ref[...] = x_ref[...] + gated_piece(0)
        for c in range(1, n_sub):
            o_ref[...] += gated_piece(c)

    @pl.when(j > 0)
    def _():
        for c in range(n_sub):
            o_ref[...] += gated_piece(c)


def _mlp(x, gain, mod, w1, w2, layer, seq):
    t, d = x.shape
    f = w1.shape[2]
    tm = _tile(seq, 1024)
    tf = _tile(f, 1024)
    n_sub = 2 if tf % (2 * LANES) == 0 else 1
    per_b = seq // tm
    mod_spec = lambda k: pl.BlockSpec((None, 1, d), lambda i, j: (i // per_b, 0, k))
    return pl.pallas_call(
        functools.partial(_mlp_kernel, n_sub=n_sub),
        grid=(t // tm, f // tf),
        in_specs=[
            pl.BlockSpec((tm, d), lambda i, j: (i, 0)),
            pl.BlockSpec((1, d), lambda i, j: (0, 0)),
            mod_spec(3), mod_spec(4), mod_spec(5),
            pl.BlockSpec((None, d, tf), lambda i, j: (layer, 0, j)),
            pl.BlockSpec((None, tf, d), lambda i, j: (layer, j, 0)),
        ],
        out_specs=pl.BlockSpec((tm, d), lambda i, j: (i, 0)),
        out_shape=jax.ShapeDtypeStruct((t, d), F32),
        scratch_shapes=[pltpu.VMEM((tm, d), BF16)],
        compiler_params=_params("parallel", "arbitrary"),
        name="mlp",
    )(x, gain.reshape(1, d), mod, mod, mod, w1, w2)


def _proj_res_kernel(y_ref, w_ref, x_ref, gate_ref, o_ref, *, n_pieces):
    rows_per_piece = y_ref.shape[0] // n_pieces
    for p in range(n_pieces):
        rows = slice(p * rows_per_piece, (p + 1) * rows_per_piece)
        o_ref[rows, :] = x_ref[rows, :] + gate_ref[...] * jnp.dot(y_ref[rows, :], w_ref[...],
                                                                  preferred_element_type=F32)


def _proj_res(y, w, layer, x, mod, gate_idx, seq):
    t, d = x.shape
    k = y.shape[1]
    tm = _tile(seq, 1024)
    n_pieces = 2 if tm % 512 == 0 and tm > 512 else 1
    tn = _tile(d, 2048)
    per_b = seq // tm
    w_mode = dict(pipeline_mode=pl.Buffered(1)) if d == tn else {}
    return pl.pallas_call(
        functools.partial(_proj_res_kernel, n_pieces=n_pieces),
        grid=(t // tm, d // tn),
        in_specs=[
            pl.BlockSpec((tm, k), lambda i, j: (i, 0)),
            pl.BlockSpec((None, k, tn), lambda i, j: (layer, 0, j), **w_mode),
            pl.BlockSpec((tm, tn), lambda i, j: (i, j)),
            pl.BlockSpec((None, 1, tn), lambda i, j: (i // per_b, 0, gate_idx * (d // tn) + j)),
        ],
        out_specs=pl.BlockSpec((tm, tn), lambda i, j: (i, j)),
        out_shape=jax.ShapeDtypeStruct((t, d), F32),
        compiler_params=_params("parallel", "parallel"),
        name="proj_res",
    )(y, w, x, mod)


def _hg_in_kernel(x_ref, gain_ref, shift_ref, scale_ref, llb_ref, l1mlb_ref, omlb_ref,
                  wq_ref, wf_ref, wi_ref, wg_ref,
                  q_ref, k_ref, lf_ref, v_ref, sg_ref, h_ref, *, heads_per_step, heads_per_piece, q_scale):
    def project_piece(first_head):
        cols = slice(first_head * LANES, (first_head + heads_per_piece) * LANES)
        h = h_ref[...]
        z = jnp.dot(h, wf_ref[:, cols], preferred_element_type=F32)
        q = jnp.dot(h, wq_ref[:, cols], preferred_element_type=F32) * q_scale
        v = jnp.dot(h, wi_ref[:, cols], preferred_element_type=F32)
        g = jnp.dot(h, wg_ref[:, cols], preferred_element_type=F32)

        e = jnp.exp(-jnp.abs(z))
        log_sig = jnp.minimum(z, 0.0) - jnp.log(1.0 + e)
        a = llb_ref[:, cols]
        c = l1mlb_ref[:, cols] + log_sig
        lf = jnp.maximum(a, c) + jnp.log(1.0 + jnp.exp(-jnp.abs(a - c)))
        key = omlb_ref[:, cols] * (jnp.where(z >= 0.0, e, 1.0) / (1.0 + e))
        sg = g / (1.0 + jnp.exp(-g))

        for hh in range(heads_per_piece):
            cs = slice(hh * LANES, (hh + 1) * LANES)
            q_ref[first_head + hh] = q[:, cs].astype(BF16)
            k_ref[first_head + hh] = key[:, cs].astype(BF16)
            lf_ref[first_head + hh] = lf[:, cs]
            v_ref[first_head + hh] = v[:, cs].astype(BF16)
            sg_ref[first_head + hh] = sg[:, cs].astype(BF16)

    def project():
        for first_head in range(0, heads_per_step, heads_per_piece):
            project_piece(first_head)

    @pl.when(pl.program_id(1) == 0)
    def _():
        h_ref[...] = _norm_mod(x_ref[...], gain_ref[...], scale_ref[...], shift_ref[...]).astype(BF16)
        project()

    @pl.when(pl.program_id(1) > 0)
    def _():
        project()


def _hg_in(x, gain, mod, lb, w_in, layer, bsz, seq):
    t, d = x.shape
    n_heads = d // LANES
    tm = _tile(seq, 1024)
    tn = _tile(d, 4 * LANES)
    hps = tn // LANES
    hpp = 2 if hps % 2 == 0 else 1
    nj = d // tn
    per_b = seq // tm
    mod_spec = lambda k: pl.BlockSpec((None, 1, d), lambda i, j: (i // per_b, 0, k))
    vec_spec = pl.BlockSpec((1, tn), lambda i, j: (0, j))
    w_spec = lambda grp: pl.BlockSpec((None, d, tn), lambda i, j: (layer, 0, grp * nj + j))
    out_spec = pl.BlockSpec((None, hps, tm, LANES), lambda i, j: (i // per_b, j, i % per_b, 0))
    hm = lambda dt: jax.ShapeDtypeStruct((bsz, n_heads, seq, LANES), dt)
    lbf = lb.reshape(1, d)
    kern = functools.partial(_hg_in_kernel, heads_per_step=hps, heads_per_piece=hpp, q_scale=float(LANES) ** -0.5)
    return pl.pallas_call(
        kern,
        grid=(t // tm, nj),
        in_specs=[
            pl.BlockSpec((tm, d), lambda i, j: (i, 0)),
            pl.BlockSpec((1, d), lambda i, j: (0, 0)),
            mod_spec(0), mod_spec(1),
            vec_spec, vec_spec, vec_spec,
            w_spec(0), w_spec(1), w_spec(2), w_spec(3),
        ],
        out_specs=[out_spec] * 5,
        out_shape=[hm(BF16), hm(BF16), hm(F32), hm(BF16), hm(BF16)],
        scratch_shapes=[pltpu.VMEM((tm, d), BF16)],
        compiler_params=_params("parallel", "arbitrary"),
        name="hg_in",
    )(x, gain.reshape(1, d), mod, mod, jnp.log(lbf), jnp.log1p(-lbf), 1.0 - lbf, w_in, w_in, w_in, w_in)


def _nt_dot(a, b):
    return lax.dot_general(a, b, (((1,), (1,)), ((), ())), preferred_element_type=F32)


def _tn_dot(a, b):
    return lax.dot_general(a, b, (((0,), (0,)), ((), ())), preferred_element_type=F32)


def _chunk_scores(q, k, b, lf):
    row = lax.broadcasted_iota(jnp.int32, (CHUNK, LANES), 0)
    ti = lax.broadcasted_iota(jnp.int32, (CHUNK, CHUNK), 0)
    si = lax.broadcasted_iota(jnp.int32, (CHUNK, CHUNK), 1)
    scores = jnp.where(ti == si, _nt_dot(q.astype(BF16), k.astype(BF16)), 0.0)
    start = b - lf
    c = 1
    while c < CHUNK:
        later = (row & c) != 0
        bound = jnp.where(later, start, pltpu.roll(start, CHUNK - c, axis=0))
        e = jnp.exp(jnp.where(later, b - bound, bound - b))
        m = jnp.where(later, q, k) * e
        qs = jnp.where(later, m, 0.0).astype(BF16)
        ks = jnp.where(later, 0.0, m).astype(BF16)
        same_parent = (ti & -(2 * c)) == (si & -(2 * c))
        scores = scores + jnp.where(same_parent, _nt_dot(qs, ks), 0.0)
        if 2 * c < CHUNK:
            start = jnp.where(later, pltpu.roll(start, c, axis=0), start)
        c *= 2
    return scores


def _cumsum_rows(lf, tri):
    hi = lf.astype(BF16)
    r1 = lf - hi.astype(F32)
    mid = r1.astype(BF16)
    lo = (r1 - mid.astype(F32)).astype(BF16)
    dot = lambda part: jnp.dot(tri, part, preferred_element_type=F32)
    return dot(hi) + dot(mid) + dot(lo)


def _rec_kernel(q_ref, k_ref, lf_ref, v_ref, sg_ref, og_ref, o_ref, st_ref, *, n_chunks):
    @pl.when(pl.program_id(2) == 0)
    def _():
        st_ref[...] = jnp.zeros_like(st_ref)

    ti = lax.broadcasted_iota(jnp.int32, (CHUNK, CHUNK), 0)
    si = lax.broadcasted_iota(jnp.int32, (CHUNK, CHUNK), 1)
    causal = si <= ti
    tri = jnp.where(causal, 1.0, 0.0).astype(BF16)
    og = og_ref[...]

    def rows(ci):
        return slice(ci * CHUNK, (ci + 1) * CHUNK)

    mean_w = jnp.full((LANES, LANES), 1.0 / LANES, BF16)

    def finish(outs):
        ms = [jnp.dot((o * o).astype(BF16), mean_w, preferred_element_type=F32) for o in outs]
        for ci, o in enumerate(outs):
            o = o * lax.rsqrt(ms[ci] + EPS) * og
            o_ref[rows(ci), :] = (o * sg_ref[rows(ci), :].astype(F32)).astype(BF16)

    cs = range(n_chunks)
    b = [_cumsum_rows(lf_ref[rows(ci), :], tri) for ci in cs]
    total = b[0][CHUNK - 1:CHUNK, :]
    for ci in range(1, n_chunks):
        total = jnp.minimum(total, b[ci][CHUNK - 1:CHUNK, :])
    mild = jnp.min(total) >= -MAX_FACTORED_DECAY
    q_dec = [(q_ref[rows(ci), :].astype(F32) * jnp.exp(b[ci])).astype(BF16) for ci in cs]
    decay = [jnp.exp(b[ci][CHUNK - 1:CHUNK, :]) for ci in cs]
    k_grow = [k_ref[rows(ci), :].astype(F32) * jnp.exp(-b[ci]) for ci in cs]
    scores = [jnp.where(causal, jnp.dot(q_dec[ci], k_grow[ci].T.astype(BF16), preferred_element_type=F32),
                        0.0).astype(BF16) for ci in cs]
    intra = [jnp.dot(scores[ci], v_ref[rows(ci), :], preferred_element_type=F32) for ci in cs]

    @pl.when(mild)
    def _():
        update = [_tn_dot(v_ref[rows(ci), :], (k_grow[ci] * decay[ci]).astype(BF16)) for ci in cs]
        states = [st_ref[...]]
        for ci in cs:
            states.append(states[ci] * decay[ci] + update[ci])
        finish([intra[ci] + jnp.dot(q_dec[ci], states[ci].T.astype(BF16), preferred_element_type=F32)
                for ci in cs])
        st_ref[...] = states[n_chunks]

    @pl.when(jnp.logical_not(mild))
    def _():
        state_t = st_ref[...]
        outs = []
        for ci in range(n_chunks):
            lf = lf_ref[rows(ci), :]
            q = q_ref[rows(ci), :].astype(F32)
            k = k_ref[rows(ci), :].astype(F32)
            v = v_ref[rows(ci), :]
            scores = _chunk_scores(q, k, b[ci], lf)
            o = jnp.dot(scores.astype(BF16), v, preferred_element_type=F32)
            o = o + _nt_dot(q_dec[ci], state_t.astype(BF16))
            k_dec = (k * jnp.exp(b[ci][CHUNK - 1:CHUNK, :] - b[ci])).astype(BF16)
            state_t = state_t * decay[ci] + _tn_dot(v, k_dec)
            outs.append(o)
        finish(outs)
        st_ref[...] = state_t


def _hg_rec(q, k, lf, v, sg, o_gain):
    bsz, n_heads, seq, _ = q.shape
    ts = _tile(seq, 2048)
    in_spec = pl.BlockSpec((None, None, ts, LANES), lambda b, h, s: (b, h, s, 0))
    return pl.pallas_call(
        functools.partial(_rec_kernel, n_chunks=ts // CHUNK),
        grid=(bsz, n_heads, seq // ts),
        in_specs=[in_spec] * 5 + [pl.BlockSpec((1, LANES), lambda b, h, s: (0, 0))],
        out_specs=pl.BlockSpec((None, ts, LANES), lambda b, h, s: (b, s, h)),
        out_shape=jax.ShapeDtypeStruct((bsz, seq, n_heads * LANES), BF16),
        scratch_shapes=[pltpu.VMEM((LANES, LANES), F32)],
        compiler_params=_params("parallel", "parallel", "arbitrary"),
        name="hg_rec",
    )(q, k, lf, v, sg, o_gain.reshape(1, LANES))


def _pair_sum_weights():
    row = lax.broadcasted_iota(jnp.int32, (QK_PAD, QK_PAD), 0)
    col = lax.broadcasted_iota(jnp.int32, (QK_PAD, QK_PAD), 1)
    return jnp.where((row < LANES) == (col < LANES), 1.0, 0.0).astype(BF16)


def _rope_pair(u):
    return u + pltpu.roll(u, ROPE, axis=1)


def _mla_kv_kernel(x_ref, gain_ref, shift_ref, scale_ref, wd_ref, lat_gain_ref, wu_ref,
                   gn_ref, gr_ref, cs_ref, k_ref, v_ref, *, n_heads, kv_lora):
    h = _norm_mod(x_ref[...], gain_ref[...], scale_ref[...], shift_ref[...]).astype(BF16)
    ckv = jnp.dot(h, wd_ref[...], preferred_element_type=F32)
    lat = ckv[:, :kv_lora]
    pe2 = ckv[:, kv_lora:]
    c_lat = (lat * lax.rsqrt(jnp.mean(lat * lat, axis=-1, keepdims=True) + EPS) * lat_gain_ref[...]).astype(BF16)
    pair_w = _pair_sum_weights()
    sq_pe = 0.5 * (pe2 * pe2)
    rot = _rope_pair(pe2 * gr_ref[...] * cs_ref[...])
    gn = gn_ref[...]

    def pair_kv(p):
        return [jnp.dot(c_lat, wu_ref[:, hh * QK_PAD:(hh + 1) * QK_PAD], preferred_element_type=F32)
                for hh in (2 * p, 2 * p + 1)]

    nxt = pair_kv(0)
    for p in range(n_heads // 2):
        kvs = nxt
        if p + 1 < n_heads // 2:
            nxt = pair_kv(p + 1)
        sq = jnp.concatenate([kv[:, :LANES] * kv[:, :LANES] + sq_pe for kv in kvs], axis=1)
        r2 = lax.rsqrt(jnp.dot(sq.astype(BF16), pair_w, preferred_element_type=F32) * (1.0 / QK) + EPS)
        for idx, kv in enumerate(kvs):
            hh = 2 * p + idx
            r = r2[:, idx * LANES:(idx + 1) * LANES]
            k_ref[hh, :, :LANES] = (kv[:, :LANES] * r * gn).astype(BF16)
            k_ref[hh, :, LANES:] = (rot * r).astype(BF16)
            v_ref[hh] = kv[:, LANES:].T.astype(BF16)


def _mla_kv(x, gain, mod, wd, lat_gain, wu, gn, gr, cs, bsz, seq):
    t, d = x.shape
    n_heads = d // LANES
    kv_lora = lat_gain.shape[0]
    tm = _tile(seq, ATTN_TILE)
    per_b = seq // tm
    mod_spec = lambda k: pl.BlockSpec((None, 1, d), lambda i: (i // per_b, 0, k))
    assert n_heads % 2 == 0, n_heads
    full = lambda a: pl.BlockSpec(a.shape, lambda i: (0,) * a.ndim)
    lat_gain = lat_gain.reshape(1, kv_lora)
    return pl.pallas_call(
        functools.partial(_mla_kv_kernel, n_heads=n_heads, kv_lora=kv_lora),
        grid=(t // tm,),
        in_specs=[
            pl.BlockSpec((tm, d), lambda i: (i, 0)),
            pl.BlockSpec((1, d), lambda i: (0, 0)),
            mod_spec(0), mod_spec(1),
            full(wd), full(lat_gain), full(wu), full(gn), full(gr),
            pl.BlockSpec((tm, LANES), lambda i: (i, 0)),
        ],
        out_specs=[
            pl.BlockSpec((None, n_heads, tm, QK_PAD), lambda i: (i // per_b, 0, i % per_b, 0)),
            pl.BlockSpec((None, n_heads, None, LANES, tm), lambda i: (i // per_b, 0, i % per_b, 0, 0)),
        ],
        out_shape=[jax.ShapeDtypeStruct((bsz, n_heads, seq, QK_PAD), BF16),
                   jax.ShapeDtypeStruct((bsz, n_heads, seq // tm, LANES, tm), BF16)],
        compiler_params=_params("parallel"),
        name="mla_kv",
    )(x, gain.reshape(1, d), mod, mod, wd, lat_gain, wu, gn, gr, cs)


def _mla_q_kernel(x_ref, gain_ref, shift_ref, scale_ref, wd_ref, lat_gain_ref, wu_ref,
                  gn_ref, gr_ref, cs_ref, q_ref, *, n_heads):
    h = _norm_mod(x_ref[...], gain_ref[...], scale_ref[...], shift_ref[...]).astype(BF16)
    cq = jnp.dot(h, wd_ref[...], preferred_element_type=F32)
    cq = (cq * lax.rsqrt(jnp.mean(cq * cq, axis=-1, keepdims=True) + EPS) * lat_gain_ref[...]).astype(BF16)
    gn = gn_ref[...]
    rope_tab = gr_ref[...] * cs_ref[...]
    pair_w = _pair_sum_weights()

    def pair_q(p):
        return [jnp.dot(cq, wu_ref[:, hh * QK_PAD:(hh + 1) * QK_PAD], preferred_element_type=F32)
                for hh in (2 * p, 2 * p + 1)]

    nxt = pair_q(0)
    for p in range(n_heads // 2):
        qhs = nxt
        if p + 1 < n_heads // 2:
            nxt = pair_q(p + 1)
        sq = jnp.concatenate([qh[:, :LANES] * qh[:, :LANES] + 0.5 * (qh[:, LANES:] * qh[:, LANES:])
                              for qh in qhs], axis=1)
        r2 = lax.rsqrt(jnp.dot(sq.astype(BF16), pair_w, preferred_element_type=F32) * (1.0 / QK) + EPS)
        for idx, qh in enumerate(qhs):
            hh = 2 * p + idx
            r = r2[:, idx * LANES:(idx + 1) * LANES]
            q_ref[hh, :LANES, :] = (qh[:, :LANES] * r * gn).T.astype(BF16)
            q_ref[hh, LANES:, :] = (qh[:, LANES:] * rope_tab * r).T.astype(BF16)


def _mla_q(x, gain, mod, wd, layer, lat_gain, wu, gn, gr, cs, bsz, seq):
    t, d = x.shape
    n_heads = d // LANES
    q_lora = lat_gain.shape[0]
    tm = _tile(seq, ATTN_TILE)
    per_b = seq // tm
    mod_spec = lambda k: pl.BlockSpec((None, 1, d), lambda i: (i // per_b, 0, k))
    assert n_heads % 2 == 0, n_heads
    full = lambda a: pl.BlockSpec(a.shape, lambda i: (0,) * a.ndim)
    lat_gain = lat_gain.reshape(1, q_lora)
    return pl.pallas_call(
        functools.partial(_mla_q_kernel, n_heads=n_heads),
        grid=(t // tm,),
        in_specs=[
            pl.BlockSpec((tm, d), lambda i: (i, 0)),
            pl.BlockSpec((1, d), lambda i: (0, 0)),
            mod_spec(0), mod_spec(1),
            pl.BlockSpec((None, d, q_lora), lambda i: (layer, 0, 0)),
            full(lat_gain), full(wu), full(gn), full(gr),
            pl.BlockSpec((tm, LANES), lambda i: (i, 0)),
        ],
        out_specs=pl.BlockSpec((None, n_heads, None, QK_PAD, tm), lambda i: (i // per_b, 0, i % per_b, 0, 0)),
        out_shape=jax.ShapeDtypeStruct((bsz, n_heads, seq // tm, QK_PAD, tm), BF16),
        compiler_params=_params("parallel"),
        name="mla_q",
    )(x, gain.reshape(1, d), mod, mod, wd, lat_gain, wu, gn, gr, cs)


def _flash_kernel(bound_ref, qt_ref, k_ref, vt_ref, o_ref, s_ref, m_ref, l_ref, acc_ref, *,
                  tq, n_tiles, heads_per_step):
    def tile_rows(j):
        return pl.ds(pl.multiple_of(j * tq, tq), tq)

    def scores(hh, i, j):
        return jnp.dot(k_ref[hh, tile_rows(j), :], qt_ref[hh, i], preferred_element_type=F32)

    bound = bound_ref[0, 0]

    def consume(hh, j, masked, fixed_shift):
        s = s_ref[hh]
        if masked:
            ki = lax.broadcasted_iota(jnp.int32, (tq, tq), 0)
            qi = lax.broadcasted_iota(jnp.int32, (tq, tq), 1)
            s = jnp.where(ki <= qi, s, -jnp.inf)
        if fixed_shift:
            p = jnp.exp2(s)
            l_ref[hh] += jnp.sum(p, axis=0, keepdims=True)
            acc_ref[hh] += jnp.dot(vt_ref[hh, j], p.astype(BF16), preferred_element_type=F32)
            return
        m = m_ref[hh]
        m_new = jnp.maximum(m, jnp.max(s, axis=0, keepdims=True))
        alpha = jnp.exp2(m - m_new)
        p = jnp.exp2(s - m_new)
        m_ref[hh] = m_new
        l_ref[hh] = alpha * l_ref[hh] + jnp.sum(p, axis=0, keepdims=True)
        pv = jnp.dot(vt_ref[hh, j], p.astype(BF16), preferred_element_type=F32)
        acc_ref[hh] = alpha * acc_ref[hh] + pv

    def attend(fixed_shift):
        for hh in range(heads_per_step):
            s_ref[hh] = scores(hh, 0, 0)

        def query_tile(i, carry):
            m_ref[...] = jnp.full(m_ref.shape, -jnp.inf, F32)
            l_ref[...] = jnp.zeros_like(l_ref)
            acc_ref[...] = jnp.zeros_like(acc_ref)

            def key_tile(j, c):
                for hh in range(heads_per_step):
                    s_next = scores(hh, i, j + 1)
                    consume(hh, j, False, fixed_shift)
                    s_ref[hh] = s_next
                return c

            lax.fori_loop(0, i, key_tile, 0)
            i_next = jnp.minimum(i + 1, n_tiles - 1)
            for hh in range(heads_per_step):
                s_next = scores(hh, i_next, 0)
                consume(hh, i, True, fixed_shift)
                s_ref[hh] = s_next
                o_ref[tile_rows(i), hh * LANES:(hh + 1) * LANES] = (acc_ref[hh] / l_ref[hh]).T.astype(BF16)
            return carry

        lax.fori_loop(0, n_tiles, query_tile, 0)

    @pl.when(bound <= MAX_FIXED_SOFTMAX_SHIFT)
    def _():
        attend(True)

    @pl.when(jnp.logical_not(bound <= MAX_FIXED_SOFTMAX_SHIFT))
    def _():
        attend(False)


def _flash(qt, k, vt, score_bound):
    bsz, n_heads, seq, _ = k.shape
    tq = vt.shape[-1]
    hps = 4 if n_heads % 4 == 0 else (2 if n_heads % 2 == 0 else 1)
    return pl.pallas_call(
        functools.partial(_flash_kernel, tq=tq, n_tiles=seq // tq, heads_per_step=hps),
        grid=(bsz, n_heads // hps),
        in_specs=[
            pl.BlockSpec(memory_space=pltpu.SMEM),
            pl.BlockSpec((None, hps, seq // tq, QK_PAD, tq), lambda b, h: (b, h, 0, 0, 0)),
            pl.BlockSpec((None, hps, seq, QK_PAD), lambda b, h: (b, h, 0, 0)),
            pl.BlockSpec((None, hps, seq // tq, LANES, tq), lambda b, h: (b, h, 0, 0, 0)),
        ],
        out_specs=pl.BlockSpec((None, seq, hps * LANES), lambda b, h: (b, 0, h)),
        out_shape=jax.ShapeDtypeStruct((bsz, seq, n_heads * LANES), BF16),
        scratch_shapes=[pltpu.VMEM((hps, tq, tq), F32),
                        pltpu.VMEM((hps, 1, tq), F32), pltpu.VMEM((hps, 1, tq), F32),
                        pltpu.VMEM((hps, LANES, tq), F32)],
        compiler_params=_params("parallel", "parallel"),
        name="flash",
    )(score_bound, qt, k, vt)


def _swap_halves(a):
    half = a.shape[-1] // 2
    return jnp.concatenate([a[..., half:], a[..., :half]], axis=-1)


def _rope_gain(g):
    return jnp.concatenate([g, _swap_halves(g)]).reshape(1, LANES)


def kernel(x, c, positions, ada_w, ada_b, norm_mix, norm_mlp, mlp_w1, mlp_w2, hg_w_in, hg_lower, hg_o_norm, hg_w_out, kv_ada_w, kv_ada_b, kv_in_norm, mla_w_dkv, mla_kv_norm, mla_w_ukv, mla_k_norm, mla_w_dq, mla_q_lat_norm, mla_w_uq, mla_q_norm, mla_w_o):
    bsz, seq, d = x.shape
    depth = ada_w.shape[0]
    n_a = hg_w_in.shape[0]
    n_heads = d // LANES
    kv_lora = mla_kv_norm.shape[0]
    q_lora = mla_w_dq.shape[2]
    t = bsz * seq

    mod = _ada(c, ada_w, ada_b).reshape(depth, bsz, 1, 6 * d)
    kv_mod = _ada(c, kv_ada_w[None], kv_ada_b[None]).reshape(bsz, 1, 2 * d)

    inv_freq = 1.0 / (ROPE_THETA ** (jnp.arange(0, ROPE, 2, dtype=F32) / ROPE))
    ang = positions.astype(F32)[..., None] * inv_freq
    cos, sin = jnp.cos(ang), jnp.sin(ang)
    cs = jnp.concatenate([cos, cos, -sin, sin], axis=-1).reshape(t, LANES)

    lb_all = jnp.cumsum(jax.nn.softmax(hg_lower.astype(F32), axis=0), axis=0)[:n_a]

    hg_w_in, hg_w_out, mla_w_dq, mla_w_o, mlp_w1, mlp_w2 = (
        w.astype(BF16) for w in (hg_w_in, hg_w_out, mla_w_dq, mla_w_o, mlp_w1, mlp_w2))

    xf = x.reshape(t, d)
    k_sh = v_sh = None
    for layer in range(depth):
        if layer == n_a:
            wd = jnp.concatenate([mla_w_dkv, _swap_halves(mla_w_dkv[:, kv_lora:])], axis=1).astype(BF16)
            k_sh, v_sh = _mla_kv(xf, kv_in_norm, kv_mod, wd, mla_kv_norm, mla_w_ukv.astype(BF16),
                                 mla_k_norm[:LANES].reshape(1, LANES), _rope_gain(mla_k_norm[LANES:]),
                                 cs, bsz, seq)
        if layer < n_a:
            q, k, lf, v, sg = _hg_in(xf, norm_mix[layer], mod[layer], lb_all[layer],
                                     hg_w_in, layer, bsz, seq)
            y = _hg_rec(q, k, lf, v, sg, hg_o_norm[layer]).reshape(t, d)
            xf = _proj_res(y, hg_w_out, layer, xf, mod[layer], 2, seq)
        else:
            j = layer - n_a
            wu = mla_w_uq[j].reshape(q_lora, n_heads, QK)
            wu = jnp.concatenate([wu, _swap_halves(wu[..., LANES:])], axis=-1)
            wu = wu.reshape(q_lora, n_heads * QK_PAD).astype(BF16)
            sm_scale = float(QK) ** -0.5 * LOG2E
            qg = mla_q_norm[j] * sm_scale
            qh = _mla_q(xf, norm_mix[layer], mod[layer], mla_w_dq, j, mla_q_lat_norm[j], wu,
                        qg[:LANES].reshape(1, LANES), _rope_gain(qg[LANES:]), cs, bsz, seq)
            score_bound = (1.03 * QK * 2.0 ** 0.5) * jnp.max(jnp.abs(qg)) * jnp.max(jnp.abs(mla_k_norm))
            y = _flash(qh, k_sh, v_sh, score_bound.reshape(1, 1).astype(F32)).reshape(t, d)
            xf = _proj_res(y, mla_w_o, j, xf, mod[layer], 2, seq)
        xf = _mlp(xf, norm_mlp[layer], mod[layer], mlp_w1, mlp_w2, layer, seq)
    return xf.reshape(bsz, seq, d)
```

```python
import functools

import jax
import jax.numpy as jnp
from jax import lax
from jax.experimental import pallas as pl
from jax.experimental.pallas import tpu as pltpu

F32 = jnp.float32
BF16 = jnp.bfloat16

EPS = 1e-6
LANES = 128
ROPE = 64
QK = LANES + ROPE
QK_PAD = 2 * LANES
CHUNK = 64
ATTN_TILE = 512
MAX_FIXED_SOFTMAX_SHIFT = 40.0
MAX_FACTORED_DECAY = 72.0
ROPE_THETA = 10000.0
LOG2E = 1.4426950408889634
VMEM_LIMIT = 56 * 1024 * 1024


def _params(*sem):
    return pltpu.CompilerParams(dimension_semantics=sem, vmem_limit_bytes=VMEM_LIMIT)


def _tile(n, pref):
    t = min(n, pref)
    while n % t:
        t -= LANES
    assert t > 0 and n % t == 0, (n, pref)
    return t


def _norm_mod(x, gain, scale, shift):
    r = lax.rsqrt(jnp.mean(x * x, axis=-1, keepdims=True) + EPS)
    return (x * r) * (gain * (1.0 + scale)) + shift


def _ada_kernel(c_ref, w_ref, b_ref, o_ref):
    c = c_ref[...]
    cond = (c / (1.0 + jnp.exp(-c))).astype(BF16)
    o_ref[...] = jnp.dot(cond, w_ref[...].astype(BF16), preferred_element_type=F32) + b_ref[...]


def _ada(c, w, b):
    n_layers, d, n = w.shape
    bsz = c.shape[0]
    tn = _tile(n, 1024)
    return pl.pallas_call(
        _ada_kernel,
        grid=(n_layers, n // tn),
        in_specs=[
            pl.BlockSpec((bsz, d), lambda l, j: (0, 0)),
            pl.BlockSpec((None, d, tn), lambda l, j: (l, 0, j)),
            pl.BlockSpec((None, 1, tn), lambda l, j: (l, 0, j)),
        ],
        out_specs=pl.BlockSpec((None, bsz, tn), lambda l, j: (l, 0, j)),
        out_shape=jax.ShapeDtypeStruct((n_layers, bsz, n), F32),
        compiler_params=_params("parallel", "parallel"),
        name="ada",
    )(c, w, b.reshape(n_layers, 1, n))


def _mlp_kernel(x_ref, gain_ref, shift_ref, scale_ref, gate_ref, w1_ref, w2_ref, o_ref, h_ref, *, n_sub):
    j = pl.program_id(1)
    sub = w1_ref.shape[1] // n_sub

    def gated_piece(c):
        cols = slice(c * sub, (c + 1) * sub)
        a = jnp.maximum(jnp.dot(h_ref[...], w1_ref[:, cols], preferred_element_type=F32), 0.0)
        return gate_ref[...] * jnp.dot((a * a).astype(BF16), w2_ref[cols, :], preferred_element_type=F32)

    @pl.when(j == 0)
    def _():
        h_ref[...] = _norm_mod(x_ref[...], gain_ref[...], scale_ref[...], shift_ref[...]).astype(BF16)
        o_ref[...] = x_ref[...] + gated_piece(0)
        for c in range(1, n_sub):
            o_ref[...] += gated_piece(c)

    @pl.when(j > 0)
    def _():
        for c in range(n_sub):
            o_ref[...] += gated_piece(c)


def _mlp(x, gain, mod, w1, w2, layer, seq):
    t, d = x.shape
    f = w1.shape[2]
    tm = _tile(seq, 1024)
    tf = _tile(f, 1024)
    n_sub = 2 if tf % (2 * LANES) == 0 else 1
    per_b = seq // tm
    mod_spec = lambda k: pl.BlockSpec((None, 1, d), lambda i, j: (i // per_b, 0, k))
    return pl.pallas_call(
        functools.partial(_mlp_kernel, n_sub=n_sub),
        grid=(t // tm, f // tf),
        in_specs=[
            pl.BlockSpec((tm, d), lambda i, j: (i, 0)),
            pl.BlockSpec((1, d), lambda i, j: (0, 0)),
            mod_spec(3), mod_spec(4), mod_spec(5),
            pl.BlockSpec((None, d, tf), lambda i, j: (layer, 0, j)),
            pl.BlockSpec((None, tf, d), lambda i, j: (layer, j, 0)),
        ],
        out_specs=pl.BlockSpec((tm, d), lambda i, j: (i, 0)),
        out_shape=jax.ShapeDtypeStruct((t, d), F32),
        scratch_shapes=[pltpu.VMEM((tm, d), BF16)],
        compiler_params=_params("parallel", "arbitrary"),
        name="mlp",
    )(x, gain.reshape(1, d), mod, mod, mod, w1, w2)


def _proj_res_kernel(y_ref, w_ref, x_ref, gate_ref, o_ref, *, n_pieces):
    rows_per_piece = y_ref.shape[0] // n_pieces
    for p in range(n_pieces):
        rows = slice(p * rows_per_piece, (p + 1) * rows_per_piece)
        o_ref[rows, :] = x_ref[rows, :] + gate_ref[...] * jnp.dot(y_ref[rows, :], w_ref[...],
                                                                  preferred_element_type=F32)


def _proj_res(y, w, layer, x, mod, gate_idx, seq):
    t, d = x.shape
    k = y.shape[1]
    tm = _tile(seq, 1024)
    n_pieces = 2 if tm % 512 == 0 and tm > 512 else 1
    tn = _tile(d, 2048)
    per_b = seq // tm
    w_mode = dict(pipeline_mode=pl.Buffered(1)) if d == tn else {}
    return pl.pallas_call(
        functools.partial(_proj_res_kernel, n_pieces=n_pieces),
        grid=(t // tm, d // tn),
        in_specs=[
            pl.BlockSpec((tm, k), lambda i, j: (i, 0)),
            pl.BlockSpec((None, k, tn), lambda i, j: (layer, 0, j), **w_mode),
            pl.BlockSpec((tm, tn), lambda i, j: (i, j)),
            pl.BlockSpec((None, 1, tn), lambda i, j: (i // per_b, 0, gate_idx * (d // tn) + j)),
        ],
        out_specs=pl.BlockSpec((tm, tn), lambda i, j: (i, j)),
        out_shape=jax.ShapeDtypeStruct((t, d), F32),
        compiler_params=_params("parallel", "parallel"),
        name="proj_res",
    )(y, w, x, mod)


def _hg_in_kernel(x_ref, gain_ref, shift_ref, scale_ref, llb_ref, l1mlb_ref, omlb_ref,
                  wq_ref, wf_ref, wi_ref, wg_ref,
                  q_ref, k_ref, lf_ref, v_ref, sg_ref, h_ref, *, heads_per_step, heads_per_piece, q_scale):
    def project_piece(first_head):
        cols = slice(first_head * LANES, (first_head + heads_per_piece) * LANES)
        h = h_ref[...]
        z = jnp.dot(h, wf_ref[:, cols], preferred_element_type=F32)
        q = jnp.dot(h, wq_ref[:, cols], preferred_element_type=F32) * q_scale
        v = jnp.dot(h, wi_ref[:, cols], preferred_element_type=F32)
        g = jnp.dot(h, wg_ref[:, cols], preferred_element_type=F32)

        e = jnp.exp(-jnp.abs(z))
        log_sig = jnp.minimum(z, 0.0) - jnp.log(1.0 + e)
        a = llb_ref[:, cols]
        c = l1mlb_ref[:, cols] + log_sig
        lf = jnp.maximum(a, c) + jnp.log(1.0 + jnp.exp(-jnp.abs(a - c)))
        key = omlb_ref[:, cols] * (jnp.where(z >= 0.0, e, 1.0) / (1.0 + e))
        sg = g / (1.0 + jnp.exp(-g))

        for hh in range(heads_per_piece):
            cs = slice(hh * LANES, (hh + 1) * LANES)
            q_ref[first_head + hh] = q[:, cs].astype(BF16)
            k_ref[first_head + hh] = key[:, cs].astype(BF16)
            lf_ref[first_head + hh] = lf[:, cs]
            v_ref[first_head + hh] = v[:, cs].astype(BF16)
            sg_ref[first_head + hh] = sg[:, cs].astype(BF16)

    def project():
        for first_head in range(0, heads_per_step, heads_per_piece):
            project_piece(first_head)

    @pl.when(pl.program_id(1) == 0)
    def _():
        h_ref[...] = _norm_mod(x_ref[...], gain_ref[...], scale_ref[...], shift_ref[...]).astype(BF16)
        project()

    @pl.when(pl.program_id(1) > 0)
    def _():
        project()


def _hg_in(x, gain, mod, lb, w_in, layer, bsz, seq):
    t, d = x.shape
    n_heads = d // LANES
    tm = _tile(seq, 1024)
    tn = _tile(d, 4 * LANES)
    hps = tn // LANES
    hpp = 2 if hps % 2 == 0 else 1
    nj = d // tn
    per_b = seq // tm
    mod_spec = lambda k: pl.BlockSpec((None, 1, d), lambda i, j: (i // per_b, 0, k))
    vec_spec = pl.BlockSpec((1, tn), lambda i, j: (0, j))
    w_spec = lambda grp: pl.BlockSpec((None, d, tn), lambda i, j: (layer, 0, grp * nj + j))
    out_spec = pl.BlockSpec((None, hps, tm, LANES), lambda i, j: (i // per_b, j, i % per_b, 0))
    hm = lambda dt: jax.ShapeDtypeStruct((bsz, n_heads, seq, LANES), dt)
    lbf = lb.reshape(1, d)
    kern = functools.partial(_hg_in_kernel, heads_per_step=hps, heads_per_piece=hpp, q_scale=float(LANES) ** -0.5)
    return pl.pallas_call(
        kern,
        grid=(t // tm, nj),
        in_specs=[
            pl.BlockSpec((tm, d), lambda i, j: (i, 0)),
            pl.BlockSpec((1, d), lambda i, j: (0, 0)),
            mod_spec(0), mod_spec(1),
            vec_spec, vec_spec, vec_spec,
            w_spec(0), w_spec(1), w_spec(2), w_spec(3),
        ],
        out_specs=[out_spec] * 5,
        out_shape=[hm(BF16), hm(BF16), hm(F32), hm(BF16), hm(BF16)],
        scratch_shapes=[pltpu.VMEM((tm, d), BF16)],
        compiler_params=_params("parallel", "arbitrary"),
        name="hg_in",
    )(x, gain.reshape(1, d), mod, mod, jnp.log(lbf), jnp.log1p(-lbf), 1.0 - lbf, w_in, w_in, w_in, w_in)


def _nt_dot(a, b):
    return lax.dot_general(a, b, (((1,), (1,)), ((), ())), preferred_element_type=F32)


def _tn_dot(a, b):
    return lax.dot_general(a, b, (((0,), (0,)), ((), ())), preferred_element_type=F32)


def _chunk_scores(q, k, b, lf):
    row = lax.broadcasted_iota(jnp.int32, (CHUNK, LANES), 0)
    ti = lax.broadcasted_iota(jnp.int32, (CHUNK, CHUNK), 0)
    si = lax.broadcasted_iota(jnp.int32, (CHUNK, CHUNK), 1)
    scores = jnp.where(ti == si, _nt_dot(q.astype(BF16), k.astype(BF16)), 0.0)
    start = b - lf
    c = 1
    while c < CHUNK:
        later = (row & c) != 0
        bound = jnp.where(later, start, pltpu.roll(start, CHUNK - c, axis=0))
        e = jnp.exp(jnp.where(later, b - bound, bound - b))
        m = jnp.where(later, q, k) * e
        qs = jnp.where(later, m, 0.0).astype(BF16)
        ks = jnp.where(later, 0.0, m).astype(BF16)
        same_parent = (ti & -(2 * c)) == (si & -(2 * c))
        scores = scores + jnp.where(same_parent, _nt_dot(qs, ks), 0.0)
        if 2 * c < CHUNK:
            start = jnp.where(later, pltpu.roll(start, c, axis=0), start)
        c *= 2
    return scores


def _cumsum_rows(lf, tri):
    hi = lf.astype(BF16)
    r1 = lf - hi.astype(F32)
    mid = r1.astype(BF16)
    lo = (r1 - mid.astype(F32)).astype(BF16)
    dot = lambda part: jnp.dot(tri, part, preferred_element_type=F32)
    return dot(hi) + dot(mid) + dot(lo)


def _rec_kernel(q_ref, k_ref, lf_ref, v_ref, sg_ref, og_ref, o_ref, st_ref, *, n_chunks):
    @pl.when(pl.program_id(2) == 0)
    def _():
        st_ref[...] = jnp.zeros_like(st_ref)

    ti = lax.broadcasted_iota(jnp.int32, (CHUNK, CHUNK), 0)
    si = lax.broadcasted_iota(jnp.int32, (CHUNK, CHUNK), 1)
    causal = si <= ti
    tri = jnp.where(causal, 1.0, 0.0).astype(BF16)
    og = og_ref[...]

    def rows(ci):
        return slice(ci * CHUNK, (ci + 1) * CHUNK)

    mean_w = jnp.full((LANES, LANES), 1.0 / LANES, BF16)

    def finish(outs):
        ms = [jnp.dot((o * o).astype(BF16), mean_w, preferred_element_type=F32) for o in outs]
        for ci, o in enumerate(outs):
            o = o * lax.rsqrt(ms[ci] + EPS) * og
            o_ref[rows(ci), :] = (o * sg_ref[rows(ci), :].astype(F32)).astype(BF16)

    cs = range(n_chunks)
    b = [_cumsum_rows(lf_ref[rows(ci), :], tri) for ci in cs]
    total = b[0][CHUNK - 1:CHUNK, :]
    for ci in range(1, n_chunks):
        total = jnp.minimum(total, b[ci][CHUNK - 1:CHUNK, :])
    mild = jnp.min(total) >= -MAX_FACTORED_DECAY
    q_dec = [(q_ref[rows(ci), :].astype(F32) * jnp.exp(b[ci])).astype(BF16) for ci in cs]
    decay = [jnp.exp(b[ci][CHUNK - 1:CHUNK, :]) for ci in cs]
    k_grow = [k_ref[rows(ci), :].astype(F32) * jnp.exp(-b[ci]) for ci in cs]
    scores = [jnp.where(causal, jnp.dot(q_dec[ci], k_grow[ci].T.astype(BF16), preferred_element_type=F32),
                        0.0).astype(BF16) for ci in cs]
    intra = [jnp.dot(scores[ci], v_ref[rows(ci), :], preferred_element_type=F32) for ci in cs]

    @pl.when(mild)
    def _():
        update = [_tn_dot(v_ref[rows(ci), :], (k_grow[ci] * decay[ci]).astype(BF16)) for ci in cs]
        states = [st_ref[...]]
        for ci in cs:
            states.append(states[ci] * decay[ci] + update[ci])
        finish([intra[ci] + jnp.dot(q_dec[ci], states[ci].T.astype(BF16), preferred_element_type=F32)
                for ci in cs])
        st_ref[...] = states[n_chunks]

    @pl.when(jnp.logical_not(mild))
    def _():
        state_t = st_ref[...]
        outs = []
        for ci in range(n_chunks):
            lf = lf_ref[rows(ci), :]
            q = q_ref[rows(ci), :].astype(F32)
            k = k_ref[rows(ci), :].astype(F32)
            v = v_ref[rows(ci), :]
            scores = _chunk_scores(q, k, b[ci], lf)
            o = jnp.dot(scores.astype(BF16), v, preferred_element_type=F32)
            o = o + _nt_dot(q_dec[ci], state_t.astype(BF16))
            k_dec = (k * jnp.exp(b[ci][CHUNK - 1:CHUNK, :] - b[ci])).astype(BF16)
            state_t = state_t * decay[ci] + _tn_dot(v, k_dec)
            outs.append(o)
        finish(outs)
        st_ref[...] = state_t


def _hg_rec(q, k, lf, v, sg, o_gain):
    bsz, n_heads, seq, _ = q.shape
    ts = _tile(seq, 4096)
    in_spec = pl.BlockSpec((None, None, ts, LANES), lambda b, h, s: (b, h, s, 0))
    return pl.pallas_call(
        functools.partial(_rec_kernel, n_chunks=ts // CHUNK),
        grid=(bsz, n_heads, seq // ts),
        in_specs=[in_spec] * 5 + [pl.BlockSpec((1, LANES), lambda b, h, s: (0, 0))],
        out_specs=pl.BlockSpec((None, ts, LANES), lambda b, h, s: (b, s, h)),
        out_shape=jax.ShapeDtypeStruct((bsz, seq, n_heads * LANES), BF16),
        scratch_shapes=[pltpu.VMEM((LANES, LANES), F32)],
        compiler_params=_params("parallel", "parallel", "arbitrary"),
        name="hg_rec",
    )(q, k, lf, v, sg, o_gain.reshape(1, LANES))


def _pair_sum_weights():
    row = lax.broadcasted_iota(jnp.int32, (QK_PAD, QK_PAD), 0)
    col = lax.broadcasted_iota(jnp.int32, (QK_PAD, QK_PAD), 1)
    return jnp.where((row < LANES) == (col < LANES), 1.0, 0.0).astype(BF16)


def _rope_pair(u):
    return u + pltpu.roll(u, ROPE, axis=1)


def _mla_kv_kernel(x_ref, gain_ref, shift_ref, scale_ref, wd_ref, lat_gain_ref, wu_ref,
                   gn_ref, gr_ref, cs_ref, k_ref, v_ref, *, n_heads, kv_lora):
    h = _norm_mod(x_ref[...], gain_ref[...], scale_ref[...], shift_ref[...]).astype(BF16)
    ckv = jnp.dot(h, wd_ref[...], preferred_element_type=F32)
    lat = ckv[:, :kv_lora]
    pe2 = ckv[:, kv_lora:]
    c_lat = (lat * lax.rsqrt(jnp.mean(lat * lat, axis=-1, keepdims=True) + EPS) * lat_gain_ref[...]).astype(BF16)
    pair_w = _pair_sum_weights()
    sq_pe = 0.5 * (pe2 * pe2)
    rot = _rope_pair(pe2 * gr_ref[...] * cs_ref[...])
    gn = gn_ref[...]

    def pair_kv(p):
        return [jnp.dot(c_lat, wu_ref[:, hh * QK_PAD:(hh + 1) * QK_PAD], preferred_element_type=F32)
                for hh in (2 * p, 2 * p + 1)]

    nxt = pair_kv(0)
    for p in range(n_heads // 2):
        kvs = nxt
        if p + 1 < n_heads // 2:
            nxt = pair_kv(p + 1)
        sq = jnp.concatenate([kv[:, :LANES] * kv[:, :LANES] + sq_pe for kv in kvs], axis=1)
        r2 = lax.rsqrt(jnp.dot(sq.astype(BF16), pair_w, preferred_element_type=F32) * (1.0 / QK) + EPS)
        for idx, kv in enumerate(kvs):
            hh = 2 * p + idx
            r = r2[:, idx * LANES:(idx + 1) * LANES]
            k_ref[hh, :, :LANES] = (kv[:, :LANES] * r * gn).astype(BF16)
            k_ref[hh, :, LANES:] = (rot * r).astype(BF16)
            v_ref[hh] = kv[:, LANES:].T.astype(BF16)


def _mla_kv(x, gain, mod, wd, lat_gain, wu, gn, gr, cs, bsz, seq):
    t, d = x.shape
    n_heads = d // LANES
    kv_lora = lat_gain.shape[0]
    tm = _tile(seq, ATTN_TILE)
    per_b = seq // tm
    mod_spec = lambda k: pl.BlockSpec((None, 1, d), lambda i: (i // per_b, 0, k))
    assert n_heads % 2 == 0, n_heads
    full = lambda a: pl.BlockSpec(a.shape, lambda i: (0,) * a.ndim)
    lat_gain = lat_gain.reshape(1, kv_lora)
    return pl.pallas_call(
        functools.partial(_mla_kv_kernel, n_heads=n_heads, kv_lora=kv_lora),
        grid=(t // tm,),
        in_specs=[
            pl.BlockSpec((tm, d), lambda i: (i, 0)),
            pl.BlockSpec((1, d), lambda i: (0, 0)),
            mod_spec(0), mod_spec(1),
            full(wd), full(lat_gain), full(wu), full(gn), full(gr),
            pl.BlockSpec((tm, LANES), lambda i: (i, 0)),
        ],
        out_specs=[
            pl.BlockSpec((None, n_heads, tm, QK_PAD), lambda i: (i // per_b, 0, i % per_b, 0)),
            pl.BlockSpec((None, n_heads, None, LANES, tm), lambda i: (i // per_b, 0, i % per_b, 0, 0)),
        ],
        out_shape=[jax.ShapeDtypeStruct((bsz, n_heads, seq, QK_PAD), BF16),
                   jax.ShapeDtypeStruct((bsz, n_heads, seq // tm, LANES, tm), BF16)],
        compiler_params=_params("parallel"),
        name="mla_kv",
    )(x, gain.reshape(1, d), mod, mod, wd, lat_gain, wu, gn, gr, cs)


def _mla_q_kernel(x_ref, gain_ref, shift_ref, scale_ref, wd_ref, lat_gain_ref, wu_ref,
                  gn_ref, gr_ref, cs_ref, q_ref, *, n_heads):
    h = _norm_mod(x_ref[...], gain_ref[...], scale_ref[...], shift_ref[...]).astype(BF16)
    cq = jnp.dot(h, wd_ref[...], preferred_element_type=F32)
    cq = (cq * lax.rsqrt(jnp.mean(cq * cq, axis=-1, keepdims=True) + EPS) * lat_gain_ref[...]).astype(BF16)
    gn = gn_ref[...]
    rope_tab = gr_ref[...] * cs_ref[...]
    pair_w = _pair_sum_weights()

    def pair_q(p):
        return [jnp.dot(cq, wu_ref[:, hh * QK_PAD:(hh + 1) * QK_PAD], preferred_element_type=F32)
                for hh in (2 * p, 2 * p + 1)]

    nxt = pair_q(0)
    for p in range(n_heads // 2):
        qhs = nxt
        if p + 1 < n_heads // 2:
            nxt = pair_q(p + 1)
        sq = jnp.concatenate([qh[:, :LANES] * qh[:, :LANES] + 0.5 * (qh[:, LANES:] * qh[:, LANES:])
                              for qh in qhs], axis=1)
        r2 = lax.rsqrt(jnp.dot(sq.astype(BF16), pair_w, preferred_element_type=F32) * (1.0 / QK) + EPS)
        for idx, qh in enumerate(qhs):
            hh = 2 * p + idx
            r = r2[:, idx * LANES:(idx + 1) * LANES]
            q_ref[hh, :LANES, :] = (qh[:, :LANES] * r * gn).T.astype(BF16)
            q_ref[hh, LANES:, :] = (qh[:, LANES:] * rope_tab * r).T.astype(BF16)


def _mla_q(x, gain, mod, wd, layer, lat_gain, wu, gn, gr, cs, bsz, seq):
    t, d = x.shape
    n_heads = d // LANES
    q_lora = lat_gain.shape[0]
    tm = _tile(seq, ATTN_TILE)
    per_b = seq // tm
    mod_spec = lambda k: pl.BlockSpec((None, 1, d), lambda i: (i // per_b, 0, k))
    assert n_heads % 2 == 0, n_heads
    full = lambda a: pl.BlockSpec(a.shape, lambda i: (0,) * a.ndim)
    lat_gain = lat_gain.reshape(1, q_lora)
    return pl.pallas_call(
        functools.partial(_mla_q_kernel, n_heads=n_heads),
        grid=(t // tm,),
        in_specs=[
            pl.BlockSpec((tm, d), lambda i: (i, 0)),
            pl.BlockSpec((1, d), lambda i: (0, 0)),
            mod_spec(0), mod_spec(1),
            pl.BlockSpec((None, d, q_lora), lambda i: (layer, 0, 0)),
            full(lat_gain), full(wu), full(gn), full(gr),
            pl.BlockSpec((tm, LANES), lambda i: (i, 0)),
        ],
        out_specs=pl.BlockSpec((None, n_heads, None, QK_PAD, tm), lambda i: (i // per_b, 0, i % per_b, 0, 0)),
        out_shape=jax.ShapeDtypeStruct((bsz, n_heads, seq // tm, QK_PAD, tm), BF16),
        compiler_params=_params("parallel"),
        name="mla_q",
    )(x, gain.reshape(1, d), mod, mod, wd, lat_gain, wu, gn, gr, cs)


def _flash_kernel(bound_ref, qt_ref, k_ref, vt_ref, o_ref, s_ref, m_ref, l_ref, acc_ref, *,
                  tq, n_tiles, heads_per_step):
    def tile_rows(j):
        return pl.ds(pl.multiple_of(j * tq, tq), tq)

    def scores(hh, i, j):
        return jnp.dot(k_ref[hh, tile_rows(j), :], qt_ref[hh, i], preferred_element_type=F32)

    bound = bound_ref[0, 0]

    def consume(hh, j, masked, fixed_shift):
        s = s_ref[hh]
        if masked:
            ki = lax.broadcasted_iota(jnp.int32, (tq, tq), 0)
            qi = lax.broadcasted_iota(jnp.int32, (tq, tq), 1)
            s = jnp.where(ki <= qi, s, -jnp.inf)
        if fixed_shift:
            p = jnp.exp2(s)
            l_ref[hh] += jnp.sum(p, axis=0, keepdims=True)
            acc_ref[hh] += jnp.dot(vt_ref[hh, j], p.astype(BF16), preferred_element_type=F32)
            return
        m = m_ref[hh]
        m_new = jnp.maximum(m, jnp.max(s, axis=0, keepdims=True))
        alpha = jnp.exp2(m - m_new)
        p = jnp.exp2(s - m_new)
        m_ref[hh] = m_new
        l_ref[hh] = alpha * l_ref[hh] + jnp.sum(p, axis=0, keepdims=True)
        pv = jnp.dot(vt_ref[hh, j], p.astype(BF16), preferred_element_type=F32)
        acc_ref[hh] = alpha * acc_ref[hh] + pv

    def attend(fixed_shift):
        for hh in range(heads_per_step):
            s_ref[hh] = scores(hh, 0, 0)

        def query_tile(i, carry):
            m_ref[...] = jnp.full(m_ref.shape, -jnp.inf, F32)
            l_ref[...] = jnp.zeros_like(l_ref)
            acc_ref[...] = jnp.zeros_like(acc_ref)

            def key_tile(j, c):
                for hh in range(heads_per_step):
                    s_next = scores(hh, i, j + 1)
                    consume(hh, j, False, fixed_shift)
                    s_ref[hh] = s_next
                return c

            lax.fori_loop(0, i, key_tile, 0)
            i_next = jnp.minimum(i + 1, n_tiles - 1)
            for hh in range(heads_per_step):
                s_next = scores(hh, i_next, 0)
                consume(hh, i, True, fixed_shift)
                s_ref[hh] = s_next
                o_ref[tile_rows(i), hh * LANES:(hh + 1) * LANES] = (acc_ref[hh] / l_ref[hh]).T.astype(BF16)
            return carry

        lax.fori_loop(0, n_tiles, query_tile, 0)

    @pl.when(bound <= MAX_FIXED_SOFTMAX_SHIFT)
    def _():
        attend(True)

    @pl.when(jnp.logical_not(bound <= MAX_FIXED_SOFTMAX_SHIFT))
    def _():
        attend(False)


def _flash(qt, k, vt, score_bound):
    bsz, n_heads, seq, _ = k.shape
    tq = vt.shape[-1]
    hps = 4 if n_heads % 4 == 0 else (2 if n_heads % 2 == 0 else 1)
    return pl.pallas_call(
        functools.partial(_flash_kernel, tq=tq, n_tiles=seq // tq, heads_per_step=hps),
        grid=(bsz, n_heads // hps),
        in_specs=[
            pl.BlockSpec(memory_space=pltpu.SMEM),
            pl.BlockSpec((None, hps, seq // tq, QK_PAD, tq), lambda b, h: (b, h, 0, 0, 0)),
            pl.BlockSpec((None, hps, seq, QK_PAD), lambda b, h: (b, h, 0, 0)),
            pl.BlockSpec((None, hps, seq // tq, LANES, tq), lambda b, h: (b, h, 0, 0, 0)),
        ],
        out_specs=pl.BlockSpec((None, seq, hps * LANES), lambda b, h: (b, 0, h)),
        out_shape=jax.ShapeDtypeStruct((bsz, seq, n_heads * LANES), BF16),
        scratch_shapes=[pltpu.VMEM((hps, tq, tq), F32),
                        pltpu.VMEM((hps, 1, tq), F32), pltpu.VMEM((hps, 1, tq), F32),
                        pltpu.VMEM((hps, LANES, tq), F32)],
        compiler_params=_params("parallel", "parallel"),
        name="flash",
    )(score_bound, qt, k, vt)


def _swap_halves(a):
    half = a.shape[-1] // 2
    return jnp.concatenate([a[..., half:], a[..., :half]], axis=-1)


def _rope_gain(g):
    return jnp.concatenate([g, _swap_halves(g)]).reshape(1, LANES)


def kernel(x, c, positions, ada_w, ada_b, norm_mix, norm_mlp, mlp_w1, mlp_w2, hg_w_in, hg_lower, hg_o_norm, hg_w_out, kv_ada_w, kv_ada_b, kv_in_norm, mla_w_dkv, mla_kv_norm, mla_w_ukv, mla_k_norm, mla_w_dq, mla_q_lat_norm, mla_w_uq, mla_q_norm, mla_w_o):
    bsz, seq, d = x.shape
    depth = ada_w.shape[0]
    n_a = hg_w_in.shape[0]
    n_heads = d // LANES
    kv_lora = mla_kv_norm.shape[0]
    q_lora = mla_w_dq.shape[2]
    t = bsz * seq

    mod = _ada(c, ada_w, ada_b).reshape(depth, bsz, 1, 6 * d)
    kv_mod = _ada(c, kv_ada_w[None], kv_ada_b[None]).reshape(bsz, 1, 2 * d)

    inv_freq = 1.0 / (ROPE_THETA ** (jnp.arange(0, ROPE, 2, dtype=F32) / ROPE))
    ang = positions.astype(F32)[..., None] * inv_freq
    cos, sin = jnp.cos(ang), jnp.sin(ang)
    cs = jnp.concatenate([cos, cos, -sin, sin], axis=-1).reshape(t, LANES)

    lb_all = jnp.cumsum(jax.nn.softmax(hg_lower.astype(F32), axis=0), axis=0)[:n_a]

    hg_w_in, hg_w_out, mla_w_dq, mla_w_o, mlp_w1, mlp_w2 = (
        w.astype(BF16) for w in (hg_w_in, hg_w_out, mla_w_dq, mla_w_o, mlp_w1, mlp_w2))

    xf = x.reshape(t, d)
    k_sh = v_sh = None
    for layer in range(depth):
        if layer == n_a:
            wd = jnp.concatenate([mla_w_dkv, _swap_halves(mla_w_dkv[:, kv_lora:])], axis=1).astype(BF16)
            k_sh, v_sh = _mla_kv(xf, kv_in_norm, kv_mod, wd, mla_kv_norm, mla_w_ukv.astype(BF16),
                                 mla_k_norm[:LANES].reshape(1, LANES), _rope_gain(mla_k_norm[LANES:]),
                                 cs, bsz, seq)
        if layer < n_a:
            q, k, lf, v, sg = _hg_in(xf, norm_mix[layer], mod[layer], lb_all[layer],
                                     hg_w_in, layer, bsz, seq)
            y = _hg_rec(q, k, lf, v, sg, hg_o_norm[layer]).reshape(t, d)
            xf = _proj_res(y, hg_w_out, layer, xf, mod[layer], 2, seq)
        else:
            j = layer - n_a
            wu = mla_w_uq[j].reshape(q_lora, n_heads, QK)
            wu = jnp.concatenate([wu, _swap_halves(wu[..., LANES:])], axis=-1)
            wu = wu.reshape(q_lora, n_heads * QK_PAD).astype(BF16)
            sm_scale = float(QK) ** -0.5 * LOG2E
            qg = mla_q_norm[j] * sm_scale
            qh = _mla_q(xf, norm_mix[layer], mod[layer], mla_w_dq, j, mla_q_lat_norm[j], wu,
                        qg[:LANES].reshape(1, LANES), _rope_gain(qg[LANES:]), cs, bsz, seq)
            score_bound = (1.03 * QK * 2.0 ** 0.5) * jnp.max(jnp.abs(qg)) * jnp.max(jnp.abs(mla_k_norm))
            y = _flash(qh, k_sh, v_sh, score_bound.reshape(1, 1).astype(F32)).reshape(t, d)
            xf = _proj_res(y, mla_w_o, j, xf, mod[layer], 2, seq)
        xf = _mlp(xf, norm_mlp[layer], mod[layer], mlp_w1, mlp_w2, layer, seq)
    return xf.reshape(bsz, seq, d)
```
